```python
import math
import jax, jax.numpy as jnp
from jax import lax
import numpy as np

D_MODEL = 1024
BATCH = 8
SEQ = 4096
DEPTH = 2
DEC_BATCH = 8
DEC_SEQ = 2048
PAST_LEN = 128

GRID_W = 64
N_BRANCH = 4
BRANCH_W = 512
DN_HEADS = 4
DN_DK = 128
DN_DV = 128
DN_QKV = 2 * DN_HEADS * DN_DK + DN_HEADS * DN_DV
DN_CONV = 4
DN_CHUNK = 64
S5_WIDTH = 512
S5_P = 16
S5_GROUPS = S5_WIDTH // S5_P
S5_N = 64
NA_HEADS = 8
NA_DH = 64
NA_WIDTH = NA_HEADS * NA_DH
NA_KH_MAX = 8
NA_KW = 16
LRU_WIDTH = 512
LRU_BLOCKS = 8
LRU_BW = LRU_WIDTH // LRU_BLOCKS
LRU_CONV = 4
LRU_C = 8.0
D_FF = 4 * D_MODEL
ALPHA = float((2 * DEPTH) ** 0.25)
BETA_INIT = float((8 * DEPTH) ** -0.25)
EPS = 1e-5

IN_SPLITS = (DN_HEADS * DN_DK, DN_HEADS * DN_DK, DN_HEADS * DN_DV, DN_HEADS * DN_DV, 2 * DN_HEADS, 2 * DN_HEADS,
             S5_WIDTH, NA_WIDTH, NA_WIDTH, NA_WIDTH, LRU_WIDTH, LRU_WIDTH, N_BRANCH * D_MODEL)
N_IN = sum(IN_SPLITS)

kernel_name = 'hybrid_bidir_encoder_deltanet_s5_natten_rglru'


def _split_columns(h):
    parts = []
    start = 0
    for w in IN_SPLITS:
        parts.append(h[..., start:start + w])
        start += w
    return parts


def _layer_norm(x, g, b):
    xf = x.astype(jnp.float32)
    mu = jnp.mean(xf, axis=-1, keepdims=True)
    var = jnp.mean(jnp.square(xf - mu), axis=-1, keepdims=True)
    y = (xf - mu) * lax.rsqrt(var + EPS) * g.astype(jnp.float32) + b.astype(jnp.float32)
    return y.astype(x.dtype)


def _l2norm(x):
    return x * lax.rsqrt(jnp.sum(jnp.square(x), axis=-1, keepdims=True) + 1e-6)


def _dwconv_centred(x, w):
    k = w.shape[0]
    left = (k - 1) // 2
    return lax.conv_general_dilated(x, w[:, None, :].astype(x.dtype), window_strides=(1,),
                                    padding=[(left, k - 1 - left)],
                                    dimension_numbers=('NWC', 'WIO', 'NWC'),
                                    feature_group_count=x.shape[-1])


def _lin_op(e1, e2):
    a1, b1 = e1
    a2, b2 = e2
    return a1 * a2, a2 * b1 + b2


def _cmul(ar, ai, br, bi):
    return ar * br - ai * bi, ar * bi + ai * br


def _gated_delta_chunked(q, k, v, g, beta):
    b_, t_, h_, dk = q.shape
    dv = v.shape[-1]
    n_c = t_ // DN_CHUNK

    def chunks(a):
        return jnp.moveaxis(a.reshape((b_, n_c, DN_CHUNK, h_) + a.shape[3:]), 3, 2)

    q, k, v, g, beta = chunks(q), chunks(k), chunks(v), chunks(g), chunks(beta)
    q = q * (dk ** -0.5)
    gc = jnp.cumsum(g, axis=-1)
    idx = jnp.arange(DN_CHUNK)
    incl = idx[:, None] >= idx[None, :]
    strict = idx[:, None] > idx[None, :]
    decay = jnp.exp(jnp.where(incl, gc[..., :, None] - gc[..., None, :], -jnp.inf))
    kb = k * beta[..., None]
    vb = v * beta[..., None]
    lmat = jnp.where(strict, jnp.einsum('bnhid,bnhjd->bnhij', kb, k) * decay, 0.0)
    amat = lmat + jnp.eye(DN_CHUNK, dtype=lmat.dtype)
    rhs = jnp.concatenate([vb, kb * jnp.exp(gc)[..., None]], axis=-1)
    sol = lax.linalg.triangular_solve(amat, rhs, left_side=True, lower=True, unit_diagonal=True)
    u, w = sol[..., :dv], sol[..., dv:]
    qk = jnp.where(incl, jnp.einsum('bnhid,bnhjd->bnhij', q, k) * decay, 0.0)
    g_last = gc[..., -1]
    k_dec = k * jnp.exp(g_last[..., None] - gc)[..., None]
    q_dec = q * jnp.exp(gc)[..., None]

    def step(s, xs):
        qd, kd, uu, ww, qkm, gl = xs
        v_new = uu - jnp.einsum('bhck,bhkv->bhcv', ww, s)
        o = jnp.einsum('bhck,bhkv->bhcv', qd, s) + jnp.einsum('bhij,bhjv->bhiv', qkm, v_new)
        s = s * jnp.exp(gl)[..., None, None] + jnp.einsum('bhck,bhcv->bhkv', kd, v_new)
        return s, o

    xs = (jnp.moveaxis(q_dec, 1, 0), jnp.moveaxis(k_dec, 1, 0), jnp.moveaxis(u, 1, 0),
          jnp.moveaxis(w, 1, 0), jnp.moveaxis(qk, 1, 0), jnp.moveaxis(g_last, 1, 0))
    s0 = jnp.zeros((b_, h_, dk, dv), jnp.float32)
    _, o = lax.scan(step, s0, xs)
    return jnp.moveaxis(o, 0, 1).transpose(0, 1, 3, 2, 4).reshape(b_, t_, h_, dv)


def _mixer_deltanet(q, k, v, z, a_in, b_in, conv_w, a_log, dt_bias, norm_g):
    bsz, t, _ = q.shape
    qkv = jax.nn.silu(_dwconv_centred(jnp.concatenate([q, k, v], axis=-1), conv_w)).astype(jnp.float32)
    qf = qkv[..., :DN_HEADS * DN_DK].reshape(bsz, t, DN_HEADS, DN_DK)
    kf = qkv[..., DN_HEADS * DN_DK:2 * DN_HEADS * DN_DK].reshape(bsz, t, DN_HEADS, DN_DK)
    vf = qkv[..., 2 * DN_HEADS * DN_DK:].reshape(bsz, t, DN_HEADS, DN_DV)
    qf, kf = _l2norm(qf), _l2norm(kf)
    a_in = a_in.astype(jnp.float32).reshape(bsz, t, 2, DN_HEADS)
    b_in = b_in.astype(jnp.float32).reshape(bsz, t, 2, DN_HEADS)
    g = -jnp.exp(a_log.astype(jnp.float32)) * jax.nn.softplus(a_in + dt_bias.astype(jnp.float32))
    beta = jax.nn.sigmoid(b_in)
    o_f = _gated_delta_chunked(qf, kf, vf, g[:, :, 0], beta[:, :, 0])
    fl = lambda a: jnp.flip(a, axis=1)
    o_b = fl(_gated_delta_chunked(fl(qf), fl(kf), fl(vf), fl(g[:, :, 1]), fl(beta[:, :, 1])))
    o = o_f + o_b
    o = o * lax.rsqrt(jnp.mean(jnp.square(o), axis=-1, keepdims=True) + EPS) * norm_g.astype(jnp.float32)
    o = o * jax.nn.silu(z.astype(jnp.float32).reshape(bsz, t, DN_HEADS, DN_DV))
    return o.reshape(bsz, t, DN_HEADS * DN_DV)


def _s5_discretise(lam_re, lam_im, log_dt, b_re, b_im):
    dt = jnp.exp(log_dt)[:, None]
    mag = jnp.exp(lam_re * dt)
    ab_re = mag * jnp.cos(lam_im * dt)
    ab_im = mag * jnp.sin(lam_im * dt)
    den = jnp.square(lam_re) + jnp.square(lam_im)
    nr, ni = ab_re - 1.0, ab_im
    kr = (nr * lam_re + ni * lam_im) / den
    ki = (ni * lam_re - nr * lam_im) / den
    bb_re, bb_im = _cmul(kr[..., None], ki[..., None], b_re, b_im)
    return ab_re, ab_im, bb_re, bb_im


def _s5_scan(u, ab_re, ab_im, bb_re, bb_im, reverse):
    bu_re = jnp.einsum('btgp,gnp->btgn', u, bb_re)
    bu_im = jnp.einsum('btgp,gnp->btgn', u, bb_im)
    a_re = jnp.broadcast_to(ab_re, bu_re.shape)
    a_im = jnp.broadcast_to(ab_im, bu_re.shape)

    def op(e1, e2):
        a1r, a1i, b1r, b1i = e1
        a2r, a2i, b2r, b2i = e2
        ar, ai = _cmul(a2r, a2i, a1r, a1i)
        br, bi = _cmul(a2r, a2i, b1r, b1i)
        return ar, ai, br + b2r, bi + b2i

    _, _, h_re, h_im = lax.associative_scan(op, (a_re, a_im, bu_re, bu_im), axis=1, reverse=reverse)
    return h_re, h_im


def _mixer_s5(u, lam_re, lam_im, log_dt, b_re, b_im, c_re, c_im, d_skip, glu_w, glu_b):
    bsz, t, _ = u.shape
    f32 = jnp.float32
    uf = u.astype(f32).reshape(bsz, t, S5_GROUPS, S5_P)
    disc_f = _s5_discretise(lam_re[0].astype(f32), lam_im[0].astype(f32), log_dt[0].astype(f32), b_re.astype(f32), b_im.astype(f32))
    disc_b = _s5_discretise(lam_re[1].astype(f32), lam_im[1].astype(f32), log_dt[1].astype(f32), b_re.astype(f32), b_im.astype(f32))
    hf_re, hf_im = _s5_scan(uf, *disc_f, reverse=False)
    hb_re, hb_im = _s5_scan(uf, *disc_b, reverse=True)
    h_re = hf_re + hb_re
    h_im = hf_im + hb_im
    y = (jnp.einsum('btgn,gpn->btgp', h_re, c_re.astype(f32)) - jnp.einsum('btgn,gpn->btgp', h_im, c_im.astype(f32))
         + d_skip.astype(f32).reshape(S5_GROUPS, S5_P) * uf)
    y = jax.nn.gelu(y.reshape(bsz, t, S5_WIDTH))
    return y * jax.nn.sigmoid(jnp.matmul(y, glu_w.astype(f32)) + glu_b.astype(f32))


def _mixer_natten(q, k, v, rpb):
    bsz, t, _ = q.shape
    rows = t // GRID_W
    kh = min(NA_KH_MAX, rows)
    shp = (bsz, rows, GRID_W, NA_HEADS, NA_DH)
    q = q.reshape(shp) * (NA_DH ** -0.5)
    k = k.reshape(shp)
    v = v.reshape(shp)
    r = jnp.arange(rows)
    r0 = jnp.clip(r - kh // 2, 0, rows - kh)
    row_idx = r0[:, None] + jnp.arange(kh)[None, :]
    k_rows = k[:, row_idx]
    v_rows = v[:, row_idx]
    c = jnp.arange(GRID_W)
    c0 = jnp.clip(c - NA_KW // 2, 0, GRID_W - NA_KW)
    col_in = (c[None, :] >= c0[:, None]) & (c[None, :] < c0[:, None] + NA_KW)
    dr = row_idx - r[:, None] + (NA_KH_MAX - 1)
    dc = jnp.clip(c[None, :] - c[:, None], -(NA_KW - 1), NA_KW - 1) + (NA_KW - 1)
    bias = rpb[:, dr[:, None, :, None], dc[None, :, None, :]]
    s = jnp.einsum('brqhd,brjkhd->bhrqjk', q, k_rows).astype(jnp.float32) + bias.astype(jnp.float32)
    s = jnp.where(col_in[:, None, :], s, -1e30)
    p = jax.nn.softmax(s, axis=(-2, -1)).astype(v.dtype)
    o = jnp.einsum('bhrqjk,brjkhd->brqhd', p, v_rows)
    return o.reshape(bsz, t, NA_WIDTH).astype(jnp.float32)


def _mixer_rglru(xb, gate_in, conv_w, conv_b, gate_w, gate_b, lam):
    bsz, t, _ = xb.shape
    f32 = jnp.float32
    xc = (_dwconv_centred(xb, conv_w) + conv_b).astype(f32)
    xblk = xc.reshape(bsz, t, LRU_BLOCKS, LRU_BW)
    gates = jnp.einsum('btnc,dgncm->btdgnm', xblk, gate_w.astype(f32)).reshape(bsz, t, 2, 2, LRU_WIDTH)
    gates = jax.nn.sigmoid(gates + gate_b.astype(f32))
    log_a = -LRU_C * gates[:, :, :, 0] * jax.nn.softplus(-lam.astype(f32))
    a = jnp.exp(log_a)
    b = jnp.sqrt(-jnp.expm1(2.0 * log_a)) * gates[:, :, :, 1] * xc[:, :, None, :]
    _, h_f = lax.associative_scan(_lin_op, (a[:, :, 0], b[:, :, 0]), axis=1, reverse=False)
    _, h_b = lax.associative_scan(_lin_op, (a[:, :, 1], b[:, :, 1]), axis=1, reverse=True)
    return (h_f + h_b) * jax.nn.gelu(gate_in.astype(f32))


def _layer(x, w_in, dn_conv_w, dn_a_log, dn_dt_bias, dn_norm_g, s5_lambda_re, s5_lambda_im, s5_log_dt,
           s5_b_re, s5_b_im, s5_c_re, s5_c_im, s5_d, s5_glu_w, s5_glu_b, na_rpb, lru_conv_w, lru_conv_b,
           lru_gate_w, lru_gate_b, lru_lambda, w_branch, w_out, ln1_g, ln1_b, mlp_w1, mlp_b1, mlp_w2, mlp_b2,
           ln2_g, ln2_b):
    bsz, t, _ = x.shape
    h = jnp.matmul(x, w_in)
    dq, dk, dv, dz, da, db, su, nq, nk, nv, lx, lg, gt = _split_columns(h)
    y_a = _mixer_deltanet(dq, dk, dv, dz, da, db, dn_conv_w, dn_a_log, dn_dt_bias, dn_norm_g)
    y_b = _mixer_s5(su, s5_lambda_re, s5_lambda_im, s5_log_dt, s5_b_re, s5_b_im, s5_c_re, s5_c_im, s5_d, s5_glu_w, s5_glu_b)
    y_c = _mixer_natten(nq, nk, nv, na_rpb)
    y_d = _mixer_rglru(lx, lg, lru_conv_w, lru_conv_b, lru_gate_w, lru_gate_b, lru_lambda)
    ys = jnp.stack([y_a, y_b, y_c, y_d], axis=2).astype(x.dtype)
    proj = jnp.einsum('btnc,ncd->btnd', ys, w_branch)
    gates = jax.nn.sigmoid(gt.reshape(bsz, t, N_BRANCH, D_MODEL))
    mix = jnp.matmul(jnp.einsum('btnd,btnd->btd', gates, proj), w_out)
    x = _layer_norm(ALPHA * x + mix, ln1_g, ln1_b)
    f = jnp.square(jax.nn.relu(jnp.matmul(x, mlp_w1) + mlp_b1))
    f = jnp.matmul(f, mlp_w2) + mlp_b2
    return _layer_norm(ALPHA * x + f, ln2_g, ln2_b)


def _trunk(x, ln_in_g, ln_in_b, layer_weights):
    x = _layer_norm(x, ln_in_g, ln_in_b)
    for l in range(DEPTH):
        x = _layer(x, *[w[l] for w in layer_weights])
    return x


def setup_inputs(seed: int = 0) -> dict:
    key = jax.random.key(seed)
    ks = iter(jax.random.split(key, 48))

    def nrm(shape, scale):
        return scale * jax.random.normal(next(ks), shape, jnp.float32)

    def unif(shape, lo, hi):
        return jax.random.uniform(next(ks), shape, jnp.float32, lo, hi)

    L = DEPTH
    x_prompt = nrm((BATCH, SEQ, D_MODEL), 1.0)
    x_sample = nrm((DEC_BATCH, DEC_SEQ, D_MODEL), 1.0)
    ln_in_g = 1.0 + nrm((D_MODEL,), 0.01)
    ln_in_b = nrm((D_MODEL,), 0.01)
    w_in = nrm((L, D_MODEL, N_IN), D_MODEL ** -0.5)
    dn_conv_w = nrm((L, DN_CONV, DN_QKV), DN_CONV ** -0.5)
    dn_a_log = jnp.log(unif((L, 2, DN_HEADS), 1.0, 16.0))
    dt = jnp.exp(unif((L, 2, DN_HEADS), math.log(1e-3), math.log(1e-1)))
    dn_dt_bias = dt + jnp.log(-jnp.expm1(-dt))
    dn_norm_g = 1.0 + nrm((L, DN_DV), 0.01)
    n_idx = jnp.arange(S5_N, dtype=jnp.float32)
    s5_lambda_re = -0.5 + nrm((L, 2, S5_GROUPS, S5_N), 0.01)
    s5_lambda_im = math.pi * n_idx + nrm((L, 2, S5_GROUPS, S5_N), 0.01)
    s5_log_dt = unif((L, 2, S5_GROUPS), math.log(1e-3), math.log(1e-1))
    s5_b_re = nrm((L, S5_GROUPS, S5_N, S5_P), (2.0 * S5_P) ** -0.5)
    s5_b_im = nrm((L, S5_GROUPS, S5_N, S5_P), (2.0 * S5_P) ** -0.5)
    s5_c_re = nrm((L, S5_GROUPS, S5_P, S5_N), (2.0 * S5_N) ** -0.5)
    s5_c_im = nrm((L, S5_GROUPS, S5_P, S5_N), (2.0 * S5_N) ** -0.5)
    s5_d = nrm((L, S5_WIDTH), 1.0)
    s5_glu_w = nrm((L, S5_WIDTH, S5_WIDTH), S5_WIDTH ** -0.5)
    s5_glu_b = nrm((L, S5_WIDTH), 0.01)
    na_rpb = nrm((L, NA_HEADS, 2 * NA_KH_MAX - 1, 2 * NA_KW - 1), 0.02)
    lru_conv_w = nrm((L, LRU_CONV, LRU_WIDTH), LRU_CONV ** -0.5)
    lru_conv_b = nrm((L, LRU_WIDTH), 0.01)
    lru_gate_w = nrm((L, 2, 2, LRU_BLOCKS, LRU_BW, LRU_BW), LRU_BW ** -0.5)
    lru_gate_b = nrm((L, 2, 2, LRU_WIDTH), 0.01)
    a_pow = unif((L, 2, LRU_WIDTH), 0.9, 0.999)
    a_base = a_pow ** (1.0 / LRU_C)
    lru_lambda = jnp.log(a_base) - jnp.log1p(-a_base)
    w_branch = nrm((L, N_BRANCH, BRANCH_W, D_MODEL), BETA_INIT * BRANCH_W ** -0.5)
    w_out = nrm((L, D_MODEL, D_MODEL), BETA_INIT * D_MODEL ** -0.5)
    ln1_g = 1.0 + nrm((L, D_MODEL), 0.01)
    ln1_b = nrm((L, D_MODEL), 0.01)
    mlp_w1 = nrm((L, D_MODEL, D_FF), D_MODEL ** -0.5)
    mlp_b1 = nrm((L, D_FF), 0.01)
    mlp_w2 = nrm((L, D_FF, D_MODEL), BETA_INIT * D_FF ** -0.5)
    mlp_b2 = nrm((L, D_MODEL), 0.01)
    ln2_g = 1.0 + nrm((L, D_MODEL), 0.01)
    ln2_b = nrm((L, D_MODEL), 0.01)
    return {'x_prompt': x_prompt, 'x_sample': x_sample, 'ln_in_g': ln_in_g, 'ln_in_b': ln_in_b, 'w_in': w_in,
            'dn_conv_w': dn_conv_w, 'dn_a_log': dn_a_log, 'dn_dt_bias': dn_dt_bias, 'dn_norm_g': dn_norm_g,
            's5_lambda_re': s5_lambda_re, 's5_lambda_im': s5_lambda_im, 's5_log_dt': s5_log_dt,
            's5_b_re': s5_b_re, 's5_b_im': s5_b_im, 's5_c_re': s5_c_re, 's5_c_im': s5_c_im, 's5_d': s5_d,
            's5_glu_w': s5_glu_w, 's5_glu_b': s5_glu_b, 'na_rpb': na_rpb, 'lru_conv_w': lru_conv_w,
            'lru_conv_b': lru_conv_b, 'lru_gate_w': lru_gate_w, 'lru_gate_b': lru_gate_b, 'lru_lambda': lru_lambda,
            'w_branch': w_branch, 'w_out': w_out, 'ln1_g': ln1_g, 'ln1_b': ln1_b, 'mlp_w1': mlp_w1,
            'mlp_b1': mlp_b1, 'mlp_w2': mlp_w2, 'mlp_b2': mlp_b2, 'ln2_g': ln2_g, 'ln2_b': ln2_b}


def reference(x_prompt, x_sample, ln_in_g, ln_in_b, w_in, dn_conv_w, dn_a_log, dn_dt_bias, dn_norm_g,
              s5_lambda_re, s5_lambda_im, s5_log_dt, s5_b_re, s5_b_im, s5_c_re, s5_c_im, s5_d, s5_glu_w, s5_glu_b,
              na_rpb, lru_conv_w, lru_conv_b, lru_gate_w, lru_gate_b, lru_lambda, w_branch, w_out, ln1_g, ln1_b,
              mlp_w1, mlp_b1, mlp_w2, mlp_b2, ln2_g, ln2_b):
    layer_weights = (w_in, dn_conv_w, dn_a_log, dn_dt_bias, dn_norm_g, s5_lambda_re, s5_lambda_im, s5_log_dt,
                     s5_b_re, s5_b_im, s5_c_re, s5_c_im, s5_d, s5_glu_w, s5_glu_b, na_rpb, lru_conv_w, lru_conv_b,
                     lru_gate_w, lru_gate_b, lru_lambda, w_branch, w_out, ln1_g, ln1_b, mlp_w1, mlp_b1, mlp_w2,
                     mlp_b2, ln2_g, ln2_b)
    y_prompt = _trunk(x_prompt, ln_in_g, ln_in_b, layer_weights)
    y_sample = _trunk(x_sample, ln_in_g, ln_in_b, layer_weights)
    return (y_prompt, y_sample)
```

```python
import functools
import math

import jax
import jax.numpy as jnp
from jax import lax
from jax.experimental import pallas as pl
from jax.experimental.pallas import tpu as pltpu

F32 = jnp.float32
BF16 = jnp.bfloat16

D_MODEL = 1024
DEPTH = 2
GRID_W = 64
N_BRANCH = 4
BRANCH_W = 512
DN_HEADS = 4
DN_DK = 128
DN_DV = 128
DN_CONV = 4
DN_CHUNK = 64
S5_WIDTH = 512
S5_P = 16
S5_GROUPS = S5_WIDTH // S5_P
S5_N = 64
NA_HEADS = 8
NA_DH = 64
NA_WIDTH = NA_HEADS * NA_DH
NA_KH = 8
NA_KW = 16
LRU_WIDTH = 512
LRU_BLOCKS = 8
LRU_BW = LRU_WIDTH // LRU_BLOCKS
LRU_CONV = 4
LRU_C = 8.0
D_FF = 4 * D_MODEL
ALPHA = float((2 * DEPTH) ** 0.25)
EPS = 1e-5

_IN_SPLITS = (512, 512, 512, 512, 8, 8, 512, 512, 512, 512, 512, 512, 4096)
_IN_OFFS = tuple(sum(_IN_SPLITS[:i]) for i in range(len(_IN_SPLITS)))

LANE = 128
SUBLANE = 8
C_QKV = 0
C_Z = 1536
C_SU = 2048
C_LX = 2560
C_LG = 3072
C_AB = 3584
W_MAIN = 3712

S5_L = 64
S5_LW = S5_L * S5_P
S5_HW = 4 * LANE

VMEM_LIMIT = 56 * 1024 * 1024


def _cparams(sem):
    return pltpu.CompilerParams(dimension_semantics=sem, vmem_limit_bytes=VMEM_LIMIT)


def _layer_norm(x, g, b):
    mu = jnp.mean(x, axis=-1, keepdims=True)
    xc = x - mu
    var = jnp.mean(xc * xc, axis=-1, keepdims=True)
    return xc * lax.rsqrt(var + EPS) * g + b


def _const_spec(shape):
    nd = len(shape)
    return pl.BlockSpec(shape, lambda *_: (0,) * nd, pipeline_mode=pl.Buffered(1))


def _ln_kernel(x_ref, g_ref, b_ref, o_ref):
    o_ref[...] = _layer_norm(x_ref[...], g_ref[...], b_ref[...])


def _input_ln(x, g, b, tm=512):
    n = x.shape[0]
    return pl.pallas_call(
        _ln_kernel,
        grid=(n // tm,),
        in_specs=[pl.BlockSpec((tm, D_MODEL), lambda i: (i, 0)), _const_spec((1, D_MODEL)), _const_spec((1, D_MODEL))],
        out_specs=pl.BlockSpec((tm, D_MODEL), lambda i: (i, 0)),
        out_shape=jax.ShapeDtypeStruct((n, D_MODEL), F32),
        compiler_params=_cparams(("parallel",)),
        name="input_ln",
    )(x, g.reshape(1, -1), b.reshape(1, -1))


def _proj_kernel(x_ref, wm_ref, wn_ref, hm_ref, na_ref):
    xb = x_ref[...].astype(BF16)
    step = 4 * LANE
    for c0 in range(0, W_MAIN, step):
        c1 = min(c0 + step, W_MAIN)
        hm_ref[:, c0:c1] = jnp.dot(xb, wm_ref[:, c0:c1], preferred_element_type=F32)
    for c0 in range(0, 3 * NA_WIDTH, step):
        na_ref[:, c0:c0 + step] = jnp.dot(xb, wn_ref[:, c0:c0 + step], preferred_element_type=F32).astype(BF16)


def _in_proj(x, w_main, w_na, tm=256):
    n = x.shape[0]
    return pl.pallas_call(
        _proj_kernel,
        grid=(n // tm,),
        in_specs=[pl.BlockSpec((tm, D_MODEL), lambda i: (i, 0)),
                  _const_spec((D_MODEL, W_MAIN)), _const_spec((D_MODEL, 3 * NA_WIDTH))],
        out_specs=[pl.BlockSpec((tm, W_MAIN), lambda i: (i, 0)), pl.BlockSpec((tm, 3 * NA_WIDTH), lambda i: (i, 0))],
        out_shape=[jax.ShapeDtypeStruct((n, W_MAIN), F32), jax.ShapeDtypeStruct((n, 3 * NA_WIDTH), BF16)],
        compiler_params=_cparams(("parallel",)),
        name="in_proj",
    )(x, w_main, w_na)


def _conv_centred(prev8, cur, next8, w, first, last):
    tb = cur.shape[0]
    prev8 = jnp.where(first, 0.0, prev8)
    next8 = jnp.where(last, 0.0, next8)
    xp = jnp.concatenate([prev8, cur, next8], axis=0)
    left = (w.shape[0] - 1) // 2
    acc = None
    for j in range(w.shape[0]):
        s = SUBLANE - left + j
        term = xp[s:s + tb] * w[j:j + 1]
        acc = term if acc is None else acc + term
    return acc


def _dot_hi(a, b):
    return jnp.dot(a, b, preferred_element_type=F32, precision=lax.Precision.HIGHEST)


def _dot_bf(a, b):
    return jnp.dot(a.astype(BF16), b.astype(BF16), preferred_element_type=F32)


def _dot_nt_bf(a, b):
    return lax.dot_general(a.astype(BF16), b.astype(BF16), (((1,), (1,)), ((), ())), preferred_element_type=F32)


def _dn_direction(d, cur, prev8, next8, ab, first, last, cw, alog, dtb, s_ref, o_ref):
    tb = cur.shape[0]
    c = DN_CHUNK
    qkv = _conv_centred(prev8, cur, next8, cw, first, last)
    qkv = qkv * jax.nn.sigmoid(qkv)
    gates = -jnp.exp(alog) * jax.nn.softplus(ab + dtb)
    betas = jax.nn.sigmoid(ab)

    row = lax.broadcasted_iota(jnp.int32, (c, c), 0)
    col = lax.broadcasted_iota(jnp.int32, (c, c), 1)
    if d == 0:
        incl = col <= row
        strict = col < row
    else:
        incl = col >= row
        strict = col > row
    lm = incl.astype(F32)
    lmt = (row <= col if d == 0 else row >= col).astype(F32)
    ones = jnp.ones((c, c), F32)
    eye = (row == col).astype(F32)
    edge = c - 1 if d == 0 else 0

    n_chunks = tb // c
    order = range(n_chunks) if d == 0 else range(n_chunks - 1, -1, -1)
    for ci in order:
        r0 = ci * c
        for h in range(DN_HEADS):
            q = qkv[r0:r0 + c, h * DN_DK:(h + 1) * DN_DK]
            k = qkv[r0:r0 + c, 512 + h * DN_DK:512 + (h + 1) * DN_DK]
            v = qkv[r0:r0 + c, 1024 + h * DN_DV:1024 + (h + 1) * DN_DV]
            q = q * lax.rsqrt(jnp.sum(q * q, axis=-1, keepdims=True) + 1e-6) * (DN_DK ** -0.5)
            k = k * lax.rsqrt(jnp.sum(k * k, axis=-1, keepdims=True) + 1e-6)
            lane = d * DN_HEADS + h
            g_col = gates[r0:r0 + c, lane:lane + 1]
            beta = betas[r0:r0 + c, 8 + lane:9 + lane]
            g_wide = jnp.broadcast_to(g_col, (c, DN_DK))
            gc_rows = _dot_hi(lm, g_wide)
            gc_lanes = _dot_hi(ones, g_wide[:, :c] * lmt)
            dlog = gc_rows[:, :c] - gc_lanes
            decay = jnp.exp(jnp.where(incl, dlog, -1e30))
            g_last = gc_rows[edge:edge + 1, :]
            e_gc = jnp.exp(gc_rows)
            kb = k * beta
            vb = v * beta
            amat = jnp.where(strict, _dot_nt_bf(kb, k) * decay, 0.0)
            x = -amat
            tinv = eye + x
            for _ in range(5):
                x = _dot_bf(x, x)
                tinv = tinv + _dot_bf(tinv, x)
            sol = _dot_bf(tinv, jnp.concatenate([vb, kb * e_gc], axis=1))
            u = sol[:, :DN_DV]
            w = sol[:, DN_DV:]
            qk = jnp.where(incl, _dot_nt_bf(q, k) * decay, 0.0)
            k_dec = k * jnp.exp(g_last - gc_rows)
            q_dec = q * e_gc
            s = s_ref[lane]
            sb = s.astype(BF16)
            v_new = u - jnp.dot(w.astype(BF16), sb, preferred_element_type=F32)
            o = jnp.dot(q_dec.astype(BF16), sb, preferred_element_type=F32) + _dot_bf(qk, v_new)
            s_ref[lane] = s * jnp.exp(g_last) + _dot_bf(k_dec.T, v_new)
            o_ref[r0:r0 + c, h * DN_DV:(h + 1) * DN_DV] = o


def _dn_kernel(cur_f, prev_f, next_f, ab_f, cur_b, prev_b, next_b, ab_b, cw_ref, alog_ref, dtb_ref,
               of_ref, ob_ref, s_ref):
    i = pl.program_id(1)
    nb = pl.num_programs(1)

    @pl.when(i == 0)
    def _():
        s_ref[...] = jnp.zeros_like(s_ref)

    cw = cw_ref[...]
    alog = alog_ref[...]
    dtb = dtb_ref[...]
    _dn_direction(0, cur_f[...], prev_f[...], next_f[...], ab_f[...], i == 0, i == nb - 1, cw, alog, dtb, s_ref, of_ref)
    _dn_direction(1, cur_b[...], prev_b[...], next_b[...], ab_b[...], i == nb - 1, i == 0, cw, alog, dtb, s_ref, ob_ref)


def _halo_specs(tb, width, col_block, nb, t, reverse):
    per = tb // SUBLANE
    last8 = t // SUBLANE - 1
    if reverse:
        blk = lambda i: nb - 1 - i
    else:
        blk = lambda i: i
    cur = pl.BlockSpec((None, tb, width), lambda b, i: (b, blk(i), col_block))
    prev = pl.BlockSpec((None, SUBLANE, width), lambda b, i: (b, jnp.maximum(blk(i) * per - 1, 0), col_block))
    nxt = pl.BlockSpec((None, SUBLANE, width), lambda b, i: (b, jnp.minimum((blk(i) + 1) * per, last8), col_block))
    return cur, prev, nxt


def _deltanet(hm3, conv_w, a_log, dt_bias, tb=256):
    bsz, t, _ = hm3.shape
    nb = t // tb
    qkv_w = 3 * 512
    alog_v = jnp.zeros((1, LANE), F32).at[0, :8].set(a_log.reshape(-1))
    dtb_v = jnp.zeros((1, LANE), F32).at[0, :8].set(dt_bias.reshape(-1))
    ab_col = C_AB // LANE
    in_specs = []
    for rev in (False, True):
        in_specs += list(_halo_specs(tb, qkv_w, 0, nb, t, rev))
        if rev:
            in_specs.append(pl.BlockSpec((None, tb, LANE), lambda b, i: (b, nb - 1 - i, ab_col)))
        else:
            in_specs.append(pl.BlockSpec((None, tb, LANE), lambda b, i: (b, i, ab_col)))
    in_specs += [_const_spec((DN_CONV, qkv_w)), _const_spec((1, LANE)), _const_spec((1, LANE))]
    out_specs = [pl.BlockSpec((None, tb, 512), lambda b, i: (b, i, 0)),
                 pl.BlockSpec((None, tb, 512), lambda b, i: (b, nb - 1 - i, 0))]
    return pl.pallas_call(
        _dn_kernel,
        grid=(bsz, nb),
        in_specs=in_specs,
        out_specs=out_specs,
        out_shape=[jax.ShapeDtypeStruct((bsz, t, 512), F32)] * 2,
        scratch_shapes=[pltpu.VMEM((2 * DN_HEADS, DN_DK, DN_DV), F32)],
        compiler_params=_cparams(("parallel", "arbitrary")),
        name="deltanet",
    )(hm3, hm3, hm3, hm3, hm3, hm3, hm3, hm3, conv_w, alog_v, dtb_v)


def _s5_discretise(lam_re, lam_im, log_dt, b_re, b_im):
    dt = jnp.exp(log_dt)[:, None]
    mag = jnp.exp(lam_re * dt)
    ab_re = mag * jnp.cos(lam_im * dt)
    ab_im = mag * jnp.sin(lam_im * dt)
    den = jnp.square(lam_re) + jnp.square(lam_im)
    nr, ni = ab_re - 1.0, ab_im
    kr = ((nr * lam_re + ni * lam_im) / den)[..., None]
    ki = ((ni * lam_re - nr * lam_im) / den)[..., None]
    return kr * b_re - ki * b_im, kr * b_im + ki * b_re


def _s5_operators(lam_re, lam_im, log_dt, b_re, b_im, c_re, c_im):
    hi = lax.Precision.HIGHEST
    L, G, N, P = S5_L, S5_GROUPS, S5_N, S5_P
    j = jnp.arange(L + 1, dtype=F32)[:, None, None]
    kcomb = 0.0
    e_cols, f_rows, al = [], [], []
    for d in range(2):
        dt = jnp.exp(log_dt[d])[:, None]
        bb_re, bb_im = _s5_discretise(lam_re[d], lam_im[d], log_dt[d], b_re, b_im)
        mag = jnp.exp(lam_re[d] * dt * j)
        ang = lam_im[d] * dt * j
        aj_re, aj_im = mag * jnp.cos(ang), mag * jnp.sin(ang)
        ca_re = c_re[None] * aj_re[:, :, None, :] - c_im[None] * aj_im[:, :, None, :]
        ca_im = c_re[None] * aj_im[:, :, None, :] + c_im[None] * aj_re[:, :, None, :]
        kj = (jnp.einsum('jgpn,gnq->jgpq', ca_re[:L], bb_re, precision=hi)
              - jnp.einsum('jgpn,gnq->jgpq', ca_im[:L], bb_im, precision=hi))
        zeros = jnp.zeros((L - 1,) + kj.shape[1:], F32)
        if d == 0:
            kcomb = kcomb + jnp.concatenate([zeros, kj], axis=0)
        else:
            kcomb = kcomb + jnp.concatenate([kj[::-1], zeros], axis=0)
        pw_re = aj_re[:L][::-1] if d == 0 else aj_re[:L]
        pw_im = aj_im[:L][::-1] if d == 0 else aj_im[:L]
        e_re = pw_re[..., None] * bb_re[None] - pw_im[..., None] * bb_im[None]
        e_im = pw_re[..., None] * bb_im[None] + pw_im[..., None] * bb_re[None]
        for e in (e_re, e_im):
            e = e.transpose(1, 0, 3, 2).reshape(G, L * P, N)
            e_cols.append(jnp.pad(e, ((0, 0), (0, 0), (0, LANE - N))))
        sel = slice(1, L + 1)
        fr = ca_re[sel] if d == 0 else ca_re[sel][::-1]
        fi = ca_im[sel] if d == 0 else ca_im[sel][::-1]
        for f in (fr, -fi):
            f = f.transpose(1, 3, 0, 2).reshape(G, N, L * P)
            f_rows.append(jnp.pad(f, ((0, 0), (0, LANE - N), (0, 0))))
        al += [jnp.pad(aj_re[L], ((0, 0), (0, LANE - N))), jnp.pad(aj_im[L], ((0, 0), (0, LANE - N)))]
    s_idx = jnp.arange(L)[:, None]
    t_idx = jnp.arange(L)[None, :]
    toep = kcomb[t_idx - s_idx + L - 1]
    toep = toep.transpose(2, 0, 4, 1, 3).reshape(G, L * P, L * P)
    w1 = jnp.concatenate([toep] + e_cols, axis=2).astype(BF16)
    w2 = jnp.concatenate(f_rows, axis=1).astype(BF16)
    return w1, w2, jnp.stack(al, axis=1)


def _s5_kernel(u_ref, w1_ref, w2_ref, al_ref, y_ref, hloc_ref, hin_ref, *, n_chunks, bsz):
    u = u_ref[...]
    y_ref[...] = jnp.dot(u, w1_ref[:, :S5_LW], preferred_element_type=F32)
    hloc_ref[...] = jnp.dot(u, w1_ref[:, S5_LW:], preferred_element_type=F32)
    al = al_ref[...]
    a_re = (al[0:1], al[2:3])
    a_im = (al[1:2], al[3:4])

    def body(cidx, carry):
        new = []
        for d in range(2):
            cr, ci = carry[2 * d], carry[2 * d + 1]
            cc = cidx if d == 0 else n_chunks - 1 - cidx
            rows = pl.ds(pl.multiple_of(cc * bsz, bsz), bsz)
            hin_ref[rows, 2 * d * LANE:(2 * d + 1) * LANE] = cr
            hin_ref[rows, (2 * d + 1) * LANE:(2 * d + 2) * LANE] = ci
            lr = hloc_ref[rows, 2 * d * LANE:(2 * d + 1) * LANE]
            li = hloc_ref[rows, (2 * d + 1) * LANE:(2 * d + 2) * LANE]
            new += [a_re[d] * cr - a_im[d] * ci + lr, a_re[d] * ci + a_im[d] * cr + li]
        return tuple(new)

    zero = jnp.zeros((bsz, LANE), F32)
    lax.fori_loop(0, n_chunks, body, (zero, zero, zero, zero))
    y_ref[...] += jnp.dot(hin_ref[...].astype(BF16), w2_ref[...], preferred_element_type=F32)


def _s5(hm3, w1, w2, al):
    bsz, t, _ = hm3.shape
    n_chunks = t // S5_L
    rows = n_chunks * bsz
    u = hm3[:, :, C_SU:C_SU + S5_WIDTH].reshape(bsz, n_chunks, S5_L, S5_GROUPS, S5_P)
    u = u.transpose(3, 1, 0, 2, 4).reshape(S5_GROUPS, rows, S5_LW).astype(BF16)
    y = pl.pallas_call(
        functools.partial(_s5_kernel, n_chunks=n_chunks, bsz=bsz),
        grid=(S5_GROUPS,),
        in_specs=[pl.BlockSpec((None, rows, S5_LW), lambda g: (g, 0, 0)),
                  pl.BlockSpec((None, S5_LW, S5_LW + S5_HW), lambda g: (g, 0, 0)),
                  pl.BlockSpec((None, S5_HW, S5_LW), lambda g: (g, 0, 0)),
                  pl.BlockSpec((None, 4, LANE), lambda g: (g, 0, 0))],
        out_specs=pl.BlockSpec((None, rows, S5_LW), lambda g: (g, 0, 0)),
        out_shape=jax.ShapeDtypeStruct((S5_GROUPS, rows, S5_LW), F32),
        scratch_shapes=[pltpu.VMEM((rows, S5_HW), F32), pltpu.VMEM((rows, S5_HW), F32)],
        compiler_params=_cparams(("parallel",)),
        name="s5",
    )(u, w1, w2, al)
    y = y.reshape(S5_GROUPS, n_chunks, bsz, S5_L, S5_P).transpose(2, 1, 3, 0, 4)
    return y.reshape(bsz * t, S5_WIDTH)


def _na_bias_table(rpb):
    delta = jnp.arange(NA_KH)[:, None]
    jrow = jnp.arange(NA_KH)[None, :]
    dr = jrow - delta + (NA_KH - 1)
    c = jnp.arange(GRID_W)
    c0 = jnp.clip(c - NA_KW // 2, 0, GRID_W - NA_KW)
    col_in = (c[None, :] >= c0[:, None]) & (c[None, :] < c0[:, None] + NA_KW)
    dc = jnp.clip(c[None, :] - c[:, None], -(NA_KW - 1), NA_KW - 1) + (NA_KW - 1)
    bias = rpb[:, dr[:, None, :, None], dc[None, :, None, :]]
    bias = jnp.where(col_in[None, None, :, None, :], bias, -1e30)
    return bias.reshape(NA_HEADS, NA_KH, GRID_W, NA_KH * GRID_W).astype(F32)


def _na_kernel(q_ref, k_ref, v_ref, tab_ref, o_ref, *, rows_per_step, n_rows):
    i = pl.program_id(2)
    lane = lax.broadcasted_iota(jnp.int32, (GRID_W, LANE), 1)
    low = lane < NA_DH
    nk = NA_KH * GRID_W

    def body(rr, _):
        r = i * rows_per_step + rr
        r0 = jnp.clip(r - NA_KH // 2, 0, n_rows - NA_KH)
        delta = r - r0
        krows = pl.ds(pl.multiple_of(r0 * GRID_W, GRID_W), nk)
        kr = k_ref[krows, :]
        vr = v_ref[krows, :]
        qrows = pl.ds(pl.multiple_of(rr * GRID_W, GRID_W), GRID_W)
        qr = q_ref[qrows, :]
        outs = []
        for hh in range(2):
            qm = jnp.where(low if hh == 0 else ~low, qr, jnp.zeros_like(qr))
            s = lax.dot_general(qm, kr, (((1,), (1,)), ((), ())), preferred_element_type=F32)
            s = s + tab_ref[hh, delta]
            m = jnp.max(s, axis=-1, keepdims=True)
            p = jnp.exp(s - m)
            l = jnp.sum(p, axis=-1, keepdims=True)
            outs.append(jnp.dot(p.astype(BF16), vr, preferred_element_type=F32) / l)
        o_ref[qrows, :] = jnp.where(low, outs[0], outs[1]).astype(o_ref.dtype)
        return 0

    lax.fori_loop(0, rows_per_step, body, 0)


def _natten(na3, table, rows_per_step=8):
    bsz, t, _ = na3.shape
    n_rows = t // GRID_W
    tq = rows_per_step * GRID_W
    pairs = NA_HEADS // 2
    kcol = NA_WIDTH // LANE
    return pl.pallas_call(
        functools.partial(_na_kernel, rows_per_step=rows_per_step, n_rows=n_rows),
        grid=(pairs, bsz, t // tq),
        in_specs=[pl.BlockSpec((None, tq, LANE), lambda p, b, i: (b, i, p)),
                  pl.BlockSpec((None, t, LANE), lambda p, b, i: (b, 0, kcol + p)),
                  pl.BlockSpec((None, t, LANE), lambda p, b, i: (b, 0, 2 * kcol + p)),
                  pl.BlockSpec((2, NA_KH, GRID_W, NA_KH * GRID_W), lambda p, b, i: (p, 0, 0, 0))],
        out_specs=pl.BlockSpec((None, tq, LANE), lambda p, b, i: (b, i, p)),
        out_shape=jax.ShapeDtypeStruct((bsz, t, NA_WIDTH), BF16),
        compiler_params=_cparams(("parallel", "parallel", "parallel")),
        name="natten",
    )(na3, na3, na3, table)


def _lru_direction(d, cur, prev8, next8, first, last, cw, cb, wg_ref, gb, sp_lam, a_scr, b_scr, carry_ref, o_ref):
    tb = cur.shape[0]
    xc = _conv_centred(prev8, cur, next8, cw, first, last) + cb
    width = 2 * LRU_WIDTH
    gates = jnp.dot(xc.astype(BF16), wg_ref[:, d * width:(d + 1) * width], preferred_element_type=F32)
    gates = jax.nn.sigmoid(gates + gb[:, d * width:(d + 1) * width])
    log_a = -LRU_C * gates[:, :LRU_WIDTH] * sp_lam[d:d + 1]
    a = jnp.exp(log_a)
    a_scr[d] = a
    b_scr[d] = jnp.sqrt(1.0 - a * a) * gates[:, LRU_WIDTH:] * xc

    n_groups = tb // SUBLANE

    def body(gi, h):
        grp = gi if d == 0 else n_groups - 1 - gi
        rows = pl.ds(pl.multiple_of(grp * SUBLANE, SUBLANE), SUBLANE)
        a8 = a_scr[d, rows, :]
        b8 = b_scr[d, rows, :]
        out = [None] * SUBLANE
        order = range(SUBLANE) if d == 0 else range(SUBLANE - 1, -1, -1)
        for r in order:
            h = a8[r:r + 1] * h + b8[r:r + 1]
            out[r] = h
        o_ref[rows, :] = jnp.concatenate(out, axis=0)
        return h

    carry_ref[d:d + 1] = lax.fori_loop(0, n_groups, body, carry_ref[d:d + 1])


def _lru_kernel(cur_f, prev_f, next_f, cur_b, prev_b, next_b, cw_ref, cb_ref, wg_ref, gb_ref, lam_ref,
                hf_ref, hb_ref, a_scr, b_scr, carry_ref):
    i = pl.program_id(1)
    nb = pl.num_programs(1)

    @pl.when(i == 0)
    def _():
        carry_ref[...] = jnp.zeros_like(carry_ref)

    cw = cw_ref[...]
    cb = cb_ref[...]
    gb = gb_ref[...]
    sp_lam = jax.nn.softplus(-lam_ref[...])
    _lru_direction(0, cur_f[...], prev_f[...], next_f[...], i == 0, i == nb - 1, cw, cb, wg_ref, gb, sp_lam,
                   a_scr, b_scr, carry_ref, hf_ref)
    _lru_direction(1, cur_b[...], prev_b[...], next_b[...], i == nb - 1, i == 0, cw, cb, wg_ref, gb, sp_lam,
                   a_scr, b_scr, carry_ref, hb_ref)


def _lru_gate_matrix(gate_w):
    eye = jnp.eye(LRU_BLOCKS, dtype=gate_w.dtype)
    full = jnp.einsum('dgncm,nk->ncdgkm', gate_w, eye)
    return full.reshape(LRU_WIDTH, 4 * LRU_WIDTH)


def _rglru(hm3, conv_w, conv_b, wg, gate_b, lam, tb=256):
    bsz, t, _ = hm3.shape
    nb = t // tb
    col = C_LX // LRU_WIDTH
    in_specs = list(_halo_specs(tb, LRU_WIDTH, col, nb, t, False)) + list(_halo_specs(tb, LRU_WIDTH, col, nb, t, True))
    in_specs += [_const_spec((LRU_CONV, LRU_WIDTH)), _const_spec((1, LRU_WIDTH)),
                 _const_spec((LRU_WIDTH, 4 * LRU_WIDTH)), _const_spec((1, 4 * LRU_WIDTH)), _const_spec((2, LRU_WIDTH))]
    out_specs = [pl.BlockSpec((None, tb, LRU_WIDTH), lambda b, i: (b, i, 0)),
                 pl.BlockSpec((None, tb, LRU_WIDTH), lambda b, i: (b, nb - 1 - i, 0))]
    return pl.pallas_call(
        _lru_kernel,
        grid=(bsz, nb),
        in_specs=in_specs,
        out_specs=out_specs,
        out_shape=[jax.ShapeDtypeStruct((bsz, t, LRU_WIDTH), F32)] * 2,
        scratch_shapes=[pltpu.VMEM((2, tb, LRU_WIDTH), F32), pltpu.VMEM((2, tb, LRU_WIDTH), F32),
                        pltpu.VMEM((2, LRU_WIDTH), F32)],
        compiler_params=_cparams(("parallel", "arbitrary")),
        name="rglru",
    )(hm3, hm3, hm3, hm3, hm3, hm3, conv_w, conv_b.reshape(1, -1), wg, gate_b.reshape(1, -1), lam)


def _merge_kernel(x_ref, of_ref, ob_ref, z_ref, ys_ref, su_ref, na_ref, hf_ref, hb_ref, lg_ref,
                  ng_ref, sd_ref, gw_ref, gbias_ref, wgt_ref, wbr_ref, wout_ref, lng_ref, lnb_ref, o_ref):
    x = x_ref[...]
    xb = x.astype(BF16)
    o = of_ref[...] + ob_ref[...]
    z = z_ref[...]
    parts = []
    for h in range(DN_HEADS):
        oh = o[:, h * DN_DV:(h + 1) * DN_DV]
        ms = jnp.mean(oh * oh, axis=-1, keepdims=True)
        parts.append(oh * lax.rsqrt(ms + EPS) * ng_ref[...])
    y_a = jnp.concatenate(parts, axis=1) * (z * jax.nn.sigmoid(z))
    y = jax.nn.gelu(ys_ref[...] + sd_ref[...] * su_ref[...])
    y_b = y * jax.nn.sigmoid(jnp.dot(y.astype(BF16), gw_ref[...], preferred_element_type=F32) + gbias_ref[...])
    y_d = (hf_ref[...] + hb_ref[...]) * jax.nn.gelu(lg_ref[...])
    ys = (y_a.astype(BF16), y_b.astype(BF16), na_ref[...], y_d.astype(BF16))
    acc = None
    for n in range(N_BRANCH):
        gate = jax.nn.sigmoid(jnp.dot(xb, wgt_ref[:, n * D_MODEL:(n + 1) * D_MODEL], preferred_element_type=F32))
        term = gate * jnp.dot(ys[n], wbr_ref[n], preferred_element_type=F32)
        acc = term if acc is None else acc + term
    mix = jnp.dot(acc.astype(BF16), wout_ref[...], preferred_element_type=F32)
    o_ref[...] = _layer_norm(ALPHA * x + mix, lng_ref[...], lnb_ref[...])


def _merge(x, o_f, o_b, hm, y_s5, na_o, h_f, h_b, norm_g, s5_d, glu_w, glu_b, w_gate, w_branch, w_out, ln_g, ln_b,
           tm=256):
    n = x.shape[0]
    tok = lambda w, cb=0: pl.BlockSpec((tm, w), lambda i: (i, cb))
    in_specs = [tok(D_MODEL), tok(512), tok(512), tok(512, C_Z // 512), tok(512), tok(512, C_SU // 512), tok(512),
                tok(512), tok(512), tok(512, C_LG // 512),
                _const_spec((1, DN_DV)), _const_spec((1, S5_WIDTH)), _const_spec((S5_WIDTH, S5_WIDTH)),
                _const_spec((1, S5_WIDTH)), _const_spec((D_MODEL, N_BRANCH * D_MODEL)),
                _const_spec((N_BRANCH, BRANCH_W, D_MODEL)), _const_spec((D_MODEL, D_MODEL)),
                _const_spec((1, D_MODEL)), _const_spec((1, D_MODEL))]
    return pl.pallas_call(
        _merge_kernel,
        grid=(n // tm,),
        in_specs=in_specs,
        out_specs=pl.BlockSpec((tm, D_MODEL), lambda i: (i, 0)),
        out_shape=jax.ShapeDtypeStruct((n, D_MODEL), F32),
        compiler_params=_cparams(("parallel",)),
        name="merge",
    )(x, o_f, o_b, hm, y_s5, hm, na_o, h_f, h_b, hm, norm_g.reshape(1, -1), s5_d.reshape(1, -1), glu_w,
      glu_b.reshape(1, -1), w_gate, w_branch, w_out, ln_g.reshape(1, -1), ln_b.reshape(1, -1))


def _mlp_kernel(x_ref, w1_ref, b1_ref, w2_ref, b2_ref, g_ref, b_ref, o_ref):
    x = x_ref[...]
    xb = x.astype(BF16)
    acc = None
    for c0 in range(0, D_FF, D_MODEL):
        f = jnp.dot(xb, w1_ref[:, c0:c0 + D_MODEL], preferred_element_type=F32) + b1_ref[:, c0:c0 + D_MODEL]
        f = jnp.square(jnp.maximum(f, 0.0))
        term = jnp.dot(f.astype(BF16), w2_ref[c0:c0 + D_MODEL, :], preferred_element_type=F32)
        acc = term if acc is None else acc + term
    o_ref[...] = _layer_norm(ALPHA * x + acc + b2_ref[...], g_ref[...], b_ref[...])


def _mlp(x, w1, b1, w2, b2, g, b, tm=512):
    n = x.shape[0]
    return pl.pallas_call(
        _mlp_kernel,
        grid=(n // tm,),
        in_specs=[pl.BlockSpec((tm, D_MODEL), lambda i: (i, 0)), _const_spec((D_MODEL, D_FF)), _const_spec((1, D_FF)),
                  _const_spec((D_FF, D_MODEL)), _const_spec((1, D_MODEL)), _const_spec((1, D_MODEL)),
                  _const_spec((1, D_MODEL))],
        out_specs=pl.BlockSpec((tm, D_MODEL), lambda i: (i, 0)),
        out_shape=jax.ShapeDtypeStruct((n, D_MODEL), F32),
        compiler_params=_cparams(("parallel",)),
        name="mlp",
    )(x, w1, b1.reshape(1, -1), w2, b2.reshape(1, -1), g.reshape(1, -1), b.reshape(1, -1))


def _prepare_layer(l, p):
    w_in = p['w_in'][l]
    col = lambda i: w_in[:, _IN_OFFS[i]:_IN_OFFS[i] + _IN_SPLITS[i]]
    pad = jnp.zeros((D_MODEL, W_MAIN - C_AB - 16), F32)
    w_main = jnp.concatenate([col(0), col(1), col(2), col(3), col(6), col(10), col(11), col(4), col(5), pad], axis=1)
    w_na = jnp.concatenate([col(7) * (NA_DH ** -0.5), col(8), col(9)], axis=1)
    s5_w1, s5_w2, s5_al = _s5_operators(p['s5_lambda_re'][l], p['s5_lambda_im'][l], p['s5_log_dt'][l],
                                        p['s5_b_re'][l], p['s5_b_im'][l], p['s5_c_re'][l], p['s5_c_im'][l])
    return dict(
        w_main=w_main.astype(BF16), w_na=w_na.astype(BF16), w_gate=col(12).astype(BF16),
        dn_conv_w=p['dn_conv_w'][l], dn_a_log=p['dn_a_log'][l], dn_dt_bias=p['dn_dt_bias'][l],
        dn_norm_g=p['dn_norm_g'][l],
        s5_w1=s5_w1, s5_w2=s5_w2, s5_al=s5_al, s5_d=p['s5_d'][l], glu_w=p['s5_glu_w'][l].astype(BF16),
        glu_b=p['s5_glu_b'][l],
        na_table=_na_bias_table(p['na_rpb'][l]),
        lru_conv_w=p['lru_conv_w'][l], lru_conv_b=p['lru_conv_b'][l],
        lru_wg=_lru_gate_matrix(p['lru_gate_w'][l]).astype(BF16), lru_gate_b=p['lru_gate_b'][l],
        lru_lambda=p['lru_lambda'][l],
        w_branch=p['w_branch'][l].astype(BF16), w_out=p['w_out'][l].astype(BF16),
        ln1_g=p['ln1_g'][l], ln1_b=p['ln1_b'][l],
        mlp_w1=p['mlp_w1'][l].astype(BF16), mlp_b1=p['mlp_b1'][l], mlp_w2=p['mlp_w2'][l].astype(BF16),
        mlp_b2=p['mlp_b2'][l], ln2_g=p['ln2_g'][l], ln2_b=p['ln2_b'][l])


def _layer(x, bsz, t, lw):
    hm, na = _in_proj(x, lw['w_main'], lw['w_na'])
    hm3 = hm.reshape(bsz, t, W_MAIN)
    o_f, o_b = _deltanet(hm3, lw['dn_conv_w'], lw['dn_a_log'], lw['dn_dt_bias'])
    y_s5 = _s5(hm3, lw['s5_w1'], lw['s5_w2'], lw['s5_al'])
    na_o = _natten(na.reshape(bsz, t, 3 * NA_WIDTH), lw['na_table'])
    h_f, h_b = _rglru(hm3, lw['lru_conv_w'], lw['lru_conv_b'], lw['lru_wg'], lw['lru_gate_b'], lw['lru_lambda'])
    n = bsz * t
    x1 = _merge(x, o_f.reshape(n, 512), o_b.reshape(n, 512), hm, y_s5, na_o.reshape(n, NA_WIDTH),
                h_f.reshape(n, LRU_WIDTH), h_b.reshape(n, LRU_WIDTH), lw['dn_norm_g'], lw['s5_d'], lw['glu_w'],
                lw['glu_b'], lw['w_gate'], lw['w_branch'], lw['w_out'], lw['ln1_g'], lw['ln1_b'])
    return _mlp(x1, lw['mlp_w1'], lw['mlp_b1'], lw['mlp_w2'], lw['mlp_b2'], lw['ln2_g'], lw['ln2_b'])


def _trunk(x, ln_g, ln_b, layers):
    bsz, t, _ = x.shape
    h = _input_ln(x.reshape(bsz * t, D_MODEL), ln_g, ln_b)
    for lw in layers:
        h = _layer(h, bsz, t, lw)
    return h.reshape(bsz, t, D_MODEL)


def kernel(x_prompt, x_sample, ln_in_g, ln_in_b, w_in, dn_conv_w, dn_a_log, dn_dt_bias, dn_norm_g, s5_lambda_re,
           s5_lambda_im, s5_log_dt, s5_b_re, s5_b_im, s5_c_re, s5_c_im, s5_d, s5_glu_w, s5_glu_b, na_rpb, lru_conv_w,
           lru_conv_b, lru_gate_w, lru_gate_b, lru_lambda, w_branch, w_out, ln1_g, ln1_b, mlp_w1, mlp_b1, mlp_w2,
           mlp_b2, ln2_g, ln2_b):
    p = dict(w_in=w_in, dn_conv_w=dn_conv_w, dn_a_log=dn_a_log, dn_dt_bias=dn_dt_bias, dn_norm_g=dn_norm_g,
             s5_lambda_re=s5_lambda_re, s5_lambda_im=s5_lambda_im, s5_log_dt=s5_log_dt, s5_b_re=s5_b_re,
             s5_b_im=s5_b_im, s5_c_re=s5_c_re, s5_c_im=s5_c_im, s5_d=s5_d, s5_glu_w=s5_glu_w, s5_glu_b=s5_glu_b,
             na_rpb=na_rpb, lru_conv_w=lru_conv_w, lru_conv_b=lru_conv_b, lru_gate_w=lru_gate_w,
             lru_gate_b=lru_gate_b, lru_lambda=lru_lambda, w_branch=w_branch, w_out=w_out, ln1_g=ln1_g, ln1_b=ln1_b,
             mlp_w1=mlp_w1, mlp_b1=mlp_b1, mlp_w2=mlp_w2, mlp_b2=mlp_b2, ln2_g=ln2_g, ln2_b=ln2_b)
    layers = [_prepare_layer(l, p) for l in range(DEPTH)]
    return (_trunk(x_prompt, ln_in_g, ln_in_b, layers), _trunk(x_sample, ln_in_g, ln_in_b, layers))
```

```python
import functools
import math

import jax
import jax.numpy as jnp
from jax import lax
from jax.experimental import pallas as pl
from jax.experimental.pallas import tpu as pltpu

F32 = jnp.float32
BF16 = jnp.bfloat16

D_MODEL = 1024
DEPTH = 2
GRID_W = 64
N_BRANCH = 4
BRANCH_W = 512
DN_HEADS = 4
DN_DK = 128
DN_DV = 128
DN_CONV = 4
DN_CHUNK = 64
S5_WIDTH = 512
S5_P = 16
S5_GROUPS = S5_WIDTH // S5_P
S5_N = 64
NA_HEADS = 8
NA_DH = 64
NA_WIDTH = NA_HEADS * NA_DH
NA_KH = 8
NA_KW = 16
LRU_WIDTH = 512
LRU_BLOCKS = 8
LRU_BW = LRU_WIDTH // LRU_BLOCKS
LRU_CONV = 4
LRU_C = 8.0
D_FF = 4 * D_MODEL
ALPHA = float((2 * DEPTH) ** 0.25)
EPS = 1e-5

_IN_SPLITS = (512, 512, 512, 512, 8, 8, 512, 512, 512, 512, 512, 512, 4096)
_IN_OFFS = tuple(sum(_IN_SPLITS[:i]) for i in range(len(_IN_SPLITS)))

LANE = 128
SUBLANE = 8
C_QKV = 0
C_Z = 1536
C_SU = 2048
C_LX = 2560
C_LG = 3072
C_AB = 3584
W_MAIN = 3712

S5_L = 64
S5_LW = S5_L * S5_P
S5_HW = 4 * LANE

VMEM_LIMIT = 56 * 1024 * 1024


def _cparams(sem):
    return pltpu.CompilerParams(dimension_semantics=sem, vmem_limit_bytes=VMEM_LIMIT)


def _layer_norm(x, g, b):
    mu = jnp.mean(x, axis=-1, keepdims=True)
    xc = x - mu
    var = jnp.mean(xc * xc, axis=-1, keepdims=True)
    return xc * lax.rsqrt(var + EPS) * g + b


def _const_spec(shape):
    nd = len(shape)
    return pl.BlockSpec(shape, lambda *_: (0,) * nd, pipeline_mode=pl.Buffered(1))


def _ln_kernel(x_ref, g_ref, b_ref, o_ref):
    o_ref[...] = _layer_norm(x_ref[...], g_ref[...], b_ref[...])


def _input_ln(x, g, b, tm=512):
    n = x.shape[0]
    return pl.pallas_call(
        _ln_kernel,
        grid=(n // tm,),
        in_specs=[pl.BlockSpec((tm, D_MODEL), lambda i: (i, 0)), _const_spec((1, D_MODEL)), _const_spec((1, D_MODEL))],
        out_specs=pl.BlockSpec((tm, D_MODEL), lambda i: (i, 0)),
        out_shape=jax.ShapeDtypeStruct((n, D_MODEL), F32),
        compiler_params=_cparams(("parallel",)),
        name="input_ln",
    )(x, g.reshape(1, -1), b.reshape(1, -1))


def _proj_kernel(x_ref, wm_ref, wn_ref, hm_ref, na_ref):
    xb = x_ref[...].astype(BF16)
    step = 4 * LANE
    for c0 in range(0, W_MAIN, step):
        c1 = min(c0 + step, W_MAIN)
        hm_ref[:, c0:c1] = jnp.dot(xb, wm_ref[:, c0:c1], preferred_element_type=F32)
    for c0 in range(0, 3 * NA_WIDTH, step):
        na_ref[:, c0:c0 + step] = jnp.dot(xb, wn_ref[:, c0:c0 + step], preferred_element_type=F32).astype(BF16)


def _in_proj(x, w_main, w_na, tm=256):
    n = x.shape[0]
    return pl.pallas_call(
        _proj_kernel,
        grid=(n // tm,),
        in_specs=[pl.BlockSpec((tm, D_MODEL), lambda i: (i, 0)),
                  _const_spec((D_MODEL, W_MAIN)), _const_spec((D_MODEL, 3 * NA_WIDTH))],
        out_specs=[pl.BlockSpec((tm, W_MAIN), lambda i: (i, 0)), pl.BlockSpec((tm, 3 * NA_WIDTH), lambda i: (i, 0))],
        out_shape=[jax.ShapeDtypeStruct((n, W_MAIN), F32), jax.ShapeDtypeStruct((n, 3 * NA_WIDTH), BF16)],
        compiler_params=_cparams(("parallel",)),
        name="in_proj",
    )(x, w_main, w_na)


def _conv_centred(prev8, cur, next8, w, first, last):
    tb = cur.shape[0]
    prev8 = jnp.where(first, 0.0, prev8)
    next8 = jnp.where(last, 0.0, next8)
    xp = jnp.concatenate([prev8, cur, next8], axis=0)
    left = (w.shape[0] - 1) // 2
    acc = None
    for j in range(w.shape[0]):
        s = SUBLANE - left + j
        term = xp[s:s + tb] * w[j:j + 1]
        acc = term if acc is None else acc + term
    return acc


def _dot_hi(a, b):
    return jnp.dot(a, b, preferred_element_type=F32, precision=lax.Precision.HIGHEST)


def _dot_bf(a, b):
    return jnp.dot(a.astype(BF16), b.astype(BF16), preferred_element_type=F32)


def _dot_nt_bf(a, b):
    return lax.dot_general(a.astype(BF16), b.astype(BF16), (((1,), (1,)), ((), ())), preferred_element_type=F32)


def _dn_prepare(d, cur, prev8, next8, ab, first, last, cw, alog, dtb):
    tb = cur.shape[0]
    c = DN_CHUNK
    qkv = _conv_centred(prev8, cur, next8, cw, first, last)
    qkv = qkv * jax.nn.sigmoid(qkv)
    gates = -jnp.exp(alog) * jax.nn.softplus(ab + dtb)
    betas = jax.nn.sigmoid(ab)

    row = lax.broadcasted_iota(jnp.int32, (c, c), 0)
    col = lax.broadcasted_iota(jnp.int32, (c, c), 1)
    incl = col <= row if d == 0 else col >= row
    strict = col < row if d == 0 else col > row
    eye = (row == col).astype(F32)
    edge = c - 1 if d == 0 else 0
    n_chunks = tb // c

    rb = lax.broadcasted_iota(jnp.int32, (tb, tb), 0)
    cb = lax.broadcasted_iota(jnp.int32, (tb, tb), 1)
    tri = ((rb // c) == (cb // c)) & (cb <= rb if d == 0 else cb >= rb)
    gc_all = _dot_hi(tri.astype(F32), gates)

    qn, kn = [], []
    for h in range(DN_HEADS):
        q = qkv[:, h * DN_DK:(h + 1) * DN_DK]
        k = qkv[:, 512 + h * DN_DK:512 + (h + 1) * DN_DK]
        qn.append(q * (lax.rsqrt(jnp.sum(q * q, axis=-1, keepdims=True) + 1e-6) * (DN_DK ** -0.5)))
        kn.append(k * lax.rsqrt(jnp.sum(k * k, axis=-1, keepdims=True) + 1e-6))

    units = [(ci, h) for ci in range(n_chunks) for h in range(DN_HEADS)]
    rows = lambda ci: slice(ci * c, (ci + 1) * c)
    lane_of = lambda h: d * DN_HEADS + h
    q_u = [qn[h][rows(ci)] for ci, h in units]
    k_u = [kn[h][rows(ci)] for ci, h in units]
    v_u = [qkv[rows(ci), 1024 + h * DN_DV:1024 + (h + 1) * DN_DV] for ci, h in units]
    beta_u = [jnp.broadcast_to(betas[rows(ci), 8 + lane_of(h):9 + lane_of(h)], (c, DN_DK)) for ci, h in units]
    gcr_u = [jnp.broadcast_to(gc_all[rows(ci), lane_of(h):lane_of(h) + 1], (c, DN_DK)) for ci, h in units]
    gcl_u = [g.T[:c, :] for g in gcr_u]
    decay_u = [jnp.exp(jnp.where(incl, gr[:, :c] - gl, -1e30)) for gr, gl in zip(gcr_u, gcl_u)]
    egc_u = [jnp.exp(g) for g in gcr_u]
    glast_u = [g[edge:edge + 1, :] for g in gcr_u]
    kb_u = [k * b for k, b in zip(k_u, beta_u)]
    vb_u = [v * b for v, b in zip(v_u, beta_u)]
    kk_u = [_dot_nt_bf(jnp.concatenate([kb, q], axis=0), k) for kb, q, k in zip(kb_u, q_u, k_u)]
    x_u = [-jnp.where(strict, kk[:c] * dec, 0.0) for kk, dec in zip(kk_u, decay_u)]
    qk_u = [jnp.where(incl, kk[c:] * dec, 0.0).astype(BF16) for kk, dec in zip(kk_u, decay_u)]
    t_u = [eye + x for x in x_u]
    x_u = [_dot_bf(x, x) for x in x_u]
    for _ in range(4):
        both = [_dot_bf(jnp.concatenate([t, x], axis=0), x) for t, x in zip(t_u, x_u)]
        t_u = [t + b[:c] for t, b in zip(t_u, both)]
        x_u = [b[c:] for b in both]
    t_u = [t + _dot_bf(t, x) for t, x in zip(t_u, x_u)]
    sol_u = [_dot_bf(t, jnp.concatenate([vb, kb * e], axis=1)) for t, vb, kb, e in zip(t_u, vb_u, kb_u, egc_u)]
    u_u = [s[:, :DN_DV] for s in sol_u]
    wq_u = [jnp.concatenate([s[:, DN_DV:], q * e], axis=0).astype(BF16) for s, q, e in zip(sol_u, q_u, egc_u)]
    qkkd_u = [jnp.concatenate([qk, (k * jnp.exp(gl - g)).T.astype(BF16)], axis=0)
              for qk, k, gl, g in zip(qk_u, k_u, glast_u, gcr_u)]
    egl_u = [jnp.exp(gl) for gl in glast_u]
    return dict(zip(units, zip(u_u, wq_u, qkkd_u, egl_u)))


def _dn_kernel(cur_f, prev_f, next_f, ab_f, cur_b, prev_b, next_b, ab_b, cw_ref, alog_ref, dtb_ref,
               of_ref, ob_ref, s_ref):
    i = pl.program_id(1)
    nb = pl.num_programs(1)

    @pl.when(i == 0)
    def _():
        s_ref[...] = jnp.zeros_like(s_ref)

    cw = cw_ref[...]
    alog = alog_ref[...]
    dtb = dtb_ref[...]
    prep = (_dn_prepare(0, cur_f[...], prev_f[...], next_f[...], ab_f[...], i == 0, i == nb - 1, cw, alog, dtb),
            _dn_prepare(1, cur_b[...], prev_b[...], next_b[...], ab_b[...], i == nb - 1, i == 0, cw, alog, dtb))
    c = DN_CHUNK
    n_chunks = cur_f.shape[0] // c
    o_refs = (of_ref, ob_ref)
    chains = [(d, h) for d in range(2) for h in range(DN_HEADS)]
    chunk_at = lambda d, step: step if d == 0 else n_chunks - 1 - step
    state = [s_ref[d * DN_HEADS + h] for d, h in chains]
    for step in range(n_chunks):
        ops = [prep[d][(chunk_at(d, step), h)] for d, h in chains]
        ws = [jnp.dot(wq, s.astype(BF16), preferred_element_type=F32) for (_, wq, _, _), s in zip(ops, state)]
        v_new = [(u - w[:c]).astype(BF16) for (u, _, _, _), w in zip(ops, ws)]
        upd = [jnp.dot(qkkd, vn, preferred_element_type=F32) for (_, _, qkkd, _), vn in zip(ops, v_new)]
        state = [s * egl + up[c:] for s, (_, _, _, egl), up in zip(state, ops, upd)]
        for (d, h), w, up in zip(chains, ws, upd):
            r0 = chunk_at(d, step) * c
            o_refs[d][r0:r0 + c, h * DN_DV:(h + 1) * DN_DV] = w[c:] + up[:c]
    for (d, h), s in zip(chains, state):
        s_ref[d * DN_HEADS + h] = s


def _halo_specs(tb, width, col_block, nb, t, reverse):
    per = tb // SUBLANE
    last8 = t // SUBLANE - 1
    if reverse:
        blk = lambda i: nb - 1 - i
    else:
        blk = lambda i: i
    cur = pl.BlockSpec((None, tb, width), lambda b, i: (b, blk(i), col_block))
    prev = pl.BlockSpec((None, SUBLANE, width), lambda b, i: (b, jnp.maximum(blk(i) * per - 1, 0), col_block))
    nxt = pl.BlockSpec((None, SUBLANE, width), lambda b, i: (b, jnp.minimum((blk(i) + 1) * per, last8), col_block))
    return cur, prev, nxt


def _deltanet(hm3, conv_w, a_log, dt_bias, tb=256):
    bsz, t, _ = hm3.shape
    nb = t // tb
    qkv_w = 3 * 512
    alog_v = jnp.zeros((1, LANE), F32).at[0, :8].set(a_log.reshape(-1))
    dtb_v = jnp.zeros((1, LANE), F32).at[0, :8].set(dt_bias.reshape(-1))
    ab_col = C_AB // LANE
    in_specs = []
    for rev in (False, True):
        in_specs += list(_halo_specs(tb, qkv_w, 0, nb, t, rev))
        if rev:
            in_specs.append(pl.BlockSpec((None, tb, LANE), lambda b, i: (b, nb - 1 - i, ab_col)))
        else:
            in_specs.append(pl.BlockSpec((None, tb, LANE), lambda b, i: (b, i, ab_col)))
    in_specs += [_const_spec((DN_CONV, qkv_w)), _const_spec((1, LANE)), _const_spec((1, LANE))]
    out_specs = [pl.BlockSpec((None, tb, 512), lambda b, i: (b, i, 0)),
                 pl.BlockSpec((None, tb, 512), lambda b, i: (b, nb - 1 - i, 0))]
    return pl.pallas_call(
        _dn_kernel,
        grid=(bsz, nb),
        in_specs=in_specs,
        out_specs=out_specs,
        out_shape=[jax.ShapeDtypeStruct((bsz, t, 512), F32)] * 2,
        scratch_shapes=[pltpu.VMEM((2 * DN_HEADS, DN_DK, DN_DV), F32)],
        compiler_params=_cparams(("parallel", "arbitrary")),
        name="deltanet",
    )(hm3, hm3, hm3, hm3, hm3, hm3, hm3, hm3, conv_w, alog_v, dtb_v)


def _s5_discretise(lam_re, lam_im, log_dt, b_re, b_im):
    dt = jnp.exp(log_dt)[:, None]
    mag = jnp.exp(lam_re * dt)
    ab_re = mag * jnp.cos(lam_im * dt)
    ab_im = mag * jnp.sin(lam_im * dt)
    den = jnp.square(lam_re) + jnp.square(lam_im)
    nr, ni = ab_re - 1.0, ab_im
    kr = ((nr * lam_re + ni * lam_im) / den)[..., None]
    ki = ((ni * lam_re - nr * lam_im) / den)[..., None]
    return kr * b_re - ki * b_im, kr * b_im + ki * b_re


def _s5_operators(lam_re, lam_im, log_dt, b_re, b_im, c_re, c_im):
    hi = lax.Precision.HIGHEST
    L, G, N, P = S5_L, S5_GROUPS, S5_N, S5_P
    j = jnp.arange(L + 1, dtype=F32)[:, None, None]
    kcomb = 0.0
    e_cols, f_rows, al = [], [], []
    for d in range(2):
        dt = jnp.exp(log_dt[d])[:, None]
        bb_re, bb_im = _s5_discretise(lam_re[d], lam_im[d], log_dt[d], b_re, b_im)
        mag = jnp.exp(lam_re[d] * dt * j)
        ang = lam_im[d] * dt * j
        aj_re, aj_im = mag * jnp.cos(ang), mag * jnp.sin(ang)
        ca_re = c_re[None] * aj_re[:, :, None, :] - c_im[None] * aj_im[:, :, None, :]
        ca_im = c_re[None] * aj_im[:, :, None, :] + c_im[None] * aj_re[:, :, None, :]
        kj = (jnp.einsum('jgpn,gnq->jgpq', ca_re[:L], bb_re, precision=hi)
              - jnp.einsum('jgpn,gnq->jgpq', ca_im[:L], bb_im, precision=hi))
        zeros = jnp.zeros((L - 1,) + kj.shape[1:], F32)
        if d == 0:
            kcomb = kcomb + jnp.concatenate([zeros, kj], axis=0)
        else:
            kcomb = kcomb + jnp.concatenate([kj[::-1], zeros], axis=0)
        pw_re = aj_re[:L][::-1] if d == 0 else aj_re[:L]
        pw_im = aj_im[:L][::-1] if d == 0 else aj_im[:L]
        e_re = pw_re[..., None] * bb_re[None] - pw_im[..., None] * bb_im[None]
        e_im = pw_re[..., None] * bb_im[None] + pw_im[..., None] * bb_re[None]
        for e in (e_re, e_im):
            e = e.transpose(1, 0, 3, 2).reshape(G, L * P, N)
            e_cols.append(jnp.pad(e, ((0, 0), (0, 0), (0, LANE - N))))
        sel = slice(1, L + 1)
        fr = ca_re[sel] if d == 0 else ca_re[sel][::-1]
        fi = ca_im[sel] if d == 0 else ca_im[sel][::-1]
        for f in (fr, -fi):
            f = f.transpose(1, 3, 0, 2).reshape(G, N, L * P)
            f_rows.append(jnp.pad(f, ((0, 0), (0, LANE - N), (0, 0))))
        al += [jnp.pad(aj_re[L], ((0, 0), (0, LANE - N))), jnp.pad(aj_im[L], ((0, 0), (0, LANE - N)))]
    kc = jnp.pad(kcomb.astype(BF16).transpose(1, 3, 0, 2), ((0, 0), (0, 0), (0, 1), (0, 0)))
    skew = jnp.broadcast_to(kc[:, :, None], (G, P, L, 2 * L, P)).reshape(G, P, L * 2 * L * P)
    skew = skew[:, :, (L - 1) * P:(L - 1) * P + L * (2 * L - 1) * P].reshape(G, P, L, 2 * L - 1, P)
    toep = skew[:, :, :, :L].transpose(0, 2, 1, 3, 4).reshape(G, L * P, L * P)
    w1 = jnp.concatenate([toep] + [e.astype(BF16) for e in e_cols], axis=2)
    w2 = jnp.concatenate(f_rows, axis=1).astype(BF16)
    return w1, w2, jnp.stack(al, axis=1)


def _s5_kernel(u_ref, w1_ref, w2_ref, al_ref, y_ref, hloc_ref, hin_ref, *, n_chunks, bsz):
    u = u_ref[...]
    y_ref[...] = jnp.dot(u, w1_ref[:, :S5_LW], preferred_element_type=F32)
    hloc_ref[...] = jnp.dot(u, w1_ref[:, S5_LW:], preferred_element_type=F32)
    al = al_ref[...]
    a_re = (al[0:1], al[2:3])
    a_im = (al[1:2], al[3:4])

    def body(cidx, carry):
        new = []
        for d in range(2):
            cr, ci = carry[2 * d], carry[2 * d + 1]
            cc = cidx if d == 0 else n_chunks - 1 - cidx
            rows = pl.ds(pl.multiple_of(cc * bsz, bsz), bsz)
            hin_ref[rows, 2 * d * LANE:(2 * d + 1) * LANE] = cr
            hin_ref[rows, (2 * d + 1) * LANE:(2 * d + 2) * LANE] = ci
            lr = hloc_ref[rows, 2 * d * LANE:(2 * d + 1) * LANE]
            li = hloc_ref[rows, (2 * d + 1) * LANE:(2 * d + 2) * LANE]
            new += [a_re[d] * cr - a_im[d] * ci + lr, a_re[d] * ci + a_im[d] * cr + li]
        return tuple(new)

    zero = jnp.zeros((bsz, LANE), F32)
    lax.fori_loop(0, n_chunks, body, (zero, zero, zero, zero))
    y_ref[...] += jnp.dot(hin_ref[...].astype(BF16), w2_ref[...], preferred_element_type=F32)


def _s5(hm3, w1, w2, al):
    bsz, t, _ = hm3.shape
    n_chunks = t // S5_L
    rows = n_chunks * bsz
    u = hm3[:, :, C_SU:C_SU + S5_WIDTH].reshape(bsz, n_chunks, S5_L, S5_GROUPS, S5_P)
    u = u.transpose(3, 1, 0, 2, 4).reshape(S5_GROUPS, rows, S5_LW).astype(BF16)
    y = pl.pallas_call(
        functools.partial(_s5_kernel, n_chunks=n_chunks, bsz=bsz),
        grid=(S5_GROUPS,),
        in_specs=[pl.BlockSpec((None, rows, S5_LW), lambda g: (g, 0, 0)),
                  pl.BlockSpec((None, S5_LW, S5_LW + S5_HW), lambda g: (g, 0, 0)),
                  pl.BlockSpec((None, S5_HW, S5_LW), lambda g: (g, 0, 0)),
                  pl.BlockSpec((None, 4, LANE), lambda g: (g, 0, 0))],
        out_specs=pl.BlockSpec((None, rows, S5_LW), lambda g: (g, 0, 0)),
        out_shape=jax.ShapeDtypeStruct((S5_GROUPS, rows, S5_LW), F32),
        scratch_shapes=[pltpu.VMEM((rows, S5_HW), F32), pltpu.VMEM((rows, S5_HW), F32)],
        compiler_params=_cparams(("parallel",)),
        name="s5",
    )(u, w1, w2, al)
    y = y.reshape(S5_GROUPS, n_chunks, bsz, S5_L, S5_P).transpose(2, 1, 3, 0, 4)
    return y.reshape(bsz * t, S5_WIDTH)


def _na_bias_table(rpb):
    delta = jnp.arange(NA_KH)[:, None]
    jrow = jnp.arange(NA_KH)[None, :]
    dr = jrow - delta + (NA_KH - 1)
    c = jnp.arange(GRID_W)
    c0 = jnp.clip(c - NA_KW // 2, 0, GRID_W - NA_KW)
    col_in = (c[None, :] >= c0[:, None]) & (c[None, :] < c0[:, None] + NA_KW)
    dc = jnp.clip(c[None, :] - c[:, None], -(NA_KW - 1), NA_KW - 1) + (NA_KW - 1)
    bias = rpb[:, dr[:, None, :, None], dc[None, :, None, :]]
    bias = jnp.where(col_in[None, None, :, None, :], bias, -1e30)
    return bias.reshape(NA_HEADS, NA_KH, GRID_W, NA_KH * GRID_W).astype(F32)


def _na_kernel(q_ref, k_ref, v_ref, tab_ref, o_ref, *, rows_per_step, n_rows):
    i = pl.program_id(2)
    lane = lax.broadcasted_iota(jnp.int32, (GRID_W, LANE), 1)
    low = lane < NA_DH
    nk = NA_KH * GRID_W

    def body(rr, _):
        r = i * rows_per_step + rr
        r0 = jnp.clip(r - NA_KH // 2, 0, n_rows - NA_KH)
        delta = r - r0
        krows = pl.ds(pl.multiple_of(r0 * GRID_W, GRID_W), nk)
        kr = k_ref[krows, :]
        vr = v_ref[krows, :]
        qrows = pl.ds(pl.multiple_of(rr * GRID_W, GRID_W), GRID_W)
        qr = q_ref[qrows, :]
        outs = []
        for hh in range(2):
            qm = jnp.where(low if hh == 0 else ~low, qr, jnp.zeros_like(qr))
            s = lax.dot_general(qm, kr, (((1,), (1,)), ((), ())), preferred_element_type=F32)
            s = s + tab_ref[hh, delta]
            m = jnp.max(s, axis=-1, keepdims=True)
            p = jnp.exp(s - m)
            l = jnp.sum(p, axis=-1, keepdims=True)
            outs.append(jnp.dot(p.astype(BF16), vr, preferred_element_type=F32) / l)
        o_ref[qrows, :] = jnp.where(low, outs[0], outs[1]).astype(o_ref.dtype)
        return 0

    lax.fori_loop(0, rows_per_step, body, 0)


def _natten(na3, table, rows_per_step=8):
    bsz, t, _ = na3.shape
    n_rows = t // GRID_W
    tq = rows_per_step * GRID_W
    pairs = NA_HEADS // 2
    kcol = NA_WIDTH // LANE
    return pl.pallas_call(
        functools.partial(_na_kernel, rows_per_step=rows_per_step, n_rows=n_rows),
        grid=(pairs, bsz, t // tq),
        in_specs=[pl.BlockSpec((None, tq, LANE), lambda p, b, i: (b, i, p)),
                  pl.BlockSpec((None, t, LANE), lambda p, b, i: (b, 0, kcol + p)),
                  pl.BlockSpec((None, t, LANE), lambda p, b, i: (b, 0, 2 * kcol + p)),
                  pl.BlockSpec((2, NA_KH, GRID_W, NA_KH * GRID_W), lambda p, b, i: (p, 0, 0, 0))],
        out_specs=pl.BlockSpec((None, tq, LANE), lambda p, b, i: (b, i, p)),
        out_shape=jax.ShapeDtypeStruct((bsz, t, NA_WIDTH), BF16),
        compiler_params=_cparams(("parallel", "parallel", "parallel")),
        name="natten",
    )(na3, na3, na3, table)


def _lru_direction(d, cur, prev8, next8, first, last, cw, cb, wg_ref, gb, sp_lam, a_scr, b_scr, carry_ref, o_ref):
    tb = cur.shape[0]
    xc = _conv_centred(prev8, cur, next8, cw, first, last) + cb
    width = 2 * LRU_WIDTH
    gates = jnp.dot(xc.astype(BF16), wg_ref[:, d * width:(d + 1) * width], preferred_element_type=F32)
    gates = jax.nn.sigmoid(gates + gb[:, d * width:(d + 1) * width])
    log_a = -LRU_C * gates[:, :LRU_WIDTH] * sp_lam[d:d + 1]
    a = jnp.exp(log_a)
    a_scr[d] = a
    b_scr[d] = jnp.sqrt(1.0 - a * a) * gates[:, LRU_WIDTH:] * xc

    n_groups = tb // SUBLANE

    def body(gi, h):
        grp = gi if d == 0 else n_groups - 1 - gi
        rows = pl.ds(pl.multiple_of(grp * SUBLANE, SUBLANE), SUBLANE)
        a8 = a_scr[d, rows, :]
        b8 = b_scr[d, rows, :]
        out = [None] * SUBLANE
        order = range(SUBLANE) if d == 0 else range(SUBLANE - 1, -1, -1)
        for r in order:
            h = a8[r:r + 1] * h + b8[r:r + 1]
            out[r] = h
        o_ref[rows, :] = jnp.concatenate(out, axis=0)
        return h

    carry_ref[d:d + 1] = lax.fori_loop(0, n_groups, body, carry_ref[d:d + 1])


def _lru_kernel(cur_f, prev_f, next_f, cur_b, prev_b, next_b, cw_ref, cb_ref, wg_ref, gb_ref, lam_ref,
                hf_ref, hb_ref, a_scr, b_scr, carry_ref):
    i = pl.program_id(1)
    nb = pl.num_programs(1)

    @pl.when(i == 0)
    def _():
        carry_ref[...] = jnp.zeros_like(carry_ref)

    cw = cw_ref[...]
    cb = cb_ref[...]
    gb = gb_ref[...]
    sp_lam = jax.nn.softplus(-lam_ref[...])
    _lru_direction(0, cur_f[...], prev_f[...], next_f[...], i == 0, i == nb - 1, cw, cb, wg_ref, gb, sp_lam,
                   a_scr, b_scr, carry_ref, hf_ref)
    _lru_direction(1, cur_b[...], prev_b[...], next_b[...], i == nb - 1, i == 0, cw, cb, wg_ref, gb, sp_lam,
                   a_scr, b_scr, carry_ref, hb_ref)


def _lru_gate_matrix(gate_w):
    eye = jnp.eye(LRU_BLOCKS, dtype=gate_w.dtype)
    full = jnp.einsum('dgncm,nk->ncdgkm', gate_w, eye)
    return full.reshape(LRU_WIDTH, 4 * LRU_WIDTH)


def _rglru(hm3, conv_w, conv_b, wg, gate_b, lam, tb=256):
    bsz, t, _ = hm3.shape
    nb = t // tb
    col = C_LX // LRU_WIDTH
    in_specs = list(_halo_specs(tb, LRU_WIDTH, col, nb, t, False)) + list(_halo_specs(tb, LRU_WIDTH, col, nb, t, True))
    in_specs += [_const_spec((LRU_CONV, LRU_WIDTH)), _const_spec((1, LRU_WIDTH)),
                 _const_spec((LRU_WIDTH, 4 * LRU_WIDTH)), _const_spec((1, 4 * LRU_WIDTH)), _const_spec((2, LRU_WIDTH))]
    out_specs = [pl.BlockSpec((None, tb, LRU_WIDTH), lambda b, i: (b, i, 0)),
                 pl.BlockSpec((None, tb, LRU_WIDTH), lambda b, i: (b, nb - 1 - i, 0))]
    return pl.pallas_call(
        _lru_kernel,
        grid=(bsz, nb),
        in_specs=in_specs,
        out_specs=out_specs,
        out_shape=[jax.ShapeDtypeStruct((bsz, t, LRU_WIDTH), F32)] * 2,
        scratch_shapes=[pltpu.VMEM((2, tb, LRU_WIDTH), F32), pltpu.VMEM((2, tb, LRU_WIDTH), F32),
                        pltpu.VMEM((2, LRU_WIDTH), F32)],
        compiler_params=_cparams(("parallel", "arbitrary")),
        name="rglru",
    )(hm3, hm3, hm3, hm3, hm3, hm3, conv_w, conv_b.reshape(1, -1), wg, gate_b.reshape(1, -1), lam)


def _merge_kernel(x_ref, of_ref, ob_ref, z_ref, ys_ref, su_ref, na_ref, hf_ref, hb_ref, lg_ref,
                  ng_ref, sd_ref, gw_ref, gbias_ref, wgt_ref, wbr_ref, wout_ref, lng_ref, lnb_ref, o_ref):
    x = x_ref[...]
    xb = x.astype(BF16)
    o = of_ref[...] + ob_ref[...]
    z = z_ref[...]
    parts = []
    for h in range(DN_HEADS):
        oh = o[:, h * DN_DV:(h + 1) * DN_DV]
        ms = jnp.mean(oh * oh, axis=-1, keepdims=True)
        parts.append(oh * lax.rsqrt(ms + EPS) * ng_ref[...])
    y_a = jnp.concatenate(parts, axis=1) * (z * jax.nn.sigmoid(z))
    y = jax.nn.gelu(ys_ref[...] + sd_ref[...] * su_ref[...])
    y_b = y * jax.nn.sigmoid(jnp.dot(y.astype(BF16), gw_ref[...], preferred_element_type=F32) + gbias_ref[...])
    y_d = (hf_ref[...] + hb_ref[...]) * jax.nn.gelu(lg_ref[...])
    ys = (y_a.astype(BF16), y_b.astype(BF16), na_ref[...], y_d.astype(BF16))
    acc = None
    for n in range(N_BRANCH):
        gate = jax.nn.sigmoid(jnp.dot(xb, wgt_ref[:, n * D_MODEL:(n + 1) * D_MODEL], preferred_element_type=F32))
        term = gate * jnp.dot(ys[n], wbr_ref[n], preferred_element_type=F32)
        acc = term if acc is None else acc + term
    mix = jnp.dot(acc.astype(BF16), wout_ref[...], preferred_element_type=F32)
    o_ref[...] = _layer_norm(ALPHA * x + mix, lng_ref[...], lnb_ref[...])


def _merge(x, o_f, o_b, hm, y_s5, na_o, h_f, h_b, norm_g, s5_d, glu_w, glu_b, w_gate, w_branch, w_out, ln_g, ln_b,
           tm=256):
    n = x.shape[0]
    tok = lambda w, cb=0: pl.BlockSpec((tm, w), lambda i: (i, cb))
    in_specs = [tok(D_MODEL), tok(512), tok(512), tok(512, C_Z // 512), tok(512), tok(512, C_SU // 512), tok(512),
                tok(512), tok(512), tok(512, C_LG // 512),
                _const_spec((1, DN_DV)), _const_spec((1, S5_WIDTH)), _const_spec((S5_WIDTH, S5_WIDTH)),
                _const_spec((1, S5_WIDTH)), _const_spec((D_MODEL, N_BRANCH * D_MODEL)),
                _const_spec((N_BRANCH, BRANCH_W, D_MODEL)), _const_spec((D_MODEL, D_MODEL)),
                _const_spec((1, D_MODEL)), _const_spec((1, D_MODEL))]
    return pl.pallas_call(
        _merge_kernel,
        grid=(n // tm,),
        in_specs=in_specs,
        out_specs=pl.BlockSpec((tm, D_MODEL), lambda i: (i, 0)),
        out_shape=jax.ShapeDtypeStruct((n, D_MODEL), F32),
        compiler_params=_cparams(("parallel",)),
        name="merge",
    )(x, o_f, o_b, hm, y_s5, hm, na_o, h_f, h_b, hm, norm_g.reshape(1, -1), s5_d.reshape(1, -1), glu_w,
      glu_b.reshape(1, -1), w_gate, w_branch, w_out, ln_g.reshape(1, -1), ln_b.reshape(1, -1))


def _mlp_kernel(x_ref, w1_ref, b1_ref, w2_ref, b2_ref, g_ref, b_ref, o_ref):
    x = x_ref[...]
    xb = x.astype(BF16)
    acc = None
    for c0 in range(0, D_FF, D_MODEL):
        f = jnp.dot(xb, w1_ref[:, c0:c0 + D_MODEL], preferred_element_type=F32) + b1_ref[:, c0:c0 + D_MODEL]
        f = jnp.square(jnp.maximum(f, 0.0))
        term = jnp.dot(f.astype(BF16), w2_ref[c0:c0 + D_MODEL, :], preferred_element_type=F32)
        acc = term if acc is None else acc + term
    o_ref[...] = _layer_norm(ALPHA * x + acc + b2_ref[...], g_ref[...], b_ref[...])


def _mlp(x, w1, b1, w2, b2, g, b, tm=512):
    n = x.shape[0]
    return pl.pallas_call(
        _mlp_kernel,
        grid=(n // tm,),
        in_specs=[pl.BlockSpec((tm, D_MODEL), lambda i: (i, 0)), _const_spec((D_MODEL, D_FF)), _const_spec((1, D_FF)),
                  _const_spec((D_FF, D_MODEL)), _const_spec((1, D_MODEL)), _const_spec((1, D_MODEL)),
                  _const_spec((1, D_MODEL))],
        out_specs=pl.BlockSpec((tm, D_MODEL), lambda i: (i, 0)),
        out_shape=jax.ShapeDtypeStruct((n, D_MODEL), F32),
        compiler_params=_cparams(("parallel",)),
        name="mlp",
    )(x, w1, b1.reshape(1, -1), w2, b2.reshape(1, -1), g.reshape(1, -1), b.reshape(1, -1))


def _prepare_layer(l, p):
    w_in = p['w_in'][l]
    col = lambda i: w_in[:, _IN_OFFS[i]:_IN_OFFS[i] + _IN_SPLITS[i]]
    pad = jnp.zeros((D_MODEL, W_MAIN - C_AB - 16), F32)
    w_main = jnp.concatenate([col(0), col(1), col(2), col(3), col(6), col(10), col(11), col(4), col(5), pad], axis=1)
    w_na = jnp.concatenate([col(7) * (NA_DH ** -0.5), col(8), col(9)], axis=1)
    s5_w1, s5_w2, s5_al = _s5_operators(p['s5_lambda_re'][l], p['s5_lambda_im'][l], p['s5_log_dt'][l],
                                        p['s5_b_re'][l], p['s5_b_im'][l], p['s5_c_re'][l], p['s5_c_im'][l])
    return dict(
        w_main=w_main.astype(BF16), w_na=w_na.astype(BF16), w_gate=col(12).astype(BF16),
        dn_conv_w=p['dn_conv_w'][l], dn_a_log=p['dn_a_log'][l], dn_dt_bias=p['dn_dt_bias'][l],
        dn_norm_g=p['dn_norm_g'][l],
        s5_w1=s5_w1, s5_w2=s5_w2, s5_al=s5_al, s5_d=p['s5_d'][l], glu_w=p['s5_glu_w'][l].astype(BF16),
        glu_b=p['s5_glu_b'][l],
        na_table=_na_bias_table(p['na_rpb'][l]),
        lru_conv_w=p['lru_conv_w'][l], lru_conv_b=p['lru_conv_b'][l],
        lru_wg=_lru_gate_matrix(p['lru_gate_w'][l]).astype(BF16), lru_gate_b=p['lru_gate_b'][l],
        lru_lambda=p['lru_lambda'][l],
        w_branch=p['w_branch'][l].astype(BF16), w_out=p['w_out'][l].astype(BF16),
        ln1_g=p['ln1_g'][l], ln1_b=p['ln1_b'][l],
        mlp_w1=p['mlp_w1'][l].astype(BF16), mlp_b1=p['mlp_b1'][l], mlp_w2=p['mlp_w2'][l].astype(BF16),
        mlp_b2=p['mlp_b2'][l], ln2_g=p['ln2_g'][l], ln2_b=p['ln2_b'][l])


def _layer(x, bsz, t, lw):
    hm, na = _in_proj(x, lw['w_main'], lw['w_na'])
    hm3 = hm.reshape(bsz, t, W_MAIN)
    o_f, o_b = _deltanet(hm3, lw['dn_conv_w'], lw['dn_a_log'], lw['dn_dt_bias'])
    y_s5 = _s5(hm3, lw['s5_w1'], lw['s5_w2'], lw['s5_al'])
    na_o = _natten(na.reshape(bsz, t, 3 * NA_WIDTH), lw['na_table'])
    h_f, h_b = _rglru(hm3, lw['lru_conv_w'], lw['lru_conv_b'], lw['lru_wg'], lw['lru_gate_b'], lw['lru_lambda'])
    n = bsz * t
    x1 = _merge(x, o_f.reshape(n, 512), o_b.reshape(n, 512), hm, y_s5, na_o.reshape(n, NA_WIDTH),
                h_f.reshape(n, LRU_WIDTH), h_b.reshape(n, LRU_WIDTH), lw['dn_norm_g'], lw['s5_d'], lw['glu_w'],
                lw['glu_b'], lw['w_gate'], lw['w_branch'], lw['w_out'], lw['ln1_g'], lw['ln1_b'])
    return _mlp(x1, lw['mlp_w1'], lw['mlp_b1'], lw['mlp_w2'], lw['mlp_b2'], lw['ln2_g'], lw['ln2_b'])


def _trunk(x, ln_g, ln_b, layers):
    bsz, t, _ = x.shape
    h = _input_ln(x.reshape(bsz * t, D_MODEL), ln_g, ln_b)
    for lw in layers:
        h = _layer(h, bsz, t, lw)
    return h.reshape(bsz, t, D_MODEL)


def kernel(x_prompt, x_sample, ln_in_g, ln_in_b, w_in, dn_conv_w, dn_a_log, dn_dt_bias, dn_norm_g, s5_lambda_re,
           s5_lambda_im, s5_log_dt, s5_b_re, s5_b_im, s5_c_re, s5_c_im, s5_d, s5_glu_w, s5_glu_b, na_rpb, lru_conv_w,
           lru_conv_b, lru_gate_w, lru_gate_b, lru_lambda, w_branch, w_out, ln1_g, ln1_b, mlp_w1, mlp_b1, mlp_w2,
           mlp_b2, ln2_g, ln2_b):
    p = dict(w_in=w_in, dn_conv_w=dn_conv_w, dn_a_log=dn_a_log, dn_dt_bias=dn_dt_bias, dn_norm_g=dn_norm_g,
             s5_lambda_re=s5_lambda_re, s5_lambda_im=s5_lambda_im, s5_log_dt=s5_log_dt, s5_b_re=s5_b_re,
             s5_b_im=s5_b_im, s5_c_re=s5_c_re, s5_c_im=s5_c_im, s5_d=s5_d, s5_glu_w=s5_glu_w, s5_glu_b=s5_glu_b,
             na_rpb=na_rpb, lru_conv_w=lru_conv_w, lru_conv_b=lru_conv_b, lru_gate_w=lru_gate_w,
             lru_gate_b=lru_gate_b, lru_lambda=lru_lambda, w_branch=w_branch, w_out=w_out, ln1_g=ln1_g, ln1_b=ln1_b,
             mlp_w1=mlp_w1, mlp_b1=mlp_b1, mlp_w2=mlp_w2, mlp_b2=mlp_b2, ln2_g=ln2_g, ln2_b=ln2_b)
    layers = [_prepare_layer(l, p) for l in range(DEPTH)]
    return (_trunk(x_prompt, ln_in_g, ln_in_b, layers), _trunk(x_sample, ln_in_g, ln_in_b, layers))
```

```python
import functools
import math

import jax
import jax.numpy as jnp
import numpy as np
from jax import lax
from jax.experimental import pallas as pl
from jax.experimental.pallas import tpu as pltpu

F32 = jnp.float32
BF16 = jnp.bfloat16

D_MODEL = 1024
DEPTH = 2
GRID_W = 64
N_BRANCH = 4
BRANCH_W = 512
DN_HEADS = 4
DN_DK = 128
DN_DV = 128
DN_CONV = 4
DN_CHUNK = 64
S5_WIDTH = 512
S5_P = 16
S5_GROUPS = S5_WIDTH // S5_P
S5_N = 64
NA_HEADS = 8
NA_DH = 64
NA_WIDTH = NA_HEADS * NA_DH
NA_KH = 8
NA_KW = 16
LRU_WIDTH = 512
LRU_BLOCKS = 8
LRU_BW = LRU_WIDTH // LRU_BLOCKS
LRU_CONV = 4
LRU_C = 8.0
D_FF = 4 * D_MODEL
ALPHA = float((2 * DEPTH) ** 0.25)
EPS = 1e-5

_IN_SPLITS = (512, 512, 512, 512, 8, 8, 512, 512, 512, 512, 512, 512, 4096)
_IN_OFFS = tuple(sum(_IN_SPLITS[:i]) for i in range(len(_IN_SPLITS)))

LANE = 128
SUBLANE = 8
C_QKV = 0
C_Z = 1536
C_SU = 2048
C_LX = 2560
C_LG = 3072
C_AB = 3584
W_MAIN = 3712

S5_L = 64
S5_LW = S5_L * S5_P
S5_HW = 4 * LANE

VMEM_LIMIT = 56 * 1024 * 1024


def _cparams(sem):
    return pltpu.CompilerParams(dimension_semantics=sem, vmem_limit_bytes=VMEM_LIMIT)


def _layer_norm(x, g, b):
    mu = jnp.mean(x, axis=-1, keepdims=True)
    xc = x - mu
    var = jnp.mean(xc * xc, axis=-1, keepdims=True)
    return xc * lax.rsqrt(var + EPS) * g + b


def _const_spec(shape):
    nd = len(shape)
    return pl.BlockSpec(shape, lambda *_: (0,) * nd, pipeline_mode=pl.Buffered(1))


def _ln_kernel(x_ref, g_ref, b_ref, o_ref):
    o_ref[...] = _layer_norm(x_ref[...], g_ref[...], b_ref[...])


def _input_ln(x, g, b, tm=512):
    n = x.shape[0]
    return pl.pallas_call(
        _ln_kernel,
        grid=(n // tm,),
        in_specs=[pl.BlockSpec((tm, D_MODEL), lambda i: (i, 0)), _const_spec((1, D_MODEL)), _const_spec((1, D_MODEL))],
        out_specs=pl.BlockSpec((tm, D_MODEL), lambda i: (i, 0)),
        out_shape=jax.ShapeDtypeStruct((n, D_MODEL), F32),
        compiler_params=_cparams(("parallel",)),
        name="input_ln",
    )(x, g.reshape(1, -1), b.reshape(1, -1))


def _proj_kernel(x_ref, wm_ref, wn_ref, hm_ref, na_ref):
    xb = x_ref[...].astype(BF16)
    step = 4 * LANE
    for c0 in range(0, W_MAIN, step):
        c1 = min(c0 + step, W_MAIN)
        hm_ref[:, c0:c1] = jnp.dot(xb, wm_ref[:, c0:c1], preferred_element_type=F32)
    for c0 in range(0, 3 * NA_WIDTH, step):
        na_ref[:, c0:c0 + step] = jnp.dot(xb, wn_ref[:, c0:c0 + step], preferred_element_type=F32).astype(BF16)


def _in_proj(x, w_main, w_na, tm=256):
    n = x.shape[0]
    return pl.pallas_call(
        _proj_kernel,
        grid=(n // tm,),
        in_specs=[pl.BlockSpec((tm, D_MODEL), lambda i: (i, 0)),
                  _const_spec((D_MODEL, W_MAIN)), _const_spec((D_MODEL, 3 * NA_WIDTH))],
        out_specs=[pl.BlockSpec((tm, W_MAIN), lambda i: (i, 0)), pl.BlockSpec((tm, 3 * NA_WIDTH), lambda i: (i, 0))],
        out_shape=[jax.ShapeDtypeStruct((n, W_MAIN), F32), jax.ShapeDtypeStruct((n, 3 * NA_WIDTH), BF16)],
        compiler_params=_cparams(("parallel",)),
        name="in_proj",
    )(x, w_main, w_na)


def _conv_centred(prev8, cur, next8, w, first, last):
    tb = cur.shape[0]
    prev8 = jnp.where(first, 0.0, prev8)
    next8 = jnp.where(last, 0.0, next8)
    xp = jnp.concatenate([prev8, cur, next8], axis=0)
    left = (w.shape[0] - 1) // 2
    acc = None
    for j in range(w.shape[0]):
        s = SUBLANE - left + j
        term = xp[s:s + tb] * w[j:j + 1]
        acc = term if acc is None else acc + term
    return acc


def _dot_hi(a, b):
    return jnp.dot(a, b, preferred_element_type=F32, precision=lax.Precision.HIGHEST)


def _dot_bf(a, b):
    return jnp.dot(a.astype(BF16), b.astype(BF16), preferred_element_type=F32)


def _dot_nt_bf(a, b):
    return lax.dot_general(a.astype(BF16), b.astype(BF16), (((1,), (1,)), ((), ())), preferred_element_type=F32)


def _dn_prepare(d, cur, prev8, next8, ab, first, last, cw, alog, dtb):
    tb = cur.shape[0]
    c = DN_CHUNK
    qkv = _conv_centred(prev8, cur, next8, cw, first, last)
    qkv = qkv * jax.nn.sigmoid(qkv)
    gates = -jnp.exp(alog) * jax.nn.softplus(ab + dtb)
    betas = jax.nn.sigmoid(ab)

    row = lax.broadcasted_iota(jnp.int32, (c, c), 0)
    col = lax.broadcasted_iota(jnp.int32, (c, c), 1)
    incl = col <= row if d == 0 else col >= row
    strict = col < row if d == 0 else col > row
    eye = (row == col).astype(F32)
    edge = c - 1 if d == 0 else 0
    n_chunks = tb // c

    rb = lax.broadcasted_iota(jnp.int32, (tb, tb), 0)
    cb = lax.broadcasted_iota(jnp.int32, (tb, tb), 1)
    tri = ((rb // c) == (cb // c)) & (cb <= rb if d == 0 else cb >= rb)
    gc_all = _dot_hi(tri.astype(F32), gates)

    qn, kn = [], []
    for h in range(DN_HEADS):
        q = qkv[:, h * DN_DK:(h + 1) * DN_DK]
        k = qkv[:, 512 + h * DN_DK:512 + (h + 1) * DN_DK]
        qn.append(q * (lax.rsqrt(jnp.sum(q * q, axis=-1, keepdims=True) + 1e-6) * (DN_DK ** -0.5)))
        kn.append(k * lax.rsqrt(jnp.sum(k * k, axis=-1, keepdims=True) + 1e-6))

    units = [(ci, h) for ci in range(n_chunks) for h in range(DN_HEADS)]
    rows = lambda ci: slice(ci * c, (ci + 1) * c)
    lane_of = lambda h: d * DN_HEADS + h
    q_u = [qn[h][rows(ci)] for ci, h in units]
    k_u = [kn[h][rows(ci)] for ci, h in units]
    v_u = [qkv[rows(ci), 1024 + h * DN_DV:1024 + (h + 1) * DN_DV] for ci, h in units]
    beta_u = [jnp.broadcast_to(betas[rows(ci), 8 + lane_of(h):9 + lane_of(h)], (c, DN_DK)) for ci, h in units]
    gcr_u = [jnp.broadcast_to(gc_all[rows(ci), lane_of(h):lane_of(h) + 1], (c, DN_DK)) for ci, h in units]
    gcl_u = [g.T[:c, :] for g in gcr_u]
    decay_u = [jnp.exp(jnp.where(incl, gr[:, :c] - gl, -1e30)) for gr, gl in zip(gcr_u, gcl_u)]
    egc_u = [jnp.exp(g) for g in gcr_u]
    glast_u = [g[edge:edge + 1, :] for g in gcr_u]
    kb_u = [k * b for k, b in zip(k_u, beta_u)]
    vb_u = [v * b for v, b in zip(v_u, beta_u)]
    kk_u = [_dot_nt_bf(jnp.concatenate([kb, q], axis=0), k) for kb, q, k in zip(kb_u, q_u, k_u)]
    x_u = [-jnp.where(strict, kk[:c] * dec, 0.0) for kk, dec in zip(kk_u, decay_u)]
    qk_u = [jnp.where(incl, kk[c:] * dec, 0.0).astype(BF16) for kk, dec in zip(kk_u, decay_u)]
    t_u = [eye + x for x in x_u]
    x_u = [_dot_bf(x, x) for x in x_u]
    for _ in range(4):
        both = [_dot_bf(jnp.concatenate([t, x], axis=0), x) for t, x in zip(t_u, x_u)]
        t_u = [t + b[:c] for t, b in zip(t_u, both)]
        x_u = [b[c:] for b in both]
    t_u = [t + _dot_bf(t, x) for t, x in zip(t_u, x_u)]
    sol_u = [_dot_bf(t, jnp.concatenate([vb, kb * e], axis=1)) for t, vb, kb, e in zip(t_u, vb_u, kb_u, egc_u)]
    u_u = [s[:, :DN_DV] for s in sol_u]
    wq_u = [jnp.concatenate([s[:, DN_DV:], q * e], axis=0).astype(BF16) for s, q, e in zip(sol_u, q_u, egc_u)]
    qkkd_u = [jnp.concatenate([qk, (k * jnp.exp(gl - g)).T.astype(BF16)], axis=0)
              for qk, k, gl, g in zip(qk_u, k_u, glast_u, gcr_u)]
    egl_u = [jnp.exp(gl) for gl in glast_u]
    return dict(zip(units, zip(u_u, wq_u, qkkd_u, egl_u)))


def _dn_kernel(cur_f, prev_f, next_f, ab_f, cur_b, prev_b, next_b, ab_b, cw_ref, alog_ref, dtb_ref,
               of_ref, ob_ref, s_ref):
    i = pl.program_id(1)
    nb = pl.num_programs(1)

    @pl.when(i == 0)
    def _():
        s_ref[...] = jnp.zeros_like(s_ref)

    cw = cw_ref[...]
    alog = alog_ref[...]
    dtb = dtb_ref[...]
    prep = (_dn_prepare(0, cur_f[...], prev_f[...], next_f[...], ab_f[...], i == 0, i == nb - 1, cw, alog, dtb),
            _dn_prepare(1, cur_b[...], prev_b[...], next_b[...], ab_b[...], i == nb - 1, i == 0, cw, alog, dtb))
    c = DN_CHUNK
    n_chunks = cur_f.shape[0] // c
    o_refs = (of_ref, ob_ref)
    chains = [(d, h) for d in range(2) for h in range(DN_HEADS)]
    chunk_at = lambda d, step: step if d == 0 else n_chunks - 1 - step
    state = [s_ref[d * DN_HEADS + h] for d, h in chains]
    for step in range(n_chunks):
        ops = [prep[d][(chunk_at(d, step), h)] for d, h in chains]
        ws = [jnp.dot(wq, s.astype(BF16), preferred_element_type=F32) for (_, wq, _, _), s in zip(ops, state)]
        v_new = [(u - w[:c]).astype(BF16) for (u, _, _, _), w in zip(ops, ws)]
        upd = [jnp.dot(qkkd, vn, preferred_element_type=F32) for (_, _, qkkd, _), vn in zip(ops, v_new)]
        state = [s * egl + up[c:] for s, (_, _, _, egl), up in zip(state, ops, upd)]
        for (d, h), w, up in zip(chains, ws, upd):
            r0 = chunk_at(d, step) * c
            o_refs[d][r0:r0 + c, h * DN_DV:(h + 1) * DN_DV] = w[c:] + up[:c]
    for (d, h), s in zip(chains, state):
        s_ref[d * DN_HEADS + h] = s


def _halo_specs(tb, width, col_block, nb, t, reverse):
    per = tb // SUBLANE
    last8 = t // SUBLANE - 1
    if reverse:
        blk = lambda i: nb - 1 - i
    else:
        blk = lambda i: i
    cur = pl.BlockSpec((None, tb, width), lambda b, i: (b, blk(i), col_block))
    prev = pl.BlockSpec((None, SUBLANE, width), lambda b, i: (b, jnp.maximum(blk(i) * per - 1, 0), col_block))
    nxt = pl.BlockSpec((None, SUBLANE, width), lambda b, i: (b, jnp.minimum((blk(i) + 1) * per, last8), col_block))
    return cur, prev, nxt


def _deltanet(hm3, conv_w, a_log, dt_bias, tb=256):
    bsz, t, _ = hm3.shape
    nb = t // tb
    qkv_w = 3 * 512
    alog_v = jnp.zeros((1, LANE), F32).at[0, :8].set(a_log.reshape(-1))
    dtb_v = jnp.zeros((1, LANE), F32).at[0, :8].set(dt_bias.reshape(-1))
    ab_col = C_AB // LANE
    in_specs = []
    for rev in (False, True):
        in_specs += list(_halo_specs(tb, qkv_w, 0, nb, t, rev))
        if rev:
            in_specs.append(pl.BlockSpec((None, tb, LANE), lambda b, i: (b, nb - 1 - i, ab_col)))
        else:
            in_specs.append(pl.BlockSpec((None, tb, LANE), lambda b, i: (b, i, ab_col)))
    in_specs += [_const_spec((DN_CONV, qkv_w)), _const_spec((1, LANE)), _const_spec((1, LANE))]
    out_specs = [pl.BlockSpec((None, tb, 512), lambda b, i: (b, i, 0)),
                 pl.BlockSpec((None, tb, 512), lambda b, i: (b, nb - 1 - i, 0))]
    return pl.pallas_call(
        _dn_kernel,
        grid=(bsz, nb),
        in_specs=in_specs,
        out_specs=out_specs,
        out_shape=[jax.ShapeDtypeStruct((bsz, t, 512), F32)] * 2,
        scratch_shapes=[pltpu.VMEM((2 * DN_HEADS, DN_DK, DN_DV), F32)],
        compiler_params=_cparams(("parallel", "arbitrary")),
        name="deltanet",
    )(hm3, hm3, hm3, hm3, hm3, hm3, hm3, hm3, conv_w, alog_v, dtb_v)


def _s5_discretise(lam_re, lam_im, log_dt, b_re, b_im):
    dt = jnp.exp(log_dt)[:, None]
    mag = jnp.exp(lam_re * dt)
    ab_re = mag * jnp.cos(lam_im * dt)
    ab_im = mag * jnp.sin(lam_im * dt)
    den = jnp.square(lam_re) + jnp.square(lam_im)
    nr, ni = ab_re - 1.0, ab_im
    kr = ((nr * lam_re + ni * lam_im) / den)[..., None]
    ki = ((ni * lam_re - nr * lam_im) / den)[..., None]
    return kr * b_re - ki * b_im, kr * b_im + ki * b_re


def _s5_operators(lam_re, lam_im, log_dt, b_re, b_im, c_re, c_im):
    hi = lax.Precision.HIGHEST
    L, G, N, P = S5_L, S5_GROUPS, S5_N, S5_P
    j = jnp.arange(L + 1, dtype=F32)[:, None, None]
    kcomb = 0.0
    e_cols, f_rows, al = [], [], []
    for d in range(2):
        dt = jnp.exp(log_dt[d])[:, None]
        bb_re, bb_im = _s5_discretise(lam_re[d], lam_im[d], log_dt[d], b_re, b_im)
        mag = jnp.exp(lam_re[d] * dt * j)
        ang = lam_im[d] * dt * j
        aj_re, aj_im = mag * jnp.cos(ang), mag * jnp.sin(ang)
        ca_re = c_re[None] * aj_re[:, :, None, :] - c_im[None] * aj_im[:, :, None, :]
        ca_im = c_re[None] * aj_im[:, :, None, :] + c_im[None] * aj_re[:, :, None, :]
        kj = (jnp.einsum('jgpn,gnq->jgpq', ca_re[:L], bb_re, precision=hi)
              - jnp.einsum('jgpn,gnq->jgpq', ca_im[:L], bb_im, precision=hi))
        zeros = jnp.zeros((L - 1,) + kj.shape[1:], F32)
        if d == 0:
            kcomb = kcomb + jnp.concatenate([zeros, kj], axis=0)
        else:
            kcomb = kcomb + jnp.concatenate([kj[::-1], zeros], axis=0)
        pw_re = aj_re[:L][::-1] if d == 0 else aj_re[:L]
        pw_im = aj_im[:L][::-1] if d == 0 else aj_im[:L]
        e_re = pw_re[..., None] * bb_re[None] - pw_im[..., None] * bb_im[None]
        e_im = pw_re[..., None] * bb_im[None] + pw_im[..., None] * bb_re[None]
        for e in (e_re, e_im):
            e = e.transpose(1, 0, 3, 2).reshape(G, L * P, N)
            e_cols.append(jnp.pad(e, ((0, 0), (0, 0), (0, LANE - N))))
        sel = slice(1, L + 1)
        fr = ca_re[sel] if d == 0 else ca_re[sel][::-1]
        fi = ca_im[sel] if d == 0 else ca_im[sel][::-1]
        for f in (fr, -fi):
            f = f.transpose(1, 3, 0, 2).reshape(G, N, L * P)
            f_rows.append(jnp.pad(f, ((0, 0), (0, LANE - N), (0, 0))))
        al += [jnp.pad(aj_re[L], ((0, 0), (0, LANE - N))), jnp.pad(aj_im[L], ((0, 0), (0, LANE - N)))]
    kc = kcomb.astype(BF16).transpose(1, 3, 0, 2)
    toep = jnp.stack([kc[:, :, L - 1 - s:2 * L - 1 - s, :] for s in range(L)], axis=1)
    toep = toep.reshape(G, L * P, L * P)
    w1 = jnp.concatenate([toep] + [e.astype(BF16) for e in e_cols], axis=2)
    w2 = jnp.concatenate(f_rows, axis=1).astype(BF16)
    return w1, w2, jnp.stack(al, axis=1)


def _s5_kernel(u_ref, w1_ref, w2_ref, al_ref, y_ref, hloc_ref, hin_ref, *, n_chunks, bsz):
    u = u_ref[...]
    y_ref[...] = jnp.dot(u, w1_ref[:, :S5_LW], preferred_element_type=F32)
    hloc_ref[...] = jnp.dot(u, w1_ref[:, S5_LW:], preferred_element_type=F32)
    al = al_ref[...]
    a_re = (al[0:1], al[2:3])
    a_im = (al[1:2], al[3:4])

    def body(cidx, carry):
        new = []
        for d in range(2):
            cr, ci = carry[2 * d], carry[2 * d + 1]
            cc = cidx if d == 0 else n_chunks - 1 - cidx
            rows = pl.ds(pl.multiple_of(cc * bsz, bsz), bsz)
            hin_ref[rows, 2 * d * LANE:(2 * d + 1) * LANE] = cr
            hin_ref[rows, (2 * d + 1) * LANE:(2 * d + 2) * LANE] = ci
            lr = hloc_ref[rows, 2 * d * LANE:(2 * d + 1) * LANE]
            li = hloc_ref[rows, (2 * d + 1) * LANE:(2 * d + 2) * LANE]
            new += [a_re[d] * cr - a_im[d] * ci + lr, a_re[d] * ci + a_im[d] * cr + li]
        return tuple(new)

    zero = jnp.zeros((bsz, LANE), F32)
    lax.fori_loop(0, n_chunks, body, (zero, zero, zero, zero))
    y_ref[...] += jnp.dot(hin_ref[...].astype(BF16), w2_ref[...], preferred_element_type=F32)


def _s5(hm3, w1, w2, al):
    bsz, t, _ = hm3.shape
    n_chunks = t // S5_L
    rows = n_chunks * bsz
    u = hm3[:, :, C_SU:C_SU + S5_WIDTH].reshape(bsz, n_chunks, S5_L, S5_GROUPS, S5_P)
    u = u.transpose(3, 1, 0, 2, 4).reshape(S5_GROUPS, rows, S5_LW).astype(BF16)
    y = pl.pallas_call(
        functools.partial(_s5_kernel, n_chunks=n_chunks, bsz=bsz),
        grid=(S5_GROUPS,),
        in_specs=[pl.BlockSpec((None, rows, S5_LW), lambda g: (g, 0, 0)),
                  pl.BlockSpec((None, S5_LW, S5_LW + S5_HW), lambda g: (g, 0, 0)),
                  pl.BlockSpec((None, S5_HW, S5_LW), lambda g: (g, 0, 0)),
                  pl.BlockSpec((None, 4, LANE), lambda g: (g, 0, 0))],
        out_specs=pl.BlockSpec((None, rows, S5_LW), lambda g: (g, 0, 0)),
        out_shape=jax.ShapeDtypeStruct((S5_GROUPS, rows, S5_LW), F32),
        scratch_shapes=[pltpu.VMEM((rows, S5_HW), F32), pltpu.VMEM((rows, S5_HW), F32)],
        compiler_params=_cparams(("parallel",)),
        name="s5",
    )(u, w1, w2, al)
    y = y.reshape(S5_GROUPS, n_chunks, bsz, S5_L, S5_P).transpose(2, 1, 3, 0, 4)
    return y.reshape(bsz * t, S5_WIDTH)


def _na_bias_table(rpb):
    c = np.arange(GRID_W)
    c0 = np.clip(c - NA_KW // 2, 0, GRID_W - NA_KW)
    col_in = (c[None, :] >= c0[:, None]) & (c[None, :] < c0[:, None] + NA_KW)
    dc = np.clip(c[None, :] - c[:, None], -(NA_KW - 1), NA_KW - 1) + (NA_KW - 1)
    onehot = (dc[:, :, None] == np.arange(2 * NA_KW - 1)).astype(np.float32)
    by_col = jnp.einsum('hrm,qkm->hrqk', rpb.astype(F32), onehot, precision=lax.Precision.HIGHEST)
    tab = jnp.stack([by_col[:, NA_KH - 1 - dl:2 * NA_KH - 1 - dl] for dl in range(NA_KH)], axis=1)
    tab = jnp.where(col_in[None, None, None], tab, -1e30)
    return tab.transpose(0, 1, 3, 2, 4).reshape(NA_HEADS, NA_KH, GRID_W, NA_KH * GRID_W)


def _na_kernel(q_ref, k_ref, v_ref, tab_ref, o_ref, *, rows_per_step, n_rows):
    i = pl.program_id(2)
    w = GRID_W
    nk = NA_KH * w
    lane = lax.broadcasted_iota(jnp.int32, (2 * w, LANE), 1)
    rowi = lax.broadcasted_iota(jnp.int32, (2 * w, LANE), 0)
    own = (lane < NA_DH) == (rowi < w)
    low = lax.broadcasted_iota(jnp.int32, (w, LANE), 1) < NA_DH
    steps = range(rows_per_step)
    r = [i * rows_per_step + rr for rr in steps]
    r0 = [jnp.clip(x - NA_KH // 2, 0, n_rows - NA_KH) for x in r]
    krows = [pl.ds(pl.multiple_of(x * w, w), nk) for x in r0]
    q2 = [q_ref[rr * w:(rr + 1) * w, :] for rr in steps]
    q2 = [jnp.where(own, jnp.concatenate([q, q], axis=0), jnp.zeros((2 * w, LANE), q.dtype)) for q in q2]
    s = [lax.dot_general(q, k_ref[kr, :], (((1,), (1,)), ((), ())), preferred_element_type=F32)
         for q, kr in zip(q2, krows)]
    s = [jnp.concatenate([x[:w] + tab_ref[0, a - b], x[w:] + tab_ref[1, a - b]], axis=0) for x, a, b in zip(s, r, r0)]
    m = [jnp.max(x, axis=-1, keepdims=True) for x in s]
    p = [jnp.exp(x - y) for x, y in zip(s, m)]
    l = [jnp.sum(x, axis=-1, keepdims=True) for x in p]
    pv = [jnp.dot(x.astype(BF16), v_ref[kr, :], preferred_element_type=F32) / y for x, kr, y in zip(p, krows, l)]
    for rr, x in zip(steps, pv):
        o_ref[rr * w:(rr + 1) * w, :] = jnp.where(low, x[:w], x[w:]).astype(o_ref.dtype)


def _natten(na3, table, rows_per_step=8):
    bsz, t, _ = na3.shape
    n_rows = t // GRID_W
    tq = rows_per_step * GRID_W
    pairs = NA_HEADS // 2
    kcol = NA_WIDTH // LANE
    return pl.pallas_call(
        functools.partial(_na_kernel, rows_per_step=rows_per_step, n_rows=n_rows),
        grid=(pairs, bsz, t // tq),
        in_specs=[pl.BlockSpec((None, tq, LANE), lambda p, b, i: (b, i, p)),
                  pl.BlockSpec((None, t, LANE), lambda p, b, i: (b, 0, kcol + p)),
                  pl.BlockSpec((None, t, LANE), lambda p, b, i: (b, 0, 2 * kcol + p)),
                  pl.BlockSpec((2, NA_KH, GRID_W, NA_KH * GRID_W), lambda p, b, i: (p, 0, 0, 0))],
        out_specs=pl.BlockSpec((None, tq, LANE), lambda p, b, i: (b, i, p)),
        out_shape=jax.ShapeDtypeStruct((bsz, t, NA_WIDTH), BF16),
        compiler_params=_cparams(("parallel", "parallel", "parallel")),
        name="natten",
    )(na3, na3, na3, table)


def _lru_coefficients(d, cur, prev8, next8, first, last, cw, cb, wg_ref, gb, sp_lam, a_scr, b_scr):
    xc = _conv_centred(prev8, cur, next8, cw, first, last) + cb
    width = 2 * LRU_WIDTH
    gates = jnp.dot(xc.astype(BF16), wg_ref[:, d * width:(d + 1) * width], preferred_element_type=F32)
    gates = jax.nn.sigmoid(gates + gb[:, d * width:(d + 1) * width])
    log_a = -LRU_C * gates[:, :LRU_WIDTH] * sp_lam[d:d + 1]
    a = jnp.exp(log_a)
    a_scr[d] = a
    b_scr[d] = jnp.sqrt(1.0 - a * a) * gates[:, LRU_WIDTH:] * xc


def _lru_scan(a_scr, b_scr, carry_ref, o_refs):
    n_groups = a_scr.shape[1] // SUBLANE

    def body(gi, hs):
        new = []
        for d in range(2):
            h = hs[d]
            grp = gi if d == 0 else n_groups - 1 - gi
            rows = pl.ds(pl.multiple_of(grp * SUBLANE, SUBLANE), SUBLANE)
            a8 = a_scr[d, rows, :]
            b8 = b_scr[d, rows, :]
            out = [None] * SUBLANE
            for r in (range(SUBLANE) if d == 0 else range(SUBLANE - 1, -1, -1)):
                h = a8[r:r + 1] * h + b8[r:r + 1]
                out[r] = h
            o_refs[d][rows, :] = jnp.concatenate(out, axis=0)
            new.append(h)
        return tuple(new)

    hf, hb = lax.fori_loop(0, n_groups, body, (carry_ref[0:1], carry_ref[1:2]))
    carry_ref[0:1] = hf
    carry_ref[1:2] = hb


def _lru_kernel(cur_f, prev_f, next_f, cur_b, prev_b, next_b, cw_ref, cb_ref, wg_ref, gb_ref, lam_ref,
                hf_ref, hb_ref, a_scr, b_scr, carry_ref):
    i = pl.program_id(1)
    nb = pl.num_programs(1)

    @pl.when(i == 0)
    def _():
        carry_ref[...] = jnp.zeros_like(carry_ref)

    cw = cw_ref[...]
    cb = cb_ref[...]
    gb = gb_ref[...]
    sp_lam = jax.nn.softplus(-lam_ref[...])
    _lru_coefficients(0, cur_f[...], prev_f[...], next_f[...], i == 0, i == nb - 1, cw, cb, wg_ref, gb, sp_lam,
                      a_scr, b_scr)
    _lru_coefficients(1, cur_b[...], prev_b[...], next_b[...], i == nb - 1, i == 0, cw, cb, wg_ref, gb, sp_lam,
                      a_scr, b_scr)
    _lru_scan(a_scr, b_scr, carry_ref, (hf_ref, hb_ref))


def _lru_gate_matrix(gate_w):
    eye = jnp.eye(LRU_BLOCKS, dtype=gate_w.dtype)
    full = jnp.einsum('dgncm,nk->ncdgkm', gate_w, eye)
    return full.reshape(LRU_WIDTH, 4 * LRU_WIDTH)


def _rglru(hm3, conv_w, conv_b, wg, gate_b, lam, tb=256):
    bsz, t, _ = hm3.shape
    nb = t // tb
    col = C_LX // LRU_WIDTH
    in_specs = list(_halo_specs(tb, LRU_WIDTH, col, nb, t, False)) + list(_halo_specs(tb, LRU_WIDTH, col, nb, t, True))
    in_specs += [_const_spec((LRU_CONV, LRU_WIDTH)), _const_spec((1, LRU_WIDTH)),
                 _const_spec((LRU_WIDTH, 4 * LRU_WIDTH)), _const_spec((1, 4 * LRU_WIDTH)), _const_spec((2, LRU_WIDTH))]
    out_specs = [pl.BlockSpec((None, tb, LRU_WIDTH), lambda b, i: (b, i, 0)),
                 pl.BlockSpec((None, tb, LRU_WIDTH), lambda b, i: (b, nb - 1 - i, 0))]
    return pl.pallas_call(
        _lru_kernel,
        grid=(bsz, nb),
        in_specs=in_specs,
        out_specs=out_specs,
        out_shape=[jax.ShapeDtypeStruct((bsz, t, LRU_WIDTH), F32)] * 2,
        scratch_shapes=[pltpu.VMEM((2, tb, LRU_WIDTH), F32), pltpu.VMEM((2, tb, LRU_WIDTH), F32),
                        pltpu.VMEM((2, LRU_WIDTH), F32)],
        compiler_params=_cparams(("parallel", "arbitrary")),
        name="rglru",
    )(hm3, hm3, hm3, hm3, hm3, hm3, conv_w, conv_b.reshape(1, -1), wg, gate_b.reshape(1, -1), lam)


def _merge_kernel(x_ref, of_ref, ob_ref, z_ref, ys_ref, su_ref, na_ref, hf_ref, hb_ref, lg_ref,
                  ng_ref, sd_ref, gw_ref, gbias_ref, wgt_ref, wbr_ref, wout_ref, lng_ref, lnb_ref, o_ref):
    x = x_ref[...]
    xb = x.astype(BF16)
    o = of_ref[...] + ob_ref[...]
    z = z_ref[...]
    parts = []
    for h in range(DN_HEADS):
        oh = o[:, h * DN_DV:(h + 1) * DN_DV]
        ms = jnp.mean(oh * oh, axis=-1, keepdims=True)
        parts.append(oh * lax.rsqrt(ms + EPS) * ng_ref[...])
    y_a = jnp.concatenate(parts, axis=1) * (z * jax.nn.sigmoid(z))
    y = jax.nn.gelu(ys_ref[...] + sd_ref[...] * su_ref[...])
    y_b = y * jax.nn.sigmoid(jnp.dot(y.astype(BF16), gw_ref[...], preferred_element_type=F32) + gbias_ref[...])
    y_d = (hf_ref[...] + hb_ref[...]) * jax.nn.gelu(lg_ref[...])
    ys = (y_a.astype(BF16), y_b.astype(BF16), na_ref[...], y_d.astype(BF16))
    acc = None
    for n in range(N_BRANCH):
        gate = jax.nn.sigmoid(jnp.dot(xb, wgt_ref[:, n * D_MODEL:(n + 1) * D_MODEL], preferred_element_type=F32))
        term = gate * jnp.dot(ys[n], wbr_ref[n], preferred_element_type=F32)
        acc = term if acc is None else acc + term
    mix = jnp.dot(acc.astype(BF16), wout_ref[...], preferred_element_type=F32)
    o_ref[...] = _layer_norm(ALPHA * x + mix, lng_ref[...], lnb_ref[...])


def _merge(x, o_f, o_b, hm, y_s5, na_o, h_f, h_b, norm_g, s5_d, glu_w, glu_b, w_gate, w_branch, w_out, ln_g, ln_b,
           tm=256):
    n = x.shape[0]
    tok = lambda w, cb=0: pl.BlockSpec((tm, w), lambda i: (i, cb))
    in_specs = [tok(D_MODEL), tok(512), tok(512), tok(512, C_Z // 512), tok(512), tok(512, C_SU // 512), tok(512),
                tok(512), tok(512), tok(512, C_LG // 512),
                _const_spec((1, DN_DV)), _const_spec((1, S5_WIDTH)), _const_spec((S5_WIDTH, S5_WIDTH)),
                _const_spec((1, S5_WIDTH)), _const_spec((D_MODEL, N_BRANCH * D_MODEL)),
                _const_spec((N_BRANCH, BRANCH_W, D_MODEL)), _const_spec((D_MODEL, D_MODEL)),
                _const_spec((1, D_MODEL)), _const_spec((1, D_MODEL))]
    return pl.pallas_call(
        _merge_kernel,
        grid=(n // tm,),
        in_specs=in_specs,
        out_specs=pl.BlockSpec((tm, D_MODEL), lambda i: (i, 0)),
        out_shape=jax.ShapeDtypeStruct((n, D_MODEL), F32),
        compiler_params=_cparams(("parallel",)),
        name="merge",
    )(x, o_f, o_b, hm, y_s5, hm, na_o, h_f, h_b, hm, norm_g.reshape(1, -1), s5_d.reshape(1, -1), glu_w,
      glu_b.reshape(1, -1), w_gate, w_branch, w_out, ln_g.reshape(1, -1), ln_b.reshape(1, -1))


def _mlp_kernel(x_ref, w1_ref, b1_ref, w2_ref, b2_ref, g_ref, b_ref, o_ref):
    x = x_ref[...]
    xb = x.astype(BF16)
    acc = None
    for c0 in range(0, D_FF, D_MODEL):
        f = jnp.dot(xb, w1_ref[:, c0:c0 + D_MODEL], preferred_element_type=F32) + b1_ref[:, c0:c0 + D_MODEL]
        f = jnp.square(jnp.maximum(f, 0.0))
        term = jnp.dot(f.astype(BF16), w2_ref[c0:c0 + D_MODEL, :], preferred_element_type=F32)
        acc = term if acc is None else acc + term
    o_ref[...] = _layer_norm(ALPHA * x + acc + b2_ref[...], g_ref[...], b_ref[...])


def _mlp(x, w1, b1, w2, b2, g, b, tm=512):
    n = x.shape[0]
    return pl.pallas_call(
        _mlp_kernel,
        grid=(n // tm,),
        in_specs=[pl.BlockSpec((tm, D_MODEL), lambda i: (i, 0)), _const_spec((D_MODEL, D_FF)), _const_spec((1, D_FF)),
                  _const_spec((D_FF, D_MODEL)), _const_spec((1, D_MODEL)), _const_spec((1, D_MODEL)),
                  _const_spec((1, D_MODEL))],
        out_specs=pl.BlockSpec((tm, D_MODEL), lambda i: (i, 0)),
        out_shape=jax.ShapeDtypeStruct((n, D_MODEL), F32),
        compiler_params=_cparams(("parallel",)),
        name="mlp",
    )(x, w1, b1.reshape(1, -1), w2, b2.reshape(1, -1), g.reshape(1, -1), b.reshape(1, -1))


def _prepare_layer(l, p):
    w_in = p['w_in'][l]
    col = lambda i: w_in[:, _IN_OFFS[i]:_IN_OFFS[i] + _IN_SPLITS[i]]
    pad = jnp.zeros((D_MODEL, W_MAIN - C_AB - 16), F32)
    w_main = jnp.concatenate([col(0), col(1), col(2), col(3), col(6), col(10), col(11), col(4), col(5), pad], axis=1)
    w_na = jnp.concatenate([col(7) * (NA_DH ** -0.5), col(8), col(9)], axis=1)
    s5_w1, s5_w2, s5_al = _s5_operators(p['s5_lambda_re'][l], p['s5_lambda_im'][l], p['s5_log_dt'][l],
                                        p['s5_b_re'][l], p['s5_b_im'][l], p['s5_c_re'][l], p['s5_c_im'][l])
    return dict(
        w_main=w_main.astype(BF16), w_na=w_na.astype(BF16), w_gate=col(12).astype(BF16),
        dn_conv_w=p['dn_conv_w'][l], dn_a_log=p['dn_a_log'][l], dn_dt_bias=p['dn_dt_bias'][l],
        dn_norm_g=p['dn_norm_g'][l],
        s5_w1=s5_w1, s5_w2=s5_w2, s5_al=s5_al, s5_d=p['s5_d'][l], glu_w=p['s5_glu_w'][l].astype(BF16),
        glu_b=p['s5_glu_b'][l],
        na_table=_na_bias_table(p['na_rpb'][l]),
        lru_conv_w=p['lru_conv_w'][l], lru_conv_b=p['lru_conv_b'][l],
        lru_wg=_lru_gate_matrix(p['lru_gate_w'][l]).astype(BF16), lru_gate_b=p['lru_gate_b'][l],
        lru_lambda=p['lru_lambda'][l],
        w_branch=p['w_branch'][l].astype(BF16), w_out=p['w_out'][l].astype(BF16),
        ln1_g=p['ln1_g'][l], ln1_b=p['ln1_b'][l],
        mlp_w1=p['mlp_w1'][l].astype(BF16), mlp_b1=p['mlp_b1'][l], mlp_w2=p['mlp_w2'][l].astype(BF16),
        mlp_b2=p['mlp_b2'][l], ln2_g=p['ln2_g'][l], ln2_b=p['ln2_b'][l])


def _layer(x, bsz, t, lw):
    hm, na = _in_proj(x, lw['w_main'], lw['w_na'])
    hm3 = hm.reshape(bsz, t, W_MAIN)
    o_f, o_b = _deltanet(hm3, lw['dn_conv_w'], lw['dn_a_log'], lw['dn_dt_bias'])
    y_s5 = _s5(hm3, lw['s5_w1'], lw['s5_w2'], lw['s5_al'])
    na_o = _natten(na.reshape(bsz, t, 3 * NA_WIDTH), lw['na_table'])
    h_f, h_b = _rglru(hm3, lw['lru_conv_w'], lw['lru_conv_b'], lw['lru_wg'], lw['lru_gate_b'], lw['lru_lambda'])
    n = bsz * t
    x1 = _merge(x, o_f.reshape(n, 512), o_b.reshape(n, 512), hm, y_s5, na_o.reshape(n, NA_WIDTH),
                h_f.reshape(n, LRU_WIDTH), h_b.reshape(n, LRU_WIDTH), lw['dn_norm_g'], lw['s5_d'], lw['glu_w'],
                lw['glu_b'], lw['w_gate'], lw['w_branch'], lw['w_out'], lw['ln1_g'], lw['ln1_b'])
    return _mlp(x1, lw['mlp_w1'], lw['mlp_b1'], lw['mlp_w2'], lw['mlp_b2'], lw['ln2_g'], lw['ln2_b'])


def _trunk(x, ln_g, ln_b, layers):
    bsz, t, _ = x.shape
    h = _input_ln(x.reshape(bsz * t, D_MODEL), ln_g, ln_b)
    for lw in layers:
        h = _layer(h, bsz, t, lw)
    return h.reshape(bsz, t, D_MODEL)


def kernel(x_prompt, x_sample, ln_in_g, ln_in_b, w_in, dn_conv_w, dn_a_log, dn_dt_bias, dn_norm_g, s5_lambda_re,
           s5_lambda_im, s5_log_dt, s5_b_re, s5_b_im, s5_c_re, s5_c_im, s5_d, s5_glu_w, s5_glu_b, na_rpb, lru_conv_w,
           lru_conv_b, lru_gate_w, lru_gate_b, lru_lambda, w_branch, w_out, ln1_g, ln1_b, mlp_w1, mlp_b1, mlp_w2,
           mlp_b2, ln2_g, ln2_b):
    p = dict(w_in=w_in, dn_conv_w=dn_conv_w, dn_a_log=dn_a_log, dn_dt_bias=dn_dt_bias, dn_norm_g=dn_norm_g,
             s5_lambda_re=s5_lambda_re, s5_lambda_im=s5_lambda_im, s5_log_dt=s5_log_dt, s5_b_re=s5_b_re,
             s5_b_im=s5_b_im, s5_c_re=s5_c_re, s5_c_im=s5_c_im, s5_d=s5_d, s5_glu_w=s5_glu_w, s5_glu_b=s5_glu_b,
             na_rpb=na_rpb, lru_conv_w=lru_conv_w, lru_conv_b=lru_conv_b, lru_gate_w=lru_gate_w,
             lru_gate_b=lru_gate_b, lru_lambda=lru_lambda, w_branch=w_branch, w_out=w_out, ln1_g=ln1_g, ln1_b=ln1_b,
             mlp_w1=mlp_w1, mlp_b1=mlp_b1, mlp_w2=mlp_w2, mlp_b2=mlp_b2, ln2_g=ln2_g, ln2_b=ln2_b)
    layers = [_prepare_layer(l, p) for l in range(DEPTH)]
    return (_trunk(x_prompt, ln_in_g, ln_in_b, layers), _trunk(x_sample, ln_in_g, ln_in_b, layers))
```

```python
import functools
import math

import jax
import jax.numpy as jnp
import numpy as np
from jax import lax
from jax.experimental import pallas as pl
from jax.experimental.pallas import tpu as pltpu

F32 = jnp.float32
BF16 = jnp.bfloat16

D_MODEL = 1024
DEPTH = 2
GRID_W = 64
N_BRANCH = 4
BRANCH_W = 512
DN_HEADS = 4
DN_DK = 128
DN_DV = 128
DN_CONV = 4
DN_CHUNK = 64
S5_WIDTH = 512
S5_P = 16
S5_GROUPS = S5_WIDTH // S5_P
S5_N = 64
NA_HEADS = 8
NA_DH = 64
NA_WIDTH = NA_HEADS * NA_DH
NA_KH = 8
NA_KW = 16
LRU_WIDTH = 512
LRU_BLOCKS = 8
LRU_BW = LRU_WIDTH // LRU_BLOCKS
LRU_CONV = 4
LRU_C = 8.0
D_FF = 4 * D_MODEL
ALPHA = float((2 * DEPTH) ** 0.25)
EPS = 1e-5

_IN_SPLITS = (512, 512, 512, 512, 8, 8, 512, 512, 512, 512, 512, 512, 4096)
_IN_OFFS = tuple(sum(_IN_SPLITS[:i]) for i in range(len(_IN_SPLITS)))

LANE = 128
SUBLANE = 8
C_QKV = 0
C_Z = 1536
C_SU = 2048
C_LX = 2560
C_LG = 3072
C_AB = 3584
W_MAIN = 3712

S5_L = 64
S5_LW = S5_L * S5_P
S5_HW = 4 * LANE

VMEM_LIMIT = 56 * 1024 * 1024


def _cparams(sem):
    return pltpu.CompilerParams(dimension_semantics=sem, vmem_limit_bytes=VMEM_LIMIT)


def _layer_norm(x, g, b):
    mu = jnp.mean(x, axis=-1, keepdims=True)
    xc = x - mu
    var = jnp.mean(xc * xc, axis=-1, keepdims=True)
    return xc * lax.rsqrt(var + EPS) * g + b


def _sigmoid(x):
    return 0.5 * jnp.tanh(0.5 * x) + 0.5


def _const_spec(shape):
    nd = len(shape)
    return pl.BlockSpec(shape, lambda *_: (0,) * nd, pipeline_mode=pl.Buffered(1))


def _project(x, wm_ref, wn_ref, hm_ref, na_ref):
    xb = x.astype(BF16)
    step = 4 * LANE
    for c0 in range(0, W_MAIN, step):
        c1 = min(c0 + step, W_MAIN)
        hm_ref[:, c0:c1] = jnp.dot(xb, wm_ref[:, c0:c1], preferred_element_type=F32)
    for c0 in range(0, 3 * NA_WIDTH, step):
        na_ref[:, c0:c0 + step] = jnp.dot(xb, wn_ref[:, c0:c0 + step], preferred_element_type=F32).astype(BF16)


def _proj_kernel(x_ref, wm_ref, wn_ref, hm_ref, na_ref):
    _project(x_ref[...], wm_ref, wn_ref, hm_ref, na_ref)


def _ln_proj_kernel(x_ref, g_ref, b_ref, wm_ref, wn_ref, xn_ref, hm_ref, na_ref):
    x = _layer_norm(x_ref[...], g_ref[...], b_ref[...])
    xn_ref[...] = x
    _project(x, wm_ref, wn_ref, hm_ref, na_ref)


def _in_proj(x, w_main, w_na, ln=None, tm=256):
    n = x.shape[0]
    tok = lambda w: pl.BlockSpec((tm, w), lambda i: (i, 0))
    w_specs = [_const_spec((D_MODEL, W_MAIN)), _const_spec((D_MODEL, 3 * NA_WIDTH))]
    out_specs = [tok(W_MAIN), tok(3 * NA_WIDTH)]
    out_shape = [jax.ShapeDtypeStruct((n, W_MAIN), F32), jax.ShapeDtypeStruct((n, 3 * NA_WIDTH), BF16)]
    if ln is None:
        return pl.pallas_call(
            _proj_kernel, grid=(n // tm,), in_specs=[tok(D_MODEL)] + w_specs, out_specs=out_specs,
            out_shape=out_shape, compiler_params=_cparams(("parallel",)), name="in_proj",
        )(x, w_main, w_na)
    return pl.pallas_call(
        _ln_proj_kernel, grid=(n // tm,),
        in_specs=[tok(D_MODEL), _const_spec((1, D_MODEL)), _const_spec((1, D_MODEL))] + w_specs,
        out_specs=[tok(D_MODEL)] + out_specs, out_shape=[jax.ShapeDtypeStruct((n, D_MODEL), F32)] + out_shape,
        compiler_params=_cparams(("parallel",)), name="ln_in_proj",
    )(x, ln[0].reshape(1, -1), ln[1].reshape(1, -1), w_main, w_na)


def _conv_centred(prev8, cur, next8, w, first, last):
    tb = cur.shape[0]
    prev8 = jnp.where(first, 0.0, prev8)
    next8 = jnp.where(last, 0.0, next8)
    xp = jnp.concatenate([prev8, cur, next8], axis=0)
    left = (w.shape[0] - 1) // 2
    acc = None
    for j in range(w.shape[0]):
        s = SUBLANE - left + j
        term = xp[s:s + tb] * w[j:j + 1]
        acc = term if acc is None else acc + term
    return acc


def _dot_bf(a, b):
    return jnp.dot(a.astype(BF16), b.astype(BF16), preferred_element_type=F32)


def _dot_nt_bf(a, b):
    return lax.dot_general(a.astype(BF16), b.astype(BF16), (((1,), (1,)), ((), ())), preferred_element_type=F32)


def _dn_pre_kernel(cur_ref, prev_ref, next_ref, cw_ref, o_ref):
    i = pl.program_id(1)
    nb = pl.num_programs(1)
    qkv = _conv_centred(prev_ref[...], cur_ref[...], next_ref[...], cw_ref[...], i == 0, i == nb - 1)
    qkv = qkv * _sigmoid(qkv)
    for h in range(DN_HEADS):
        q = qkv[:, h * DN_DK:(h + 1) * DN_DK]
        k = qkv[:, 512 + h * DN_DK:512 + (h + 1) * DN_DK]
        o_ref[:, h * DN_DK:(h + 1) * DN_DK] = q * (lax.rsqrt(jnp.sum(q * q, axis=-1, keepdims=True) + 1e-6)
                                                    * (DN_DK ** -0.5))
        o_ref[:, 512 + h * DN_DK:512 + (h + 1) * DN_DK] = k * lax.rsqrt(jnp.sum(k * k, axis=-1, keepdims=True) + 1e-6)
    o_ref[:, 1024:] = qkv[:, 1024:]


def _dn_pre(hm3, conv_w, tb=512):
    bsz, t, _ = hm3.shape
    nb = t // tb
    qkv_w = 3 * 512
    return pl.pallas_call(
        _dn_pre_kernel,
        grid=(bsz, nb),
        in_specs=list(_halo_specs(tb, qkv_w, 0, nb, t, False)) + [_const_spec((DN_CONV, qkv_w))],
        out_specs=pl.BlockSpec((None, tb, qkv_w), lambda b, i: (b, i, 0)),
        out_shape=jax.ShapeDtypeStruct((bsz, t, qkv_w), F32),
        compiler_params=_cparams(("parallel", "parallel")),
        name="deltanet_pre",
    )(hm3, hm3, hm3, conv_w)


def _dn_prepare(d, qkv, ab, alog, dtb):
    tb = qkv.shape[0]
    c = DN_CHUNK
    gates = -jnp.exp(alog) * jax.nn.softplus(ab + dtb)
    betas = _sigmoid(ab)

    row = lax.broadcasted_iota(jnp.int32, (c, c), 0)
    col = lax.broadcasted_iota(jnp.int32, (c, c), 1)
    incl = col <= row if d == 0 else col >= row
    strict = col < row if d == 0 else col > row
    eye = (row == col).astype(F32)
    edge = c - 1 if d == 0 else 0
    n_chunks = tb // c

    g_hi = gates.astype(BF16)
    rest = gates - g_hi.astype(F32)
    g_mid = rest.astype(BF16)
    g_lo = (rest - g_mid.astype(F32)).astype(BF16)
    g3 = jnp.concatenate([g_hi, g_mid, g_lo], axis=1)
    tri = incl.astype(BF16)
    gc_parts = [jnp.dot(tri, g3[ci * c:(ci + 1) * c], preferred_element_type=F32) for ci in range(n_chunks)]
    gc_all = jnp.concatenate([p[:, :LANE] + p[:, LANE:2 * LANE] + p[:, 2 * LANE:] for p in gc_parts], axis=0)

    units = [(ci, h) for ci in range(n_chunks) for h in range(DN_HEADS)]
    rows = lambda ci: slice(ci * c, (ci + 1) * c)
    lane_of = lambda h: d * DN_HEADS + h
    q_u = [qkv[rows(ci), h * DN_DK:(h + 1) * DN_DK] for ci, h in units]
    k_u = [qkv[rows(ci), 512 + h * DN_DK:512 + (h + 1) * DN_DK] for ci, h in units]
    v_u = [qkv[rows(ci), 1024 + h * DN_DV:1024 + (h + 1) * DN_DV] for ci, h in units]
    beta_u = [jnp.broadcast_to(betas[rows(ci), 8 + lane_of(h):9 + lane_of(h)], (c, DN_DK)) for ci, h in units]
    gcr_u = [jnp.broadcast_to(gc_all[rows(ci), lane_of(h):lane_of(h) + 1], (c, DN_DK)) for ci, h in units]
    gcl_u = [g.T[:c, :] for g in gcr_u]
    decay_u = [jnp.exp(jnp.where(incl, gr[:, :c] - gl, -1e30)) for gr, gl in zip(gcr_u, gcl_u)]
    egc_u = [jnp.exp(g) for g in gcr_u]
    glast_u = [g[edge:edge + 1, :] for g in gcr_u]
    kb_u = [k * b for k, b in zip(k_u, beta_u)]
    vb_u = [v * b for v, b in zip(v_u, beta_u)]
    kk_u = [_dot_nt_bf(jnp.concatenate([kb, q], axis=0), k) for kb, q, k in zip(kb_u, q_u, k_u)]
    x_u = [-jnp.where(strict, kk[:c] * dec, 0.0) for kk, dec in zip(kk_u, decay_u)]
    qk_u = [jnp.where(incl, kk[c:] * dec, 0.0).astype(BF16) for kk, dec in zip(kk_u, decay_u)]
    t_u = [eye + x for x in x_u]
    x_u = [_dot_bf(x, x) for x in x_u]
    for _ in range(4):
        both = [_dot_bf(jnp.concatenate([t, x], axis=0), x) for t, x in zip(t_u, x_u)]
        t_u = [t + b[:c] for t, b in zip(t_u, both)]
        x_u = [b[c:] for b in both]
    t_u = [t + _dot_bf(t, x) for t, x in zip(t_u, x_u)]
    sol_u = [_dot_bf(t, jnp.concatenate([vb, kb * e], axis=1)) for t, vb, kb, e in zip(t_u, vb_u, kb_u, egc_u)]
    u_u = [s[:, :DN_DV] for s in sol_u]
    wq_u = [jnp.concatenate([s[:, DN_DV:], q * e], axis=0).astype(BF16) for s, q, e in zip(sol_u, q_u, egc_u)]
    qkkd_u = [jnp.concatenate([qk, (k * jnp.exp(gl - g)).T.astype(BF16)], axis=0)
              for qk, k, gl, g in zip(qk_u, k_u, glast_u, gcr_u)]
    egl_u = [jnp.exp(gl) for gl in glast_u]
    return dict(zip(units, zip(u_u, wq_u, qkkd_u, egl_u)))


def _dn_kernel(cur_f, ab_f, cur_b, ab_b, alog_ref, dtb_ref, of_ref, ob_ref, s_ref):
    @pl.when(pl.program_id(1) == 0)
    def _():
        s_ref[...] = jnp.zeros_like(s_ref)

    alog = alog_ref[...]
    dtb = dtb_ref[...]
    prep = (_dn_prepare(0, cur_f[...], ab_f[...], alog, dtb), _dn_prepare(1, cur_b[...], ab_b[...], alog, dtb))
    c = DN_CHUNK
    n_chunks = cur_f.shape[0] // c
    o_refs = (of_ref, ob_ref)
    chains = [(d, h) for d in range(2) for h in range(DN_HEADS)]
    chunk_at = lambda d, step: step if d == 0 else n_chunks - 1 - step
    state = [s_ref[d * DN_HEADS + h] for d, h in chains]
    for step in range(n_chunks):
        ops = [prep[d][(chunk_at(d, step), h)] for d, h in chains]
        ws = [jnp.dot(wq, s.astype(BF16), preferred_element_type=F32) for (_, wq, _, _), s in zip(ops, state)]
        v_new = [(u - w[:c]).astype(BF16) for (u, _, _, _), w in zip(ops, ws)]
        upd = [jnp.dot(qkkd, vn, preferred_element_type=F32) for (_, _, qkkd, _), vn in zip(ops, v_new)]
        state = [s * egl + up[c:] for s, (_, _, _, egl), up in zip(state, ops, upd)]
        for (d, h), w, up in zip(chains, ws, upd):
            r0 = chunk_at(d, step) * c
            o_refs[d][r0:r0 + c, h * DN_DV:(h + 1) * DN_DV] = w[c:] + up[:c]
    for (d, h), s in zip(chains, state):
        s_ref[d * DN_HEADS + h] = s


def _halo_specs(tb, width, col_block, nb, t, reverse):
    per = tb // SUBLANE
    last8 = t // SUBLANE - 1
    if reverse:
        blk = lambda i: nb - 1 - i
    else:
        blk = lambda i: i
    cur = pl.BlockSpec((None, tb, width), lambda b, i: (b, blk(i), col_block))
    prev = pl.BlockSpec((None, SUBLANE, width), lambda b, i: (b, jnp.maximum(blk(i) * per - 1, 0), col_block))
    nxt = pl.BlockSpec((None, SUBLANE, width), lambda b, i: (b, jnp.minimum((blk(i) + 1) * per, last8), col_block))
    return cur, prev, nxt


def _deltanet(hm3, conv_w, a_log, dt_bias, tb=256):
    bsz, t, _ = hm3.shape
    nb = t // tb
    qkv_w = 3 * 512
    qkvn = _dn_pre(hm3, conv_w)
    alog_v = jnp.zeros((1, LANE), F32).at[0, :8].set(a_log.reshape(-1))
    dtb_v = jnp.zeros((1, LANE), F32).at[0, :8].set(dt_bias.reshape(-1))
    ab_col = C_AB // LANE
    fwd = lambda b, i: (b, i, 0)
    bwd = lambda b, i: (b, nb - 1 - i, 0)
    in_specs = [pl.BlockSpec((None, tb, qkv_w), fwd), pl.BlockSpec((None, tb, LANE), lambda b, i: (b, i, ab_col)),
                pl.BlockSpec((None, tb, qkv_w), bwd),
                pl.BlockSpec((None, tb, LANE), lambda b, i: (b, nb - 1 - i, ab_col)),
                _const_spec((1, LANE)), _const_spec((1, LANE))]
    out_specs = [pl.BlockSpec((None, tb, 512), fwd), pl.BlockSpec((None, tb, 512), bwd)]
    return pl.pallas_call(
        _dn_kernel,
        grid=(bsz, nb),
        in_specs=in_specs,
        out_specs=out_specs,
        out_shape=[jax.ShapeDtypeStruct((bsz, t, 512), F32)] * 2,
        scratch_shapes=[pltpu.VMEM((2 * DN_HEADS, DN_DK, DN_DV), F32)],
        compiler_params=_cparams(("parallel", "arbitrary")),
        name="deltanet",
    )(qkvn, hm3, qkvn, hm3, alog_v, dtb_v)


def _s5_discretise(lam_re, lam_im, log_dt, b_re, b_im):
    dt = jnp.exp(log_dt)[:, None]
    mag = jnp.exp(lam_re * dt)
    ab_re = mag * jnp.cos(lam_im * dt)
    ab_im = mag * jnp.sin(lam_im * dt)
    den = jnp.square(lam_re) + jnp.square(lam_im)
    nr, ni = ab_re - 1.0, ab_im
    kr = ((nr * lam_re + ni * lam_im) / den)[..., None]
    ki = ((ni * lam_re - nr * lam_im) / den)[..., None]
    return kr * b_re - ki * b_im, kr * b_im + ki * b_re


def _s5_operators(lam_re, lam_im, log_dt, b_re, b_im, c_re, c_im):
    hi = lax.Precision.HIGHEST
    L, G, N, P = S5_L, S5_GROUPS, S5_N, S5_P
    j = jnp.arange(L + 1, dtype=F32)[:, None, None]
    kcomb = 0.0
    e_cols, f_rows, al = [], [], []
    for d in range(2):
        dt = jnp.exp(log_dt[d])[:, None]
        bb_re, bb_im = _s5_discretise(lam_re[d], lam_im[d], log_dt[d], b_re, b_im)
        mag = jnp.exp(lam_re[d] * dt * j)
        ang = lam_im[d] * dt * j
        aj_re, aj_im = mag * jnp.cos(ang), mag * jnp.sin(ang)
        ca_re = c_re[None] * aj_re[:, :, None, :] - c_im[None] * aj_im[:, :, None, :]
        ca_im = c_re[None] * aj_im[:, :, None, :] + c_im[None] * aj_re[:, :, None, :]
        kj = (jnp.einsum('jgpn,gnq->jgpq', ca_re[:L], bb_re, precision=hi)
              - jnp.einsum('jgpn,gnq->jgpq', ca_im[:L], bb_im, precision=hi))
        zeros = jnp.zeros((L - 1,) + kj.shape[1:], F32)
        if d == 0:
            kcomb = kcomb + jnp.concatenate([zeros, kj], axis=0)
        else:
            kcomb = kcomb + jnp.concatenate([kj[::-1], zeros], axis=0)
        pw_re = aj_re[:L][::-1] if d == 0 else aj_re[:L]
        pw_im = aj_im[:L][::-1] if d == 0 else aj_im[:L]
        e_re = pw_re[..., None] * bb_re[None] - pw_im[..., None] * bb_im[None]
        e_im = pw_re[..., None] * bb_im[None] + pw_im[..., None] * bb_re[None]
        for e in (e_re, e_im):
            e = e.transpose(1, 0, 3, 2).reshape(G, L * P, N)
            e_cols.append(jnp.pad(e, ((0, 0), (0, 0), (0, LANE - N))))
        sel = slice(1, L + 1)
        fr = ca_re[sel] if d == 0 else ca_re[sel][::-1]
        fi = ca_im[sel] if d == 0 else ca_im[sel][::-1]
        for f in (fr, -fi):
            f = f.transpose(1, 3, 0, 2).reshape(G, N, L * P)
            f_rows.append(jnp.pad(f, ((0, 0), (0, LANE - N), (0, 0))))
        al += [jnp.pad(aj_re[L], ((0, 0), (0, LANE - N))), jnp.pad(aj_im[L], ((0, 0), (0, LANE - N)))]
    kc = kcomb.astype(BF16).transpose(1, 3, 0, 2)
    toep = jnp.stack([kc[:, :, L - 1 - s:2 * L - 1 - s, :] for s in range(L)], axis=1)
    toep = toep.reshape(G, L * P, L * P)
    w1 = jnp.concatenate([toep] + [e.astype(BF16) for e in e_cols], axis=2)
    w2 = jnp.concatenate(f_rows, axis=1).astype(BF16)
    return w1, w2, jnp.stack(al, axis=1)


def _s5_kernel(u_ref, w1_ref, w2_ref, al_ref, y_ref, hloc_ref, hin_ref, *, n_chunks, bsz):
    u = u_ref[...]
    y_ref[...] = jnp.dot(u, w1_ref[:, :S5_LW], preferred_element_type=F32)
    hloc_ref[...] = jnp.dot(u, w1_ref[:, S5_LW:], preferred_element_type=F32)
    al = al_ref[...]
    a_re = (al[0:1], al[2:3])
    a_im = (al[1:2], al[3:4])

    def body(cidx, carry):
        new = []
        for d in range(2):
            cr, ci = carry[2 * d], carry[2 * d + 1]
            cc = cidx if d == 0 else n_chunks - 1 - cidx
            rows = pl.ds(pl.multiple_of(cc * bsz, bsz), bsz)
            hin_ref[rows, 2 * d * LANE:(2 * d + 1) * LANE] = cr
            hin_ref[rows, (2 * d + 1) * LANE:(2 * d + 2) * LANE] = ci
            lr = hloc_ref[rows, 2 * d * LANE:(2 * d + 1) * LANE]
            li = hloc_ref[rows, (2 * d + 1) * LANE:(2 * d + 2) * LANE]
            new += [a_re[d] * cr - a_im[d] * ci + lr, a_re[d] * ci + a_im[d] * cr + li]
        return tuple(new)

    zero = jnp.zeros((bsz, LANE), F32)
    lax.fori_loop(0, n_chunks, body, (zero, zero, zero, zero))
    y_ref[...] += jnp.dot(hin_ref[...].astype(BF16), w2_ref[...], preferred_element_type=F32)


def _s5(hm3, w1, w2, al):
    bsz, t, _ = hm3.shape
    n_chunks = t // S5_L
    rows = n_chunks * bsz
    u = hm3[:, :, C_SU:C_SU + S5_WIDTH].reshape(bsz, n_chunks, S5_L, S5_GROUPS, S5_P)
    u = u.transpose(3, 1, 0, 2, 4).reshape(S5_GROUPS, rows, S5_LW).astype(BF16)
    y = pl.pallas_call(
        functools.partial(_s5_kernel, n_chunks=n_chunks, bsz=bsz),
        grid=(S5_GROUPS,),
        in_specs=[pl.BlockSpec((None, rows, S5_LW), lambda g: (g, 0, 0)),
                  pl.BlockSpec((None, S5_LW, S5_LW + S5_HW), lambda g: (g, 0, 0)),
                  pl.BlockSpec((None, S5_HW, S5_LW), lambda g: (g, 0, 0)),
                  pl.BlockSpec((None, 4, LANE), lambda g: (g, 0, 0))],
        out_specs=pl.BlockSpec((None, rows, S5_LW), lambda g: (g, 0, 0)),
        out_shape=jax.ShapeDtypeStruct((S5_GROUPS, rows, S5_LW), F32),
        scratch_shapes=[pltpu.VMEM((rows, S5_HW), F32), pltpu.VMEM((rows, S5_HW), F32)],
        compiler_params=_cparams(("parallel",)),
        name="s5",
    )(u, w1, w2, al)
    y = y.reshape(S5_GROUPS, n_chunks, bsz, S5_L, S5_P).transpose(2, 1, 3, 0, 4)
    return y.reshape(bsz * t, S5_WIDTH)


def _na_bias_table(rpb):
    c = np.arange(GRID_W)
    c0 = np.clip(c - NA_KW // 2, 0, GRID_W - NA_KW)
    col_in = (c[None, :] >= c0[:, None]) & (c[None, :] < c0[:, None] + NA_KW)
    dc = np.clip(c[None, :] - c[:, None], -(NA_KW - 1), NA_KW - 1) + (NA_KW - 1)
    onehot = (dc[:, :, None] == np.arange(2 * NA_KW - 1)).astype(np.float32)
    by_col = jnp.einsum('hrm,qkm->hrqk', rpb.astype(F32), onehot, precision=lax.Precision.HIGHEST)
    tab = jnp.stack([by_col[:, NA_KH - 1 - dl:2 * NA_KH - 1 - dl] for dl in range(NA_KH)], axis=1)
    tab = jnp.where(col_in[None, None, None], tab, -1e30)
    return tab.transpose(0, 1, 3, 2, 4).reshape(NA_HEADS, NA_KH, GRID_W, NA_KH * GRID_W)


def _na_kernel(q_ref, k_ref, v_ref, tab_ref, o_ref, *, rows_per_step, n_rows):
    i = pl.program_id(2)
    w = GRID_W
    nk = NA_KH * w
    lane = lax.broadcasted_iota(jnp.int32, (2 * w, LANE), 1)
    rowi = lax.broadcasted_iota(jnp.int32, (2 * w, LANE), 0)
    own = (lane < NA_DH) == (rowi < w)
    low = lax.broadcasted_iota(jnp.int32, (w, LANE), 1) < NA_DH
    steps = range(rows_per_step)
    r = [i * rows_per_step + rr for rr in steps]
    r0 = [jnp.clip(x - NA_KH // 2, 0, n_rows - NA_KH) for x in r]
    krows = [pl.ds(pl.multiple_of(x * w, w), nk) for x in r0]
    q2 = [q_ref[rr * w:(rr + 1) * w, :] for rr in steps]
    q2 = [jnp.where(own, jnp.concatenate([q, q], axis=0), jnp.zeros((2 * w, LANE), q.dtype)) for q in q2]
    s = [lax.dot_general(q, k_ref[kr, :], (((1,), (1,)), ((), ())), preferred_element_type=F32)
         for q, kr in zip(q2, krows)]
    s = [jnp.concatenate([x[:w] + tab_ref[0, a - b], x[w:] + tab_ref[1, a - b]], axis=0) for x, a, b in zip(s, r, r0)]
    m = [jnp.max(x, axis=-1, keepdims=True) for x in s]
    p = [jnp.exp(x - y) for x, y in zip(s, m)]
    l = [jnp.sum(x, axis=-1, keepdims=True) for x in p]
    pv = [jnp.dot(x.astype(BF16), v_ref[kr, :], preferred_element_type=F32) / y for x, kr, y in zip(p, krows, l)]
    for rr, x in zip(steps, pv):
        o_ref[rr * w:(rr + 1) * w, :] = jnp.where(low, x[:w], x[w:]).astype(o_ref.dtype)


def _natten(na3, table, rows_per_step=8):
    bsz, t, _ = na3.shape
    n_rows = t // GRID_W
    tq = rows_per_step * GRID_W
    pairs = NA_HEADS // 2
    kcol = NA_WIDTH // LANE
    return pl.pallas_call(
        functools.partial(_na_kernel, rows_per_step=rows_per_step, n_rows=n_rows),
        grid=(pairs, bsz, t // tq),
        in_specs=[pl.BlockSpec((None, tq, LANE), lambda p, b, i: (b, i, p)),
                  pl.BlockSpec((None, t, LANE), lambda p, b, i: (b, 0, kcol + p)),
                  pl.BlockSpec((None, t, LANE), lambda p, b, i: (b, 0, 2 * kcol + p)),
                  pl.BlockSpec((2, NA_KH, GRID_W, NA_KH * GRID_W), lambda p, b, i: (p, 0, 0, 0))],
        out_specs=pl.BlockSpec((None, tq, LANE), lambda p, b, i: (b, i, p)),
        out_shape=jax.ShapeDtypeStruct((bsz, t, NA_WIDTH), BF16),
        compiler_params=_cparams(("parallel", "parallel", "parallel")),
        name="natten",
    )(na3, na3, na3, table)


def _lru_coefficients(d, cur, prev8, next8, first, last, cw, cb, wg_ref, gb, sp_lam, a_scr, b_scr):
    xc = _conv_centred(prev8, cur, next8, cw, first, last) + cb
    width = 2 * LRU_WIDTH
    gates = jnp.dot(xc.astype(BF16), wg_ref[:, d * width:(d + 1) * width], preferred_element_type=F32)
    gates = _sigmoid(gates + gb[:, d * width:(d + 1) * width])
    log_a = -LRU_C * gates[:, :LRU_WIDTH] * sp_lam[d:d + 1]
    a = jnp.exp(log_a)
    a_scr[d] = a
    b_scr[d] = jnp.sqrt(1.0 - a * a) * gates[:, LRU_WIDTH:] * xc


def _lru_scan(a_scr, b_scr, carry_ref, o_refs):
    n_groups = a_scr.shape[1] // SUBLANE

    def body(gi, hs):
        new = []
        for d in range(2):
            h = hs[d]
            grp = gi if d == 0 else n_groups - 1 - gi
            rows = pl.ds(pl.multiple_of(grp * SUBLANE, SUBLANE), SUBLANE)
            a8 = a_scr[d, rows, :]
            b8 = b_scr[d, rows, :]
            out = [None] * SUBLANE
            for r in (range(SUBLANE) if d == 0 else range(SUBLANE - 1, -1, -1)):
                h = a8[r:r + 1] * h + b8[r:r + 1]
                out[r] = h
            o_refs[d][rows, :] = jnp.concatenate(out, axis=0)
            new.append(h)
        return tuple(new)

    hf, hb = lax.fori_loop(0, n_groups, body, (carry_ref[0:1], carry_ref[1:2]))
    carry_ref[0:1] = hf
    carry_ref[1:2] = hb


def _lru_kernel(cur_f, prev_f, next_f, cur_b, prev_b, next_b, cw_ref, cb_ref, wg_ref, gb_ref, lam_ref,
                hf_ref, hb_ref, a_scr, b_scr, carry_ref):
    i = pl.program_id(1)
    nb = pl.num_programs(1)

    @pl.when(i == 0)
    def _():
        carry_ref[...] = jnp.zeros_like(carry_ref)

    cw = cw_ref[...]
    cb = cb_ref[...]
    gb = gb_ref[...]
    sp_lam = jax.nn.softplus(-lam_ref[...])
    _lru_coefficients(0, cur_f[...], prev_f[...], next_f[...], i == 0, i == nb - 1, cw, cb, wg_ref, gb, sp_lam,
                      a_scr, b_scr)
    _lru_coefficients(1, cur_b[...], prev_b[...], next_b[...], i == nb - 1, i == 0, cw, cb, wg_ref, gb, sp_lam,
                      a_scr, b_scr)
    _lru_scan(a_scr, b_scr, carry_ref, (hf_ref, hb_ref))


def _lru_gate_matrix(gate_w):
    eye = jnp.eye(LRU_BLOCKS, dtype=gate_w.dtype)
    full = jnp.einsum('dgncm,nk->ncdgkm', gate_w, eye)
    return full.reshape(LRU_WIDTH, 4 * LRU_WIDTH)


def _rglru(hm3, conv_w, conv_b, wg, gate_b, lam, tb=256):
    bsz, t, _ = hm3.shape
    nb = t // tb
    col = C_LX // LRU_WIDTH
    in_specs = list(_halo_specs(tb, LRU_WIDTH, col, nb, t, False)) + list(_halo_specs(tb, LRU_WIDTH, col, nb, t, True))
    in_specs += [_const_spec((LRU_CONV, LRU_WIDTH)), _const_spec((1, LRU_WIDTH)),
                 _const_spec((LRU_WIDTH, 4 * LRU_WIDTH)), _const_spec((1, 4 * LRU_WIDTH)), _const_spec((2, LRU_WIDTH))]
    out_specs = [pl.BlockSpec((None, tb, LRU_WIDTH), lambda b, i: (b, i, 0)),
                 pl.BlockSpec((None, tb, LRU_WIDTH), lambda b, i: (b, nb - 1 - i, 0))]
    return pl.pallas_call(
        _lru_kernel,
        grid=(bsz, nb),
        in_specs=in_specs,
        out_specs=out_specs,
        out_shape=[jax.ShapeDtypeStruct((bsz, t, LRU_WIDTH), F32)] * 2,
        scratch_shapes=[pltpu.VMEM((2, tb, LRU_WIDTH), F32), pltpu.VMEM((2, tb, LRU_WIDTH), F32),
                        pltpu.VMEM((2, LRU_WIDTH), F32)],
        compiler_params=_cparams(("parallel", "arbitrary")),
        name="rglru",
    )(hm3, hm3, hm3, hm3, hm3, hm3, conv_w, conv_b.reshape(1, -1), wg, gate_b.reshape(1, -1), lam)


def _merge_kernel(x_ref, of_ref, ob_ref, z_ref, ys_ref, su_ref, na_ref, hf_ref, hb_ref, lg_ref,
                  ng_ref, sd_ref, gw_ref, gbias_ref, wgt_ref, wbr_ref, wout_ref, lng_ref, lnb_ref, o_ref):
    x = x_ref[...]
    xb = x.astype(BF16)
    o = of_ref[...] + ob_ref[...]
    z = z_ref[...]
    parts = []
    for h in range(DN_HEADS):
        oh = o[:, h * DN_DV:(h + 1) * DN_DV]
        ms = jnp.mean(oh * oh, axis=-1, keepdims=True)
        parts.append(oh * lax.rsqrt(ms + EPS) * ng_ref[...])
    y_a = jnp.concatenate(parts, axis=1) * (z * _sigmoid(z))
    y = jax.nn.gelu(ys_ref[...] + sd_ref[...] * su_ref[...])
    y_b = y * _sigmoid(jnp.dot(y.astype(BF16), gw_ref[...], preferred_element_type=F32) + gbias_ref[...])
    y_d = (hf_ref[...] + hb_ref[...]) * jax.nn.gelu(lg_ref[...])
    ys = (y_a.astype(BF16), y_b.astype(BF16), na_ref[...], y_d.astype(BF16))
    acc = None
    for n in range(N_BRANCH):
        gate = _sigmoid(jnp.dot(xb, wgt_ref[:, n * D_MODEL:(n + 1) * D_MODEL], preferred_element_type=F32))
        term = gate * jnp.dot(ys[n], wbr_ref[n], preferred_element_type=F32)
        acc = term if acc is None else acc + term
    mix = jnp.dot(acc.astype(BF16), wout_ref[...], preferred_element_type=F32)
    o_ref[...] = _layer_norm(ALPHA * x + mix, lng_ref[...], lnb_ref[...])


def _merge(x, o_f, o_b, hm, y_s5, na_o, h_f, h_b, norm_g, s5_d, glu_w, glu_b, w_gate, w_branch, w_out, ln_g, ln_b,
           tm=256):
    n = x.shape[0]
    tok = lambda w, cb=0: pl.BlockSpec((tm, w), lambda i: (i, cb))
    in_specs = [tok(D_MODEL), tok(512), tok(512), tok(512, C_Z // 512), tok(512), tok(512, C_SU // 512), tok(512),
                tok(512), tok(512), tok(512, C_LG // 512),
                _const_spec((1, DN_DV)), _const_spec((1, S5_WIDTH)), _const_spec((S5_WIDTH, S5_WIDTH)),
                _const_spec((1, S5_WIDTH)), _const_spec((D_MODEL, N_BRANCH * D_MODEL)),
                _const_spec((N_BRANCH, BRANCH_W, D_MODEL)), _const_spec((D_MODEL, D_MODEL)),
                _const_spec((1, D_MODEL)), _const_spec((1, D_MODEL))]
    return pl.pallas_call(
        _merge_kernel,
        grid=(n // tm,),
        in_specs=in_specs,
        out_specs=pl.BlockSpec((tm, D_MODEL), lambda i: (i, 0)),
        out_shape=jax.ShapeDtypeStruct((n, D_MODEL), F32),
        compiler_params=_cparams(("parallel",)),
        name="merge",
    )(x, o_f, o_b, hm, y_s5, hm, na_o, h_f, h_b, hm, norm_g.reshape(1, -1), s5_d.reshape(1, -1), glu_w,
      glu_b.reshape(1, -1), w_gate, w_branch, w_out, ln_g.reshape(1, -1), ln_b.reshape(1, -1))


def _mlp_kernel(x_ref, w1_ref, b1_ref, w2_ref, b2_ref, g_ref, b_ref, o_ref):
    x = x_ref[...]
    xb = x.astype(BF16)
    acc = None
    for c0 in range(0, D_FF, D_MODEL):
        f = jnp.dot(xb, w1_ref[:, c0:c0 + D_MODEL], preferred_element_type=F32) + b1_ref[:, c0:c0 + D_MODEL]
        f = jnp.square(jnp.maximum(f, 0.0))
        term = jnp.dot(f.astype(BF16), w2_ref[c0:c0 + D_MODEL, :], preferred_element_type=F32)
        acc = term if acc is None else acc + term
    o_ref[...] = _layer_norm(ALPHA * x + acc + b2_ref[...], g_ref[...], b_ref[...])


def _mlp(x, w1, b1, w2, b2, g, b, tm=512):
    n = x.shape[0]
    return pl.pallas_call(
        _mlp_kernel,
        grid=(n // tm,),
        in_specs=[pl.BlockSpec((tm, D_MODEL), lambda i: (i, 0)), _const_spec((D_MODEL, D_FF)), _const_spec((1, D_FF)),
                  _const_spec((D_FF, D_MODEL)), _const_spec((1, D_MODEL)), _const_spec((1, D_MODEL)),
                  _const_spec((1, D_MODEL))],
        out_specs=pl.BlockSpec((tm, D_MODEL), lambda i: (i, 0)),
        out_shape=jax.ShapeDtypeStruct((n, D_MODEL), F32),
        compiler_params=_cparams(("parallel",)),
        name="mlp",
    )(x, w1, b1.reshape(1, -1), w2, b2.reshape(1, -1), g.reshape(1, -1), b.reshape(1, -1))


def _prepare_layer(l, p):
    w_in = p['w_in'][l]
    col = lambda i: w_in[:, _IN_OFFS[i]:_IN_OFFS[i] + _IN_SPLITS[i]]
    pad = jnp.zeros((D_MODEL, W_MAIN - C_AB - 16), F32)
    w_main = jnp.concatenate([col(0), col(1), col(2), col(3), col(6), col(10), col(11), col(4), col(5), pad], axis=1)
    w_na = jnp.concatenate([col(7) * (NA_DH ** -0.5), col(8), col(9)], axis=1)
    s5_w1, s5_w2, s5_al = _s5_operators(p['s5_lambda_re'][l], p['s5_lambda_im'][l], p['s5_log_dt'][l],
                                        p['s5_b_re'][l], p['s5_b_im'][l], p['s5_c_re'][l], p['s5_c_im'][l])
    return dict(
        w_main=w_main.astype(BF16), w_na=w_na.astype(BF16), w_gate=col(12).astype(BF16),
        dn_conv_w=p['dn_conv_w'][l], dn_a_log=p['dn_a_log'][l], dn_dt_bias=p['dn_dt_bias'][l],
        dn_norm_g=p['dn_norm_g'][l],
        s5_w1=s5_w1, s5_w2=s5_w2, s5_al=s5_al, s5_d=p['s5_d'][l], glu_w=p['s5_glu_w'][l].astype(BF16),
        glu_b=p['s5_glu_b'][l],
        na_table=_na_bias_table(p['na_rpb'][l]),
        lru_conv_w=p['lru_conv_w'][l], lru_conv_b=p['lru_conv_b'][l],
        lru_wg=_lru_gate_matrix(p['lru_gate_w'][l]).astype(BF16), lru_gate_b=p['lru_gate_b'][l],
        lru_lambda=p['lru_lambda'][l],
        w_branch=p['w_branch'][l].astype(BF16), w_out=p['w_out'][l].astype(BF16),
        ln1_g=p['ln1_g'][l], ln1_b=p['ln1_b'][l],
        mlp_w1=p['mlp_w1'][l].astype(BF16), mlp_b1=p['mlp_b1'][l], mlp_w2=p['mlp_w2'][l].astype(BF16),
        mlp_b2=p['mlp_b2'][l], ln2_g=p['ln2_g'][l], ln2_b=p['ln2_b'][l])


def _layer(x, bsz, t, lw, ln_in=None):
    if ln_in is None:
        hm, na = _in_proj(x, lw['w_main'], lw['w_na'])
    else:
        x, hm, na = _in_proj(x, lw['w_main'], lw['w_na'], ln=ln_in)
    hm3 = hm.reshape(bsz, t, W_MAIN)
    o_f, o_b = _deltanet(hm3, lw['dn_conv_w'], lw['dn_a_log'], lw['dn_dt_bias'])
    y_s5 = _s5(hm3, lw['s5_w1'], lw['s5_w2'], lw['s5_al'])
    na_o = _natten(na.reshape(bsz, t, 3 * NA_WIDTH), lw['na_table'])
    h_f, h_b = _rglru(hm3, lw['lru_conv_w'], lw['lru_conv_b'], lw['lru_wg'], lw['lru_gate_b'], lw['lru_lambda'])
    n = bsz * t
    x1 = _merge(x, o_f.reshape(n, 512), o_b.reshape(n, 512), hm, y_s5, na_o.reshape(n, NA_WIDTH),
                h_f.reshape(n, LRU_WIDTH), h_b.reshape(n, LRU_WIDTH), lw['dn_norm_g'], lw['s5_d'], lw['glu_w'],
                lw['glu_b'], lw['w_gate'], lw['w_branch'], lw['w_out'], lw['ln1_g'], lw['ln1_b'])
    return _mlp(x1, lw['mlp_w1'], lw['mlp_b1'], lw['mlp_w2'], lw['mlp_b2'], lw['ln2_g'], lw['ln2_b'])


def _trunk(x, ln_g, ln_b, layers):
    bsz, t, _ = x.shape
    h = x.reshape(bsz * t, D_MODEL)
    for l, lw in enumerate(layers):
        h = _layer(h, bsz, t, lw, ln_in=(ln_g, ln_b) if l == 0 else None)
    return h.reshape(bsz, t, D_MODEL)


def kernel(x_prompt, x_sample, ln_in_g, ln_in_b, w_in, dn_conv_w, dn_a_log, dn_dt_bias, dn_norm_g, s5_lambda_re,
           s5_lambda_im, s5_log_dt, s5_b_re, s5_b_im, s5_c_re, s5_c_im, s5_d, s5_glu_w, s5_glu_b, na_rpb, lru_conv_w,
           lru_conv_b, lru_gate_w, lru_gate_b, lru_lambda, w_branch, w_out, ln1_g, ln1_b, mlp_w1, mlp_b1, mlp_w2,
           mlp_b2, ln2_g, ln2_b):
    p = dict(w_in=w_in, dn_conv_w=dn_conv_w, dn_a_log=dn_a_log, dn_dt_bias=dn_dt_bias, dn_norm_g=dn_norm_g,
             s5_lambda_re=s5_lambda_re, s5_lambda_im=s5_lambda_im, s5_log_dt=s5_log_dt, s5_b_re=s5_b_re,
             s5_b_im=s5_b_im, s5_c_re=s5_c_re, s5_c_im=s5_c_im, s5_d=s5_d, s5_glu_w=s5_glu_w, s5_glu_b=s5_glu_b,
             na_rpb=na_rpb, lru_conv_w=lru_conv_w, lru_conv_b=lru_conv_b, lru_gate_w=lru_gate_w,
             lru_gate_b=lru_gate_b, lru_lambda=lru_lambda, w_branch=w_branch, w_out=w_out, ln1_g=ln1_g, ln1_b=ln1_b,
             mlp_w1=mlp_w1, mlp_b1=mlp_b1, mlp_w2=mlp_w2, mlp_b2=mlp_b2, ln2_g=ln2_g, ln2_b=ln2_b)
    layers = [_prepare_layer(l, p) for l in range(DEPTH)]
    return (_trunk(x_prompt, ln_in_g, ln_in_b, layers), _trunk(x_sample, ln_in_g, ln_in_b, layers))
```

```python
import functools
import math

import jax
import jax.numpy as jnp
import numpy as np
from jax import lax
from jax.experimental import pallas as pl
from jax.experimental.pallas import tpu as pltpu

F32 = jnp.float32
BF16 = jnp.bfloat16

D_MODEL = 1024
DEPTH = 2
GRID_W = 64
N_BRANCH = 4
BRANCH_W = 512
DN_HEADS = 4
DN_DK = 128
DN_DV = 128
DN_CONV = 4
DN_CHUNK = 64
S5_WIDTH = 512
S5_P = 16
S5_GROUPS = S5_WIDTH // S5_P
S5_N = 64
NA_HEADS = 8
NA_DH = 64
NA_WIDTH = NA_HEADS * NA_DH
NA_KH = 8
NA_KW = 16
LRU_WIDTH = 512
LRU_BLOCKS = 8
LRU_BW = LRU_WIDTH // LRU_BLOCKS
LRU_CONV = 4
LRU_C = 8.0
D_FF = 4 * D_MODEL
ALPHA = float((2 * DEPTH) ** 0.25)
EPS = 1e-5

_IN_SPLITS = (512, 512, 512, 512, 8, 8, 512, 512, 512, 512, 512, 512, 4096)
_IN_OFFS = tuple(sum(_IN_SPLITS[:i]) for i in range(len(_IN_SPLITS)))

LANE = 128
SUBLANE = 8
C_QKV = 0
C_Z = 1536
C_SU = 2048
C_LX = 2560
C_LG = 3072
C_AB = 3584
W_MAIN = 3712

S5_L = 64
S5_LW = S5_L * S5_P
S5_HW = 4 * LANE
S5_GPB = LANE // S5_P
S5_ROWS = 256

VMEM_LIMIT = 56 * 1024 * 1024


def _cparams(sem):
    return pltpu.CompilerParams(dimension_semantics=sem, vmem_limit_bytes=VMEM_LIMIT)


def _layer_norm(x, g, b):
    mu = jnp.mean(x, axis=-1, keepdims=True)
    xc = x - mu
    var = jnp.mean(xc * xc, axis=-1, keepdims=True)
    return xc * lax.rsqrt(var + EPS) * g + b


def _sigmoid(x):
    return 0.5 * jnp.tanh(0.5 * x) + 0.5


def _const_spec(shape):
    nd = len(shape)
    return pl.BlockSpec(shape, lambda *_: (0,) * nd, pipeline_mode=pl.Buffered(1))


def _project(x, wm_ref, wn_ref, hm_ref, na_ref):
    xb = x.astype(BF16)
    step = 4 * LANE
    for c0 in range(0, W_MAIN, step):
        c1 = min(c0 + step, W_MAIN)
        hm_ref[:, c0:c1] = jnp.dot(xb, wm_ref[:, c0:c1], preferred_element_type=F32)
    for c0 in range(0, 3 * NA_WIDTH, step):
        na_ref[:, c0:c0 + step] = jnp.dot(xb, wn_ref[:, c0:c0 + step], preferred_element_type=F32).astype(BF16)


def _proj_kernel(x_ref, wm_ref, wn_ref, hm_ref, na_ref):
    _project(x_ref[...], wm_ref, wn_ref, hm_ref, na_ref)


def _ln_proj_kernel(x_ref, g_ref, b_ref, wm_ref, wn_ref, xn_ref, hm_ref, na_ref):
    x = _layer_norm(x_ref[...], g_ref[...], b_ref[...])
    xn_ref[...] = x
    _project(x, wm_ref, wn_ref, hm_ref, na_ref)


def _in_proj(x, w_main, w_na, ln=None, tm=256):
    n = x.shape[0]
    tok = lambda w: pl.BlockSpec((tm, w), lambda i: (i, 0))
    w_specs = [_const_spec((D_MODEL, W_MAIN)), _const_spec((D_MODEL, 3 * NA_WIDTH))]
    out_specs = [tok(W_MAIN), tok(3 * NA_WIDTH)]
    out_shape = [jax.ShapeDtypeStruct((n, W_MAIN), F32), jax.ShapeDtypeStruct((n, 3 * NA_WIDTH), BF16)]
    if ln is None:
        return pl.pallas_call(
            _proj_kernel, grid=(n // tm,), in_specs=[tok(D_MODEL)] + w_specs, out_specs=out_specs,
            out_shape=out_shape, compiler_params=_cparams(("parallel",)), name="in_proj",
        )(x, w_main, w_na)
    return pl.pallas_call(
        _ln_proj_kernel, grid=(n // tm,),
        in_specs=[tok(D_MODEL), _const_spec((1, D_MODEL)), _const_spec((1, D_MODEL))] + w_specs,
        out_specs=[tok(D_MODEL)] + out_specs, out_shape=[jax.ShapeDtypeStruct((n, D_MODEL), F32)] + out_shape,
        compiler_params=_cparams(("parallel",)), name="ln_in_proj",
    )(x, ln[0].reshape(1, -1), ln[1].reshape(1, -1), w_main, w_na)


def _conv_centred(prev8, cur, next8, w, first, last):
    tb = cur.shape[0]
    prev8 = jnp.where(first, 0.0, prev8)
    next8 = jnp.where(last, 0.0, next8)
    xp = jnp.concatenate([prev8, cur, next8], axis=0)
    left = (w.shape[0] - 1) // 2
    acc = None
    for j in range(w.shape[0]):
        s = SUBLANE - left + j
        term = xp[s:s + tb] * w[j:j + 1]
        acc = term if acc is None else acc + term
    return acc


def _dot_bf(a, b):
    return jnp.dot(a.astype(BF16), b.astype(BF16), preferred_element_type=F32)


def _dot_nt_bf(a, b):
    return lax.dot_general(a.astype(BF16), b.astype(BF16), (((1,), (1,)), ((), ())), preferred_element_type=F32)


def _dn_pre_kernel(cur_ref, prev_ref, next_ref, cw_ref, o_ref):
    i = pl.program_id(1)
    nb = pl.num_programs(1)
    qkv = _conv_centred(prev_ref[...], cur_ref[...], next_ref[...], cw_ref[...], i == 0, i == nb - 1)
    qkv = qkv * _sigmoid(qkv)
    for h in range(DN_HEADS):
        q = qkv[:, h * DN_DK:(h + 1) * DN_DK]
        k = qkv[:, 512 + h * DN_DK:512 + (h + 1) * DN_DK]
        o_ref[:, h * DN_DK:(h + 1) * DN_DK] = q * (lax.rsqrt(jnp.sum(q * q, axis=-1, keepdims=True) + 1e-6)
                                                    * (DN_DK ** -0.5))
        o_ref[:, 512 + h * DN_DK:512 + (h + 1) * DN_DK] = k * lax.rsqrt(jnp.sum(k * k, axis=-1, keepdims=True) + 1e-6)
    o_ref[:, 1024:] = qkv[:, 1024:]


def _dn_pre(hm3, conv_w, tb=512):
    bsz, t, _ = hm3.shape
    nb = t // tb
    qkv_w = 3 * 512
    return pl.pallas_call(
        _dn_pre_kernel,
        grid=(bsz, nb),
        in_specs=list(_halo_specs(tb, qkv_w, 0, nb, t, False)) + [_const_spec((DN_CONV, qkv_w))],
        out_specs=pl.BlockSpec((None, tb, qkv_w), lambda b, i: (b, i, 0)),
        out_shape=jax.ShapeDtypeStruct((bsz, t, qkv_w), F32),
        compiler_params=_cparams(("parallel", "parallel")),
        name="deltanet_pre",
    )(hm3, hm3, hm3, conv_w)


def _dn_prepare(d, qkv, ab, alog, dtb):
    tb = qkv.shape[0]
    c = DN_CHUNK
    gates = -jnp.exp(alog) * jax.nn.softplus(ab + dtb)
    betas = _sigmoid(ab)

    row = lax.broadcasted_iota(jnp.int32, (c, c), 0)
    col = lax.broadcasted_iota(jnp.int32, (c, c), 1)
    incl = col <= row if d == 0 else col >= row
    strict = col < row if d == 0 else col > row
    eye = (row == col).astype(F32)
    edge = c - 1 if d == 0 else 0
    n_chunks = tb // c

    g_hi = gates.astype(BF16)
    rest = gates - g_hi.astype(F32)
    g_mid = rest.astype(BF16)
    g_lo = (rest - g_mid.astype(F32)).astype(BF16)
    g3 = jnp.concatenate([g_hi, g_mid, g_lo], axis=1)
    tri = incl.astype(BF16)
    gc_parts = [jnp.dot(tri, g3[ci * c:(ci + 1) * c], preferred_element_type=F32) for ci in range(n_chunks)]
    gc_all = jnp.concatenate([p[:, :LANE] + p[:, LANE:2 * LANE] + p[:, 2 * LANE:] for p in gc_parts], axis=0)

    units = [(ci, h) for ci in range(n_chunks) for h in range(DN_HEADS)]
    rows = lambda ci: slice(ci * c, (ci + 1) * c)
    lane_of = lambda h: d * DN_HEADS + h
    q_u = [qkv[rows(ci), h * DN_DK:(h + 1) * DN_DK] for ci, h in units]
    k_u = [qkv[rows(ci), 512 + h * DN_DK:512 + (h + 1) * DN_DK] for ci, h in units]
    v_u = [qkv[rows(ci), 1024 + h * DN_DV:1024 + (h + 1) * DN_DV] for ci, h in units]
    beta_u = [jnp.broadcast_to(betas[rows(ci), 8 + lane_of(h):9 + lane_of(h)], (c, DN_DK)) for ci, h in units]
    gcr_u = [jnp.broadcast_to(gc_all[rows(ci), lane_of(h):lane_of(h) + 1], (c, DN_DK)) for ci, h in units]
    gcl_u = [g.T[:c, :] for g in gcr_u]
    decay_u = [jnp.exp(jnp.where(incl, gr[:, :c] - gl, -1e30)) for gr, gl in zip(gcr_u, gcl_u)]
    egc_u = [jnp.exp(g) for g in gcr_u]
    glast_u = [g[edge:edge + 1, :] for g in gcr_u]
    kb_u = [k * b for k, b in zip(k_u, beta_u)]
    vb_u = [v * b for v, b in zip(v_u, beta_u)]
    kk_u = [_dot_nt_bf(jnp.concatenate([kb, q], axis=0), k) for kb, q, k in zip(kb_u, q_u, k_u)]
    x_u = [-jnp.where(strict, kk[:c] * dec, 0.0) for kk, dec in zip(kk_u, decay_u)]
    qk_u = [jnp.where(incl, kk[c:] * dec, 0.0).astype(BF16) for kk, dec in zip(kk_u, decay_u)]
    t_u = [eye + x for x in x_u]
    x_u = [_dot_bf(x, x) for x in x_u]
    for _ in range(4):
        both = [_dot_bf(jnp.concatenate([t, x], axis=0), x) for t, x in zip(t_u, x_u)]
        t_u = [t + b[:c] for t, b in zip(t_u, both)]
        x_u = [b[c:] for b in both]
    t_u = [t + _dot_bf(t, x) for t, x in zip(t_u, x_u)]
    sol_u = [_dot_bf(t, jnp.concatenate([vb, kb * e], axis=1)) for t, vb, kb, e in zip(t_u, vb_u, kb_u, egc_u)]
    u_u = [s[:, :DN_DV] for s in sol_u]
    wq_u = [jnp.concatenate([s[:, DN_DV:], q * e], axis=0).astype(BF16) for s, q, e in zip(sol_u, q_u, egc_u)]
    qkkd_u = [jnp.concatenate([qk, (k * jnp.exp(gl - g)).T.astype(BF16)], axis=0)
              for qk, k, gl, g in zip(qk_u, k_u, glast_u, gcr_u)]
    egl_u = [jnp.exp(gl) for gl in glast_u]
    return dict(zip(units, zip(u_u, wq_u, qkkd_u, egl_u)))


def _dn_kernel(cur_f, ab_f, cur_b, ab_b, alog_ref, dtb_ref, of_ref, ob_ref, s_ref):
    @pl.when(pl.program_id(1) == 0)
    def _():
        s_ref[...] = jnp.zeros_like(s_ref)

    alog = alog_ref[...]
    dtb = dtb_ref[...]
    prep = (_dn_prepare(0, cur_f[...], ab_f[...], alog, dtb), _dn_prepare(1, cur_b[...], ab_b[...], alog, dtb))
    c = DN_CHUNK
    n_chunks = cur_f.shape[0] // c
    o_refs = (of_ref, ob_ref)
    chains = [(d, h) for d in range(2) for h in range(DN_HEADS)]
    chunk_at = lambda d, step: step if d == 0 else n_chunks - 1 - step
    state = [s_ref[d * DN_HEADS + h] for d, h in chains]
    for step in range(n_chunks):
        ops = [prep[d][(chunk_at(d, step), h)] for d, h in chains]
        ws = [jnp.dot(wq, s.astype(BF16), preferred_element_type=F32) for (_, wq, _, _), s in zip(ops, state)]
        v_new = [(u - w[:c]).astype(BF16) for (u, _, _, _), w in zip(ops, ws)]
        upd = [jnp.dot(qkkd, vn, preferred_element_type=F32) for (_, _, qkkd, _), vn in zip(ops, v_new)]
        state = [s * egl + up[c:] for s, (_, _, _, egl), up in zip(state, ops, upd)]
        for (d, h), w, up in zip(chains, ws, upd):
            r0 = chunk_at(d, step) * c
            o_refs[d][r0:r0 + c, h * DN_DV:(h + 1) * DN_DV] = w[c:] + up[:c]
    for (d, h), s in zip(chains, state):
        s_ref[d * DN_HEADS + h] = s


def _halo_specs(tb, width, col_block, nb, t, reverse):
    per = tb // SUBLANE
    last8 = t // SUBLANE - 1
    if reverse:
        blk = lambda i: nb - 1 - i
    else:
        blk = lambda i: i
    cur = pl.BlockSpec((None, tb, width), lambda b, i: (b, blk(i), col_block))
    prev = pl.BlockSpec((None, SUBLANE, width), lambda b, i: (b, jnp.maximum(blk(i) * per - 1, 0), col_block))
    nxt = pl.BlockSpec((None, SUBLANE, width), lambda b, i: (b, jnp.minimum((blk(i) + 1) * per, last8), col_block))
    return cur, prev, nxt


def _deltanet(hm3, conv_w, a_log, dt_bias, tb=256):
    bsz, t, _ = hm3.shape
    nb = t // tb
    qkv_w = 3 * 512
    qkvn = _dn_pre(hm3, conv_w)
    alog_v = jnp.zeros((1, LANE), F32).at[0, :8].set(a_log.reshape(-1))
    dtb_v = jnp.zeros((1, LANE), F32).at[0, :8].set(dt_bias.reshape(-1))
    ab_col = C_AB // LANE
    fwd = lambda b, i: (b, i, 0)
    bwd = lambda b, i: (b, nb - 1 - i, 0)
    in_specs = [pl.BlockSpec((None, tb, qkv_w), fwd), pl.BlockSpec((None, tb, LANE), lambda b, i: (b, i, ab_col)),
                pl.BlockSpec((None, tb, qkv_w), bwd),
                pl.BlockSpec((None, tb, LANE), lambda b, i: (b, nb - 1 - i, ab_col)),
                _const_spec((1, LANE)), _const_spec((1, LANE))]
    out_specs = [pl.BlockSpec((None, tb, 512), fwd), pl.BlockSpec((None, tb, 512), bwd)]
    return pl.pallas_call(
        _dn_kernel,
        grid=(bsz, nb),
        in_specs=in_specs,
        out_specs=out_specs,
        out_shape=[jax.ShapeDtypeStruct((bsz, t, 512), F32)] * 2,
        scratch_shapes=[pltpu.VMEM((2 * DN_HEADS, DN_DK, DN_DV), F32)],
        compiler_params=_cparams(("parallel", "arbitrary")),
        name="deltanet",
    )(qkvn, hm3, qkvn, hm3, alog_v, dtb_v)


def _s5_discretise(lam_re, lam_im, log_dt, b_re, b_im):
    dt = jnp.exp(log_dt)[:, None]
    mag = jnp.exp(lam_re * dt)
    ab_re = mag * jnp.cos(lam_im * dt)
    ab_im = mag * jnp.sin(lam_im * dt)
    den = jnp.square(lam_re) + jnp.square(lam_im)
    nr, ni = ab_re - 1.0, ab_im
    kr = ((nr * lam_re + ni * lam_im) / den)[..., None]
    ki = ((ni * lam_re - nr * lam_im) / den)[..., None]
    return kr * b_re - ki * b_im, kr * b_im + ki * b_re


def _s5_operators(lam_re, lam_im, log_dt, b_re, b_im, c_re, c_im):
    hi = lax.Precision.HIGHEST
    L, G, N, P = S5_L, S5_GROUPS, S5_N, S5_P
    j = jnp.arange(L + 1, dtype=F32)[:, None, None]
    kcomb = 0.0
    e_cols, f_rows, al = [], [], []
    for d in range(2):
        dt = jnp.exp(log_dt[d])[:, None]
        bb_re, bb_im = _s5_discretise(lam_re[d], lam_im[d], log_dt[d], b_re, b_im)
        mag = jnp.exp(lam_re[d] * dt * j)
        ang = lam_im[d] * dt * j
        aj_re, aj_im = mag * jnp.cos(ang), mag * jnp.sin(ang)
        ca_re = c_re[None] * aj_re[:, :, None, :] - c_im[None] * aj_im[:, :, None, :]
        ca_im = c_re[None] * aj_im[:, :, None, :] + c_im[None] * aj_re[:, :, None, :]
        kj = (jnp.einsum('jgpn,gnq->jgpq', ca_re[:L], bb_re, precision=hi)
              - jnp.einsum('jgpn,gnq->jgpq', ca_im[:L], bb_im, precision=hi))
        zeros = jnp.zeros((L - 1,) + kj.shape[1:], F32)
        if d == 0:
            kcomb = kcomb + jnp.concatenate([zeros, kj], axis=0)
        else:
            kcomb = kcomb + jnp.concatenate([kj[::-1], zeros], axis=0)
        pw_re = aj_re[:L][::-1] if d == 0 else aj_re[:L]
        pw_im = aj_im[:L][::-1] if d == 0 else aj_im[:L]
        e_re = pw_re[..., None] * bb_re[None] - pw_im[..., None] * bb_im[None]
        e_im = pw_re[..., None] * bb_im[None] + pw_im[..., None] * bb_re[None]
        for e in (e_re, e_im):
            e = e.transpose(1, 0, 3, 2).reshape(G, L * P, N)
            e_cols.append(jnp.pad(e, ((0, 0), (0, 0), (0, LANE - N))))
        sel = slice(1, L + 1)
        fr = ca_re[sel] if d == 0 else ca_re[sel][::-1]
        fi = ca_im[sel] if d == 0 else ca_im[sel][::-1]
        for f in (fr, -fi):
            f = f.transpose(1, 3, 0, 2).reshape(G, N, L * P)
            f_rows.append(jnp.pad(f, ((0, 0), (0, LANE - N), (0, 0))))
        al += [jnp.pad(aj_re[L], ((0, 0), (0, LANE - N))), jnp.pad(aj_im[L], ((0, 0), (0, LANE - N)))]
    kc = kcomb.astype(BF16).transpose(1, 3, 0, 2)
    toep = jnp.stack([kc[:, :, L - 1 - s:2 * L - 1 - s, :] for s in range(L)], axis=1)
    toep = toep.reshape(G, L * P, L * P)
    w1 = jnp.concatenate([toep] + [e.astype(BF16) for e in e_cols], axis=2)
    w2 = jnp.concatenate(f_rows, axis=1).astype(BF16)
    return w1, w2, jnp.stack(al, axis=1)


def _s5_kernel(x_ref, w1_ref, w2_ref, al_ref, y_ref, u_scr, yv_scr, hloc_ref, hin_ref, *, n_chunks, nbs):
    g8 = pl.program_id(2)
    rows = nbs * n_chunks
    blk = lax.broadcasted_iota(jnp.int32, (S5_GPB, LANE), 1) // S5_P
    lane_tiles = S5_L // S5_GPB
    groups = range(S5_GPB)

    def regroup(vregs):
        m = list(vregs)
        dist = S5_GPB // 2
        while dist:
            upper = (blk & dist) != 0
            nxt = list(m)
            for v in groups:
                if not v & dist:
                    w = v + dist
                    nxt[v] = jnp.where(upper, pltpu.roll(m[w], dist * S5_P, axis=1), m[v])
                    nxt[w] = jnp.where(upper, m[w], pltpu.roll(m[v], LANE - dist * S5_P, axis=1))
            m = nxt
            dist //= 2
        return m

    @pl.when(g8 == 0)
    def _():
        def gather_rows(rg, _):
            rsl = pl.ds(pl.multiple_of(rg * SUBLANE, SUBLANE), SUBLANE)
            for k in range(lane_tiles):
                pieces = regroup([x_ref[S5_GPB * k + j, rsl, :] for j in groups])
                for g in groups:
                    u_scr[g, rsl, k * LANE:(k + 1) * LANE] = pieces[g]
            return 0

        lax.fori_loop(0, rows // SUBLANE, gather_rows, 0)

    u = u_scr[g8].astype(BF16)
    y = jnp.dot(u, w1_ref[:, :S5_LW], preferred_element_type=F32)
    hloc = jnp.dot(u, w1_ref[:, S5_LW:], preferred_element_type=F32)
    for part in range(4):
        hloc_ref[part] = hloc[:, part * LANE:(part + 1) * LANE]
    al = al_ref[...]
    a_re = (al[0:1], al[2:3])
    a_im = (al[1:2], al[3:4])

    def body(cidx, carry):
        new = []
        for d in range(2):
            cr, ci = carry[2 * d], carry[2 * d + 1]
            cc = cidx if d == 0 else n_chunks - 1 - cidx
            seqs = pl.ds(cc, nbs, stride=n_chunks)
            hin_ref[2 * d, seqs, :] = cr
            hin_ref[2 * d + 1, seqs, :] = ci
            lr = hloc_ref[2 * d, seqs, :]
            li = hloc_ref[2 * d + 1, seqs, :]
            new += [a_re[d] * cr - a_im[d] * ci + lr, a_re[d] * ci + a_im[d] * cr + li]
        return tuple(new)

    zero = jnp.zeros((nbs, LANE), F32)
    lax.fori_loop(0, n_chunks, body, (zero, zero, zero, zero))
    hin = jnp.concatenate([hin_ref[part] for part in range(4)], axis=1).astype(BF16)
    yv_scr[g8] = y + jnp.dot(hin, w2_ref[...], preferred_element_type=F32)

    @pl.when(g8 == S5_GPB - 1)
    def _():
        def scatter_rows(rg, _):
            rsl = pl.ds(pl.multiple_of(rg * SUBLANE, SUBLANE), SUBLANE)
            for k in range(lane_tiles):
                pieces = regroup([yv_scr[g, rsl, k * LANE:(k + 1) * LANE] for g in groups])
                for j in groups:
                    y_ref[S5_GPB * k + j, rsl, :] = pieces[j]
            return 0

        lax.fori_loop(0, rows // SUBLANE, scatter_rows, 0)


def _s5(hm3, w1, w2, al):
    bsz, t, _ = hm3.shape
    n_chunks = t // S5_L
    rows = n_chunks * bsz
    nbs = S5_ROWS // n_chunks
    xs = hm3[:, :, C_SU:C_SU + S5_WIDTH].reshape(rows, S5_L, S5_WIDTH).transpose(1, 0, 2)
    tok_idx = lambda q, s, g: (0, s, q)
    group = lambda q, s, g: (q * S5_GPB + g, 0, 0)
    ys = pl.pallas_call(
        functools.partial(_s5_kernel, n_chunks=n_chunks, nbs=nbs),
        grid=(S5_WIDTH // LANE, rows // S5_ROWS, S5_GPB),
        in_specs=[pl.BlockSpec((S5_L, S5_ROWS, LANE), tok_idx, pipeline_mode=pl.Buffered(1)),
                  pl.BlockSpec((None, S5_LW, S5_LW + S5_HW), group), pl.BlockSpec((None, S5_HW, S5_LW), group),
                  pl.BlockSpec((None, 4, LANE), group)],
        out_specs=pl.BlockSpec((S5_L, S5_ROWS, LANE), tok_idx),
        out_shape=jax.ShapeDtypeStruct((S5_L, rows, S5_WIDTH), F32),
        scratch_shapes=[pltpu.VMEM((S5_GPB, S5_ROWS, S5_LW), F32), pltpu.VMEM((S5_GPB, S5_ROWS, S5_LW), F32),
                        pltpu.VMEM((4, S5_ROWS, LANE), F32), pltpu.VMEM((4, S5_ROWS, LANE), F32)],
        compiler_params=_cparams(("arbitrary", "arbitrary", "arbitrary")),
        name="s5",
    )(xs, w1, w2, al)
    return ys.transpose(1, 0, 2).reshape(bsz * t, S5_WIDTH)


def _na_bias_table(rpb):
    c = np.arange(GRID_W)
    c0 = np.clip(c - NA_KW // 2, 0, GRID_W - NA_KW)
    col_in = (c[None, :] >= c0[:, None]) & (c[None, :] < c0[:, None] + NA_KW)
    dc = np.clip(c[None, :] - c[:, None], -(NA_KW - 1), NA_KW - 1) + (NA_KW - 1)
    onehot = (dc[:, :, None] == np.arange(2 * NA_KW - 1)).astype(np.float32)
    by_col = jnp.einsum('hrm,qkm->hrqk', rpb.astype(F32), onehot, precision=lax.Precision.HIGHEST)
    tab = jnp.stack([by_col[:, NA_KH - 1 - dl:2 * NA_KH - 1 - dl] for dl in range(NA_KH)], axis=1)
    tab = jnp.where(col_in[None, None, None], tab, -1e30)
    return tab.transpose(0, 1, 3, 2, 4).reshape(NA_HEADS, NA_KH, GRID_W, NA_KH * GRID_W)


def _na_kernel(q_ref, k_ref, v_ref, tab_ref, o_ref, *, rows_per_step, n_rows):
    i = pl.program_id(2)
    w = GRID_W
    nk = NA_KH * w
    lane = lax.broadcasted_iota(jnp.int32, (2 * w, LANE), 1)
    rowi = lax.broadcasted_iota(jnp.int32, (2 * w, LANE), 0)
    own = (lane < NA_DH) == (rowi < w)
    low = lax.broadcasted_iota(jnp.int32, (w, LANE), 1) < NA_DH
    steps = range(rows_per_step)
    r = [i * rows_per_step + rr for rr in steps]
    r0 = [jnp.clip(x - NA_KH // 2, 0, n_rows - NA_KH) for x in r]
    krows = [pl.ds(pl.multiple_of(x * w, w), nk) for x in r0]
    q2 = [q_ref[rr * w:(rr + 1) * w, :] for rr in steps]
    q2 = [jnp.where(own, jnp.concatenate([q, q], axis=0), jnp.zeros((2 * w, LANE), q.dtype)) for q in q2]
    s = [lax.dot_general(q, k_ref[kr, :], (((1,), (1,)), ((), ())), preferred_element_type=F32)
         for q, kr in zip(q2, krows)]
    s = [jnp.concatenate([x[:w] + tab_ref[0, a - b], x[w:] + tab_ref[1, a - b]], axis=0) for x, a, b in zip(s, r, r0)]
    m = [jnp.max(x, axis=-1, keepdims=True) for x in s]
    p = [jnp.exp(x - y) for x, y in zip(s, m)]
    l = [jnp.sum(x, axis=-1, keepdims=True) for x in p]
    pv = [jnp.dot(x.astype(BF16), v_ref[kr, :], preferred_element_type=F32) / y for x, kr, y in zip(p, krows, l)]
    for rr, x in zip(steps, pv):
        o_ref[rr * w:(rr + 1) * w, :] = jnp.where(low, x[:w], x[w:]).astype(o_ref.dtype)


def _natten(na3, table, rows_per_step=8):
    bsz, t, _ = na3.shape
    n_rows = t // GRID_W
    tq = rows_per_step * GRID_W
    pairs = NA_HEADS // 2
    kcol = NA_WIDTH // LANE
    return pl.pallas_call(
        functools.partial(_na_kernel, rows_per_step=rows_per_step, n_rows=n_rows),
        grid=(pairs, bsz, t // tq),
        in_specs=[pl.BlockSpec((None, tq, LANE), lambda p, b, i: (b, i, p)),
                  pl.BlockSpec((None, t, LANE), lambda p, b, i: (b, 0, kcol + p)),
                  pl.BlockSpec((None, t, LANE), lambda p, b, i: (b, 0, 2 * kcol + p)),
                  pl.BlockSpec((2, NA_KH, GRID_W, NA_KH * GRID_W), lambda p, b, i: (p, 0, 0, 0))],
        out_specs=pl.BlockSpec((None, tq, LANE), lambda p, b, i: (b, i, p)),
        out_shape=jax.ShapeDtypeStruct((bsz, t, NA_WIDTH), BF16),
        compiler_params=_cparams(("parallel", "parallel", "parallel")),
        name="natten",
    )(na3, na3, na3, table)


def _lru_coefficients(d, cur, prev8, next8, first, last, cw, cb, wg_ref, gb, sp_lam, a_scr, b_scr):
    xc = _conv_centred(prev8, cur, next8, cw, first, last) + cb
    width = 2 * LRU_WIDTH
    gates = jnp.dot(xc.astype(BF16), wg_ref[:, d * width:(d + 1) * width], preferred_element_type=F32)
    gates = _sigmoid(gates + gb[:, d * width:(d + 1) * width])
    log_a = -LRU_C * gates[:, :LRU_WIDTH] * sp_lam[d:d + 1]
    a = jnp.exp(log_a)
    a_scr[d] = a
    b_scr[d] = jnp.sqrt(1.0 - a * a) * gates[:, LRU_WIDTH:] * xc


def _lru_scan(a_scr, b_scr, carry_ref, o_refs):
    n_groups = a_scr.shape[1] // SUBLANE

    def body(gi, hs):
        new = []
        for d in range(2):
            h = hs[d]
            grp = gi if d == 0 else n_groups - 1 - gi
            rows = pl.ds(pl.multiple_of(grp * SUBLANE, SUBLANE), SUBLANE)
            a8 = a_scr[d, rows, :]
            b8 = b_scr[d, rows, :]
            out = [None] * SUBLANE
            for r in (range(SUBLANE) if d == 0 else range(SUBLANE - 1, -1, -1)):
                h = a8[r:r + 1] * h + b8[r:r + 1]
                out[r] = h
            o_refs[d][rows, :] = jnp.concatenate(out, axis=0)
            new.append(h)
        return tuple(new)

    hf, hb = lax.fori_loop(0, n_groups, body, (carry_ref[0:1], carry_ref[1:2]))
    carry_ref[0:1] = hf
    carry_ref[1:2] = hb


def _lru_kernel(cur_f, prev_f, next_f, cur_b, prev_b, next_b, cw_ref, cb_ref, wg_ref, gb_ref, lam_ref,
                hf_ref, hb_ref, a_scr, b_scr, carry_ref):
    i = pl.program_id(1)
    nb = pl.num_programs(1)

    @pl.when(i == 0)
    def _():
        carry_ref[...] = jnp.zeros_like(carry_ref)

    cw = cw_ref[...]
    cb = cb_ref[...]
    gb = gb_ref[...]
    sp_lam = jax.nn.softplus(-lam_ref[...])
    _lru_coefficients(0, cur_f[...], prev_f[...], next_f[...], i == 0, i == nb - 1, cw, cb, wg_ref, gb, sp_lam,
                      a_scr, b_scr)
    _lru_coefficients(1, cur_b[...], prev_b[...], next_b[...], i == nb - 1, i == 0, cw, cb, wg_ref, gb, sp_lam,
                      a_scr, b_scr)
    _lru_scan(a_scr, b_scr, carry_ref, (hf_ref, hb_ref))


def _lru_gate_matrix(gate_w):
    eye = jnp.eye(LRU_BLOCKS, dtype=gate_w.dtype)
    full = jnp.einsum('dgncm,nk->ncdgkm', gate_w, eye)
    return full.reshape(LRU_WIDTH, 4 * LRU_WIDTH)


def _rglru(hm3, conv_w, conv_b, wg, gate_b, lam, tb=256):
    bsz, t, _ = hm3.shape
    nb = t // tb
    col = C_LX // LRU_WIDTH
    in_specs = list(_halo_specs(tb, LRU_WIDTH, col, nb, t, False)) + list(_halo_specs(tb, LRU_WIDTH, col, nb, t, True))
    in_specs += [_const_spec((LRU_CONV, LRU_WIDTH)), _const_spec((1, LRU_WIDTH)),
                 _const_spec((LRU_WIDTH, 4 * LRU_WIDTH)), _const_spec((1, 4 * LRU_WIDTH)), _const_spec((2, LRU_WIDTH))]
    out_specs = [pl.BlockSpec((None, tb, LRU_WIDTH), lambda b, i: (b, i, 0)),
                 pl.BlockSpec((None, tb, LRU_WIDTH), lambda b, i: (b, nb - 1 - i, 0))]
    return pl.pallas_call(
        _lru_kernel,
        grid=(bsz, nb),
        in_specs=in_specs,
        out_specs=out_specs,
        out_shape=[jax.ShapeDtypeStruct((bsz, t, LRU_WIDTH), F32)] * 2,
        scratch_shapes=[pltpu.VMEM((2, tb, LRU_WIDTH), F32), pltpu.VMEM((2, tb, LRU_WIDTH), F32),
                        pltpu.VMEM((2, LRU_WIDTH), F32)],
        compiler_params=_cparams(("parallel", "arbitrary")),
        name="rglru",
    )(hm3, hm3, hm3, hm3, hm3, hm3, conv_w, conv_b.reshape(1, -1), wg, gate_b.reshape(1, -1), lam)


def _merge_kernel(x_ref, of_ref, ob_ref, z_ref, ys_ref, su_ref, na_ref, hf_ref, hb_ref, lg_ref,
                  ng_ref, sd_ref, gw_ref, gbias_ref, wgt_ref, wbr_ref, wout_ref, lng_ref, lnb_ref, o_ref):
    x = x_ref[...]
    xb = x.astype(BF16)
    o = of_ref[...] + ob_ref[...]
    z = z_ref[...]
    parts = []
    for h in range(DN_HEADS):
        oh = o[:, h * DN_DV:(h + 1) * DN_DV]
        ms = jnp.mean(oh * oh, axis=-1, keepdims=True)
        parts.append(oh * lax.rsqrt(ms + EPS) * ng_ref[...])
    y_a = jnp.concatenate(parts, axis=1) * (z * _sigmoid(z))
    y = jax.nn.gelu(ys_ref[...] + sd_ref[...] * su_ref[...])
    y_b = y * _sigmoid(jnp.dot(y.astype(BF16), gw_ref[...], preferred_element_type=F32) + gbias_ref[...])
    y_d = (hf_ref[...] + hb_ref[...]) * jax.nn.gelu(lg_ref[...])
    ys = (y_a.astype(BF16), y_b.astype(BF16), na_ref[...], y_d.astype(BF16))
    acc = None
    for n in range(N_BRANCH):
        gate = _sigmoid(jnp.dot(xb, wgt_ref[:, n * D_MODEL:(n + 1) * D_MODEL], preferred_element_type=F32))
        term = gate * jnp.dot(ys[n], wbr_ref[n], preferred_element_type=F32)
        acc = term if acc is None else acc + term
    mix = jnp.dot(acc.astype(BF16), wout_ref[...], preferred_element_type=F32)
    o_ref[...] = _layer_norm(ALPHA * x + mix, lng_ref[...], lnb_ref[...])


def _merge(x, o_f, o_b, hm, y_s5, na_o, h_f, h_b, norm_g, s5_d, glu_w, glu_b, w_gate, w_branch, w_out, ln_g, ln_b,
           tm=256):
    n = x.shape[0]
    tok = lambda w, cb=0: pl.BlockSpec((tm, w), lambda i: (i, cb))
    in_specs = [tok(D_MODEL), tok(512), tok(512), tok(512, C_Z // 512), tok(512), tok(512, C_SU // 512), tok(512),
                tok(512), tok(512), tok(512, C_LG // 512),
                _const_spec((1, DN_DV)), _const_spec((1, S5_WIDTH)), _const_spec((S5_WIDTH, S5_WIDTH)),
                _const_spec((1, S5_WIDTH)), _const_spec((D_MODEL, N_BRANCH * D_MODEL)),
                _const_spec((N_BRANCH, BRANCH_W, D_MODEL)), _const_spec((D_MODEL, D_MODEL)),
                _const_spec((1, D_MODEL)), _const_spec((1, D_MODEL))]
    return pl.pallas_call(
        _merge_kernel,
        grid=(n // tm,),
        in_specs=in_specs,
        out_specs=pl.BlockSpec((tm, D_MODEL), lambda i: (i, 0)),
        out_shape=jax.ShapeDtypeStruct((n, D_MODEL), F32),
        compiler_params=_cparams(("parallel",)),
        name="merge",
    )(x, o_f, o_b, hm, y_s5, hm, na_o, h_f, h_b, hm, norm_g.reshape(1, -1), s5_d.reshape(1, -1), glu_w,
      glu_b.reshape(1, -1), w_gate, w_branch, w_out, ln_g.reshape(1, -1), ln_b.reshape(1, -1))


def _mlp_kernel(x_ref, w1_ref, b1_ref, w2_ref, b2_ref, g_ref, b_ref, o_ref):
    x = x_ref[...]
    xb = x.astype(BF16)
    acc = None
    for c0 in range(0, D_FF, D_MODEL):
        f = jnp.dot(xb, w1_ref[:, c0:c0 + D_MODEL], preferred_element_type=F32) + b1_ref[:, c0:c0 + D_MODEL]
        f = jnp.square(jnp.maximum(f, 0.0))
        term = jnp.dot(f.astype(BF16), w2_ref[c0:c0 + D_MODEL, :], preferred_element_type=F32)
        acc = term if acc is None else acc + term
    o_ref[...] = _layer_norm(ALPHA * x + acc + b2_ref[...], g_ref[...], b_ref[...])


def _mlp(x, w1, b1, w2, b2, g, b, tm=512):
    n = x.shape[0]
    return pl.pallas_call(
        _mlp_kernel,
        grid=(n // tm,),
        in_specs=[pl.BlockSpec((tm, D_MODEL), lambda i: (i, 0)), _const_spec((D_MODEL, D_FF)), _const_spec((1, D_FF)),
                  _const_spec((D_FF, D_MODEL)), _const_spec((1, D_MODEL)), _const_spec((1, D_MODEL)),
                  _const_spec((1, D_MODEL))],
        out_specs=pl.BlockSpec((tm, D_MODEL), lambda i: (i, 0)),
        out_shape=jax.ShapeDtypeStruct((n, D_MODEL), F32),
        compiler_params=_cparams(("parallel",)),
        name="mlp",
    )(x, w1, b1.reshape(1, -1), w2, b2.reshape(1, -1), g.reshape(1, -1), b.reshape(1, -1))


def _prepare_layer(l, p):
    w_in = p['w_in'][l]
    col = lambda i: w_in[:, _IN_OFFS[i]:_IN_OFFS[i] + _IN_SPLITS[i]]
    pad = jnp.zeros((D_MODEL, W_MAIN - C_AB - 16), F32)
    w_main = jnp.concatenate([col(0), col(1), col(2), col(3), col(6), col(10), col(11), col(4), col(5), pad], axis=1)
    w_na = jnp.concatenate([col(7) * (NA_DH ** -0.5), col(8), col(9)], axis=1)
    s5_w1, s5_w2, s5_al = _s5_operators(p['s5_lambda_re'][l], p['s5_lambda_im'][l], p['s5_log_dt'][l],
                                        p['s5_b_re'][l], p['s5_b_im'][l], p['s5_c_re'][l], p['s5_c_im'][l])
    return dict(
        w_main=w_main.astype(BF16), w_na=w_na.astype(BF16), w_gate=col(12).astype(BF16),
        dn_conv_w=p['dn_conv_w'][l], dn_a_log=p['dn_a_log'][l], dn_dt_bias=p['dn_dt_bias'][l],
        dn_norm_g=p['dn_norm_g'][l],
        s5_w1=s5_w1, s5_w2=s5_w2, s5_al=s5_al, s5_d=p['s5_d'][l], glu_w=p['s5_glu_w'][l].astype(BF16),
        glu_b=p['s5_glu_b'][l],
        na_table=_na_bias_table(p['na_rpb'][l]),
        lru_conv_w=p['lru_conv_w'][l], lru_conv_b=p['lru_conv_b'][l],
        lru_wg=_lru_gate_matrix(p['lru_gate_w'][l]).astype(BF16), lru_gate_b=p['lru_gate_b'][l],
        lru_lambda=p['lru_lambda'][l],
        w_branch=p['w_branch'][l].astype(BF16), w_out=p['w_out'][l].astype(BF16),
        ln1_g=p['ln1_g'][l], ln1_b=p['ln1_b'][l],
        mlp_w1=p['mlp_w1'][l].astype(BF16), mlp_b1=p['mlp_b1'][l], mlp_w2=p['mlp_w2'][l].astype(BF16),
        mlp_b2=p['mlp_b2'][l], ln2_g=p['ln2_g'][l], ln2_b=p['ln2_b'][l])


def _layer(x, bsz, t, lw, ln_in=None):
    if ln_in is None:
        hm, na = _in_proj(x, lw['w_main'], lw['w_na'])
    else:
        x, hm, na = _in_proj(x, lw['w_main'], lw['w_na'], ln=ln_in)
    hm3 = hm.reshape(bsz, t, W_MAIN)
    o_f, o_b = _deltanet(hm3, lw['dn_conv_w'], lw['dn_a_log'], lw['dn_dt_bias'])
    y_s5 = _s5(hm3, lw['s5_w1'], lw['s5_w2'], lw['s5_al'])
    na_o = _natten(na.reshape(bsz, t, 3 * NA_WIDTH), lw['na_table'])
    h_f, h_b = _rglru(hm3, lw['lru_conv_w'], lw['lru_conv_b'], lw['lru_wg'], lw['lru_gate_b'], lw['lru_lambda'])
    n = bsz * t
    x1 = _merge(x, o_f.reshape(n, 512), o_b.reshape(n, 512), hm, y_s5, na_o.reshape(n, NA_WIDTH),
                h_f.reshape(n, LRU_WIDTH), h_b.reshape(n, LRU_WIDTH), lw['dn_norm_g'], lw['s5_d'], lw['glu_w'],
                lw['glu_b'], lw['w_gate'], lw['w_branch'], lw['w_out'], lw['ln1_g'], lw['ln1_b'])
    return _mlp(x1, lw['mlp_w1'], lw['mlp_b1'], lw['mlp_w2'], lw['mlp_b2'], lw['ln2_g'], lw['ln2_b'])


def _trunk(x, ln_g, ln_b, layers):
    bsz, t, _ = x.shape
    h = x.reshape(bsz * t, D_MODEL)
    for l, lw in enumerate(layers):
        h = _layer(h, bsz, t, lw, ln_in=(ln_g, ln_b) if l == 0 else None)
    return h.reshape(bsz, t, D_MODEL)


def kernel(x_prompt, x_sample, ln_in_g, ln_in_b, w_in, dn_conv_w, dn_a_log, dn_dt_bias, dn_norm_g, s5_lambda_re,
           s5_lambda_im, s5_log_dt, s5_b_re, s5_b_im, s5_c_re, s5_c_im, s5_d, s5_glu_w, s5_glu_b, na_rpb, lru_conv_w,
           lru_conv_b, lru_gate_w, lru_gate_b, lru_lambda, w_branch, w_out, ln1_g, ln1_b, mlp_w1, mlp_b1, mlp_w2,
           mlp_b2, ln2_g, ln2_b):
    p = dict(w_in=w_in, dn_conv_w=dn_conv_w, dn_a_log=dn_a_log, dn_dt_bias=dn_dt_bias, dn_norm_g=dn_norm_g,
             s5_lambda_re=s5_lambda_re, s5_lambda_im=s5_lambda_im, s5_log_dt=s5_log_dt, s5_b_re=s5_b_re,
             s5_b_im=s5_b_im, s5_c_re=s5_c_re, s5_c_im=s5_c_im, s5_d=s5_d, s5_glu_w=s5_glu_w, s5_glu_b=s5_glu_b,
             na_rpb=na_rpb, lru_conv_w=lru_conv_w, lru_conv_b=lru_conv_b, lru_gate_w=lru_gate_w,
             lru_gate_b=lru_gate_b, lru_lambda=lru_lambda, w_branch=w_branch, w_out=w_out, ln1_g=ln1_g, ln1_b=ln1_b,
             mlp_w1=mlp_w1, mlp_b1=mlp_b1, mlp_w2=mlp_w2, mlp_b2=mlp_b2, ln2_g=ln2_g, ln2_b=ln2_b)
    layers = [_prepare_layer(l, p) for l in range(DEPTH)]
    return (_trunk(x_prompt, ln_in_g, ln_in_b, layers), _trunk(x_sample, ln_in_g, ln_in_b, layers))
```

```python
import functools
import math

import jax
import jax.numpy as jnp
import numpy as np
from jax import lax
from jax.experimental import pallas as pl
from jax.experimental.pallas import tpu as pltpu

F32 = jnp.float32
BF16 = jnp.bfloat16

D_MODEL = 1024
DEPTH = 2
GRID_W = 64
N_BRANCH = 4
BRANCH_W = 512
DN_HEADS = 4
DN_DK = 128
DN_DV = 128
DN_CONV = 4
DN_CHUNK = 64
S5_WIDTH = 512
S5_P = 16
S5_GROUPS = S5_WIDTH // S5_P
S5_N = 64
NA_HEADS = 8
NA_DH = 64
NA_WIDTH = NA_HEADS * NA_DH
NA_KH = 8
NA_KW = 16
LRU_WIDTH = 512
LRU_BLOCKS = 8
LRU_BW = LRU_WIDTH // LRU_BLOCKS
LRU_CONV = 4
LRU_C = 8.0
D_FF = 4 * D_MODEL
ALPHA = float((2 * DEPTH) ** 0.25)
EPS = 1e-5

_IN_SPLITS = (512, 512, 512, 512, 8, 8, 512, 512, 512, 512, 512, 512, 4096)
_IN_OFFS = tuple(sum(_IN_SPLITS[:i]) for i in range(len(_IN_SPLITS)))

LANE = 128
SUBLANE = 8
C_QKV = 0
C_Z = 1536
C_SU = 2048
C_LX = 2560
C_LG = 3072
C_AB = 3584
W_MAIN = 3712

S5_L = 32
S5_LW = S5_L * S5_P
S5_HW = 4 * LANE
S5_GPB = LANE // S5_P
S5_ROWS = 512

VMEM_LIMIT = 56 * 1024 * 1024


def _cparams(sem):
    return pltpu.CompilerParams(dimension_semantics=sem, vmem_limit_bytes=VMEM_LIMIT)


def _layer_norm(x, g, b):
    mu = jnp.mean(x, axis=-1, keepdims=True)
    xc = x - mu
    var = jnp.mean(xc * xc, axis=-1, keepdims=True)
    return xc * lax.rsqrt(var + EPS) * g + b


def _sigmoid(x):
    return 0.5 * jnp.tanh(0.5 * x) + 0.5


def _const_spec(shape):
    nd = len(shape)
    return pl.BlockSpec(shape, lambda *_: (0,) * nd, pipeline_mode=pl.Buffered(1))


def _project(x, wm_ref, wn_ref, hm_ref, na_ref):
    xb = x.astype(BF16)
    step = 4 * LANE
    for c0 in range(0, W_MAIN, step):
        c1 = min(c0 + step, W_MAIN)
        hm_ref[:, c0:c1] = jnp.dot(xb, wm_ref[:, c0:c1], preferred_element_type=F32)
    for c0 in range(0, 3 * NA_WIDTH, step):
        na_ref[:, c0:c0 + step] = jnp.dot(xb, wn_ref[:, c0:c0 + step], preferred_element_type=F32).astype(BF16)


def _proj_kernel(x_ref, wm_ref, wn_ref, hm_ref, na_ref):
    _project(x_ref[...], wm_ref, wn_ref, hm_ref, na_ref)


def _ln_proj_kernel(x_ref, g_ref, b_ref, wm_ref, wn_ref, xn_ref, hm_ref, na_ref):
    x = _layer_norm(x_ref[...], g_ref[...], b_ref[...])
    xn_ref[...] = x
    _project(x, wm_ref, wn_ref, hm_ref, na_ref)


def _in_proj(x, w_main, w_na, ln=None, tm=256):
    n = x.shape[0]
    tok = lambda w: pl.BlockSpec((tm, w), lambda i: (i, 0))
    w_specs = [_const_spec((D_MODEL, W_MAIN)), _const_spec((D_MODEL, 3 * NA_WIDTH))]
    out_specs = [tok(W_MAIN), tok(3 * NA_WIDTH)]
    out_shape = [jax.ShapeDtypeStruct((n, W_MAIN), F32), jax.ShapeDtypeStruct((n, 3 * NA_WIDTH), BF16)]
    if ln is None:
        return pl.pallas_call(
            _proj_kernel, grid=(n // tm,), in_specs=[tok(D_MODEL)] + w_specs, out_specs=out_specs,
            out_shape=out_shape, compiler_params=_cparams(("parallel",)), name="in_proj",
        )(x, w_main, w_na)
    return pl.pallas_call(
        _ln_proj_kernel, grid=(n // tm,),
        in_specs=[tok(D_MODEL), _const_spec((1, D_MODEL)), _const_spec((1, D_MODEL))] + w_specs,
        out_specs=[tok(D_MODEL)] + out_specs, out_shape=[jax.ShapeDtypeStruct((n, D_MODEL), F32)] + out_shape,
        compiler_params=_cparams(("parallel",)), name="ln_in_proj",
    )(x, ln[0].reshape(1, -1), ln[1].reshape(1, -1), w_main, w_na)


def _conv_centred(prev8, cur, next8, w, first, last):
    tb = cur.shape[0]
    prev8 = jnp.where(first, 0.0, prev8)
    next8 = jnp.where(last, 0.0, next8)
    xp = jnp.concatenate([prev8, cur, next8], axis=0)
    left = (w.shape[0] - 1) // 2
    acc = None
    for j in range(w.shape[0]):
        s = SUBLANE - left + j
        term = xp[s:s + tb] * w[j:j + 1]
        acc = term if acc is None else acc + term
    return acc


def _dot_bf(a, b):
    return jnp.dot(a.astype(BF16), b.astype(BF16), preferred_element_type=F32)


def _dot_nt_bf(a, b):
    return lax.dot_general(a.astype(BF16), b.astype(BF16), (((1,), (1,)), ((), ())), preferred_element_type=F32)


def _dn_pre_kernel(cur_ref, prev_ref, next_ref, cw_ref, o_ref):
    i = pl.program_id(1)
    nb = pl.num_programs(1)
    qkv = _conv_centred(prev_ref[...], cur_ref[...], next_ref[...], cw_ref[...], i == 0, i == nb - 1)
    qkv = qkv * _sigmoid(qkv)
    for h in range(DN_HEADS):
        q = qkv[:, h * DN_DK:(h + 1) * DN_DK]
        k = qkv[:, 512 + h * DN_DK:512 + (h + 1) * DN_DK]
        o_ref[:, h * DN_DK:(h + 1) * DN_DK] = q * (lax.rsqrt(jnp.sum(q * q, axis=-1, keepdims=True) + 1e-6)
                                                    * (DN_DK ** -0.5))
        o_ref[:, 512 + h * DN_DK:512 + (h + 1) * DN_DK] = k * lax.rsqrt(jnp.sum(k * k, axis=-1, keepdims=True) + 1e-6)
    o_ref[:, 1024:] = qkv[:, 1024:]


def _dn_pre(hm3, conv_w, tb=512):
    bsz, t, _ = hm3.shape
    nb = t // tb
    qkv_w = 3 * 512
    return pl.pallas_call(
        _dn_pre_kernel,
        grid=(bsz, nb),
        in_specs=list(_halo_specs(tb, qkv_w, 0, nb, t, False)) + [_const_spec((DN_CONV, qkv_w))],
        out_specs=pl.BlockSpec((None, tb, qkv_w), lambda b, i: (b, i, 0)),
        out_shape=jax.ShapeDtypeStruct((bsz, t, qkv_w), F32),
        compiler_params=_cparams(("parallel", "parallel")),
        name="deltanet_pre",
    )(hm3, hm3, hm3, conv_w)


def _dn_prepare(d, qkv, ab, alog, dtb):
    tb = qkv.shape[0]
    c = DN_CHUNK
    gates = -jnp.exp(alog) * jax.nn.softplus(ab + dtb)
    betas = _sigmoid(ab)

    row = lax.broadcasted_iota(jnp.int32, (c, c), 0)
    col = lax.broadcasted_iota(jnp.int32, (c, c), 1)
    incl = col <= row if d == 0 else col >= row
    strict = col < row if d == 0 else col > row
    eye = (row == col).astype(F32)
    edge = c - 1 if d == 0 else 0
    n_chunks = tb // c

    g_hi = gates.astype(BF16)
    rest = gates - g_hi.astype(F32)
    g_mid = rest.astype(BF16)
    g_lo = (rest - g_mid.astype(F32)).astype(BF16)
    g3 = jnp.concatenate([g_hi, g_mid, g_lo], axis=1)
    tri = incl.astype(BF16)
    gc_parts = [jnp.dot(tri, g3[ci * c:(ci + 1) * c], preferred_element_type=F32) for ci in range(n_chunks)]
    gc_all = jnp.concatenate([p[:, :LANE] + p[:, LANE:2 * LANE] + p[:, 2 * LANE:] for p in gc_parts], axis=0)

    units = [(ci, h) for ci in range(n_chunks) for h in range(DN_HEADS)]
    rows = lambda ci: slice(ci * c, (ci + 1) * c)
    lane_of = lambda h: d * DN_HEADS + h
    q_u = [qkv[rows(ci), h * DN_DK:(h + 1) * DN_DK] for ci, h in units]
    k_u = [qkv[rows(ci), 512 + h * DN_DK:512 + (h + 1) * DN_DK] for ci, h in units]
    v_u = [qkv[rows(ci), 1024 + h * DN_DV:1024 + (h + 1) * DN_DV] for ci, h in units]
    beta_u = [jnp.broadcast_to(betas[rows(ci), 8 + lane_of(h):9 + lane_of(h)], (c, DN_DK)) for ci, h in units]
    gcr_u = [jnp.broadcast_to(gc_all[rows(ci), lane_of(h):lane_of(h) + 1], (c, DN_DK)) for ci, h in units]
    gcl_u = [g.T[:c, :] for g in gcr_u]
    decay_u = [jnp.exp(jnp.where(incl, gr[:, :c] - gl, -1e30)) for gr, gl in zip(gcr_u, gcl_u)]
    egc_u = [jnp.exp(g) for g in gcr_u]
    glast_u = [g[edge:edge + 1, :] for g in gcr_u]
    kb_u = [k * b for k, b in zip(k_u, beta_u)]
    vb_u = [v * b for v, b in zip(v_u, beta_u)]
    kk_u = [_dot_nt_bf(jnp.concatenate([kb, q], axis=0), k) for kb, q, k in zip(kb_u, q_u, k_u)]
    x_u = [-jnp.where(strict, kk[:c] * dec, 0.0) for kk, dec in zip(kk_u, decay_u)]
    qk_u = [jnp.where(incl, kk[c:] * dec, 0.0).astype(BF16) for kk, dec in zip(kk_u, decay_u)]
    t_u = [eye + x for x in x_u]
    x_u = [_dot_bf(x, x) for x in x_u]
    for _ in range(4):
        both = [_dot_bf(jnp.concatenate([t, x], axis=0), x) for t, x in zip(t_u, x_u)]
        t_u = [t + b[:c] for t, b in zip(t_u, both)]
        x_u = [b[c:] for b in both]
    t_u = [t + _dot_bf(t, x) for t, x in zip(t_u, x_u)]
    sol_u = [_dot_bf(t, jnp.concatenate([vb, kb * e], axis=1)) for t, vb, kb, e in zip(t_u, vb_u, kb_u, egc_u)]
    u_u = [s[:, :DN_DV] for s in sol_u]
    wq_u = [jnp.concatenate([s[:, DN_DV:], q * e], axis=0).astype(BF16) for s, q, e in zip(sol_u, q_u, egc_u)]
    qkkd_u = [jnp.concatenate([qk, (k * jnp.exp(gl - g)).T.astype(BF16)], axis=0)
              for qk, k, gl, g in zip(qk_u, k_u, glast_u, gcr_u)]
    egl_u = [jnp.exp(gl) for gl in glast_u]
    return dict(zip(units, zip(u_u, wq_u, qkkd_u, egl_u)))


def _dn_kernel(cur_f, ab_f, cur_b, ab_b, alog_ref, dtb_ref, of_ref, ob_ref, s_ref):
    @pl.when(pl.program_id(1) == 0)
    def _():
        s_ref[...] = jnp.zeros_like(s_ref)

    alog = alog_ref[...]
    dtb = dtb_ref[...]
    prep = (_dn_prepare(0, cur_f[...], ab_f[...], alog, dtb), _dn_prepare(1, cur_b[...], ab_b[...], alog, dtb))
    c = DN_CHUNK
    n_chunks = cur_f.shape[0] // c
    o_refs = (of_ref, ob_ref)
    chains = [(d, h) for d in range(2) for h in range(DN_HEADS)]
    chunk_at = lambda d, step: step if d == 0 else n_chunks - 1 - step
    state = [s_ref[d * DN_HEADS + h] for d, h in chains]
    for step in range(n_chunks):
        ops = [prep[d][(chunk_at(d, step), h)] for d, h in chains]
        ws = [jnp.dot(wq, s.astype(BF16), preferred_element_type=F32) for (_, wq, _, _), s in zip(ops, state)]
        v_new = [(u - w[:c]).astype(BF16) for (u, _, _, _), w in zip(ops, ws)]
        upd = [jnp.dot(qkkd, vn, preferred_element_type=F32) for (_, _, qkkd, _), vn in zip(ops, v_new)]
        state = [s * egl + up[c:] for s, (_, _, _, egl), up in zip(state, ops, upd)]
        for (d, h), w, up in zip(chains, ws, upd):
            r0 = chunk_at(d, step) * c
            o_refs[d][r0:r0 + c, h * DN_DV:(h + 1) * DN_DV] = w[c:] + up[:c]
    for (d, h), s in zip(chains, state):
        s_ref[d * DN_HEADS + h] = s


def _halo_specs(tb, width, col_block, nb, t, reverse):
    per = tb // SUBLANE
    last8 = t // SUBLANE - 1
    if reverse:
        blk = lambda i: nb - 1 - i
    else:
        blk = lambda i: i
    cur = pl.BlockSpec((None, tb, width), lambda b, i: (b, blk(i), col_block))
    prev = pl.BlockSpec((None, SUBLANE, width), lambda b, i: (b, jnp.maximum(blk(i) * per - 1, 0), col_block))
    nxt = pl.BlockSpec((None, SUBLANE, width), lambda b, i: (b, jnp.minimum((blk(i) + 1) * per, last8), col_block))
    return cur, prev, nxt


def _deltanet(hm3, conv_w, a_log, dt_bias, tb=256):
    bsz, t, _ = hm3.shape
    nb = t // tb
    qkv_w = 3 * 512
    qkvn = _dn_pre(hm3, conv_w)
    alog_v = jnp.zeros((1, LANE), F32).at[0, :8].set(a_log.reshape(-1))
    dtb_v = jnp.zeros((1, LANE), F32).at[0, :8].set(dt_bias.reshape(-1))
    ab_col = C_AB // LANE
    fwd = lambda b, i: (b, i, 0)
    bwd = lambda b, i: (b, nb - 1 - i, 0)
    in_specs = [pl.BlockSpec((None, tb, qkv_w), fwd), pl.BlockSpec((None, tb, LANE), lambda b, i: (b, i, ab_col)),
                pl.BlockSpec((None, tb, qkv_w), bwd),
                pl.BlockSpec((None, tb, LANE), lambda b, i: (b, nb - 1 - i, ab_col)),
                _const_spec((1, LANE)), _const_spec((1, LANE))]
    out_specs = [pl.BlockSpec((None, tb, 512), fwd), pl.BlockSpec((None, tb, 512), bwd)]
    return pl.pallas_call(
        _dn_kernel,
        grid=(bsz, nb),
        in_specs=in_specs,
        out_specs=out_specs,
        out_shape=[jax.ShapeDtypeStruct((bsz, t, 512), F32)] * 2,
        scratch_shapes=[pltpu.VMEM((2 * DN_HEADS, DN_DK, DN_DV), F32)],
        compiler_params=_cparams(("parallel", "arbitrary")),
        name="deltanet",
    )(qkvn, hm3, qkvn, hm3, alog_v, dtb_v)


def _s5_discretise(lam_re, lam_im, log_dt, b_re, b_im):
    dt = jnp.exp(log_dt)[:, None]
    mag = jnp.exp(lam_re * dt)
    ab_re = mag * jnp.cos(lam_im * dt)
    ab_im = mag * jnp.sin(lam_im * dt)
    den = jnp.square(lam_re) + jnp.square(lam_im)
    nr, ni = ab_re - 1.0, ab_im
    kr = ((nr * lam_re + ni * lam_im) / den)[..., None]
    ki = ((ni * lam_re - nr * lam_im) / den)[..., None]
    return kr * b_re - ki * b_im, kr * b_im + ki * b_re


def _s5_operators(lam_re, lam_im, log_dt, b_re, b_im, c_re, c_im):
    hi = lax.Precision.HIGHEST
    L, G, N, P = S5_L, S5_GROUPS, S5_N, S5_P
    j = jnp.arange(L + 1, dtype=F32)[:, None, None]
    kcomb = 0.0
    e_cols, f_rows, al = [], [], []
    for d in range(2):
        dt = jnp.exp(log_dt[d])[:, None]
        bb_re, bb_im = _s5_discretise(lam_re[d], lam_im[d], log_dt[d], b_re, b_im)
        mag = jnp.exp(lam_re[d] * dt * j)
        ang = lam_im[d] * dt * j
        aj_re, aj_im = mag * jnp.cos(ang), mag * jnp.sin(ang)
        ca_re = c_re[None] * aj_re[:, :, None, :] - c_im[None] * aj_im[:, :, None, :]
        ca_im = c_re[None] * aj_im[:, :, None, :] + c_im[None] * aj_re[:, :, None, :]
        kj = (jnp.einsum('jgpn,gnq->jgpq', ca_re[:L], bb_re, precision=hi)
              - jnp.einsum('jgpn,gnq->jgpq', ca_im[:L], bb_im, precision=hi))
        zeros = jnp.zeros((L - 1,) + kj.shape[1:], F32)
        if d == 0:
            kcomb = kcomb + jnp.concatenate([zeros, kj], axis=0)
        else:
            kcomb = kcomb + jnp.concatenate([kj[::-1], zeros], axis=0)
        pw_re = aj_re[:L][::-1] if d == 0 else aj_re[:L]
        pw_im = aj_im[:L][::-1] if d == 0 else aj_im[:L]
        e_re = pw_re[..., None] * bb_re[None] - pw_im[..., None] * bb_im[None]
        e_im = pw_re[..., None] * bb_im[None] + pw_im[..., None] * bb_re[None]
        for e in (e_re, e_im):
            e = e.transpose(1, 0, 3, 2).reshape(G, L * P, N)
            e_cols.append(jnp.pad(e, ((0, 0), (0, 0), (0, LANE - N))))
        sel = slice(1, L + 1)
        fr = ca_re[sel] if d == 0 else ca_re[sel][::-1]
        fi = ca_im[sel] if d == 0 else ca_im[sel][::-1]
        for f in (fr, -fi):
            f = f.transpose(1, 3, 0, 2).reshape(G, N, L * P)
            f_rows.append(jnp.pad(f, ((0, 0), (0, LANE - N), (0, 0))))
        al += [jnp.pad(aj_re[L], ((0, 0), (0, LANE - N))), jnp.pad(aj_im[L], ((0, 0), (0, LANE - N)))]
    kc = kcomb.astype(BF16).transpose(1, 3, 0, 2)
    toep = jnp.stack([kc[:, :, L - 1 - s:2 * L - 1 - s, :] for s in range(L)], axis=1)
    toep = toep.reshape(G, L * P, L * P)
    we = jnp.concatenate(e_cols, axis=2).astype(BF16)
    w2 = jnp.concatenate(f_rows, axis=1).astype(BF16)
    return toep, we, w2, jnp.stack(al, axis=1)


def _s5_kernel(x_ref, wt_ref, we_ref, w2_ref, al_ref, y_ref, u_scr, yv_scr, hloc_ref, hin_ref, *, n_chunks, nbs):
    g8 = pl.program_id(2)
    rows = nbs * n_chunks
    blk = lax.broadcasted_iota(jnp.int32, (S5_GPB, LANE), 1) // S5_P
    lane_tiles = S5_L // S5_GPB
    groups = range(S5_GPB)

    def regroup(vregs):
        m = list(vregs)
        dist = S5_GPB // 2
        while dist:
            upper = (blk & dist) != 0
            nxt = list(m)
            for v in groups:
                if not v & dist:
                    w = v + dist
                    nxt[v] = jnp.where(upper, pltpu.roll(m[w], dist * S5_P, axis=1), m[v])
                    nxt[w] = jnp.where(upper, m[w], pltpu.roll(m[v], LANE - dist * S5_P, axis=1))
            m = nxt
            dist //= 2
        return m

    @pl.when(g8 == 0)
    def _():
        def gather_rows(rg, _):
            rsl = pl.ds(pl.multiple_of(rg * SUBLANE, SUBLANE), SUBLANE)
            for k in range(lane_tiles):
                pieces = regroup([x_ref[S5_GPB * k + j, rsl, :] for j in groups])
                for g in groups:
                    u_scr[g, rsl, k * LANE:(k + 1) * LANE] = pieces[g]
            return 0

        lax.fori_loop(0, rows // SUBLANE, gather_rows, 0, unroll=2)

    u = u_scr[g8].astype(BF16)
    y = jnp.dot(u, wt_ref[...], preferred_element_type=F32)
    hloc = jnp.dot(u, we_ref[...], preferred_element_type=F32)
    for part in range(4):
        hloc_ref[part] = hloc[:, part * LANE:(part + 1) * LANE]
    al = al_ref[...]
    a_re = (al[0:1], al[2:3])
    a_im = (al[1:2], al[3:4])

    def body(cidx, carry):
        new = []
        for d in range(2):
            cr, ci = carry[2 * d], carry[2 * d + 1]
            cc = cidx if d == 0 else n_chunks - 1 - cidx
            seqs = pl.ds(cc, nbs, stride=n_chunks)
            hin_ref[2 * d, seqs, :] = cr
            hin_ref[2 * d + 1, seqs, :] = ci
            lr = hloc_ref[2 * d, seqs, :]
            li = hloc_ref[2 * d + 1, seqs, :]
            new += [a_re[d] * cr - a_im[d] * ci + lr, a_re[d] * ci + a_im[d] * cr + li]
        return tuple(new)

    zero = jnp.zeros((nbs, LANE), F32)
    lax.fori_loop(0, n_chunks, body, (zero, zero, zero, zero))
    hin = jnp.concatenate([hin_ref[part] for part in range(4)], axis=1).astype(BF16)
    yv_scr[g8] = y + jnp.dot(hin, w2_ref[...], preferred_element_type=F32)

    @pl.when(g8 == S5_GPB - 1)
    def _():
        def scatter_rows(rg, _):
            rsl = pl.ds(pl.multiple_of(rg * SUBLANE, SUBLANE), SUBLANE)
            for k in range(lane_tiles):
                pieces = regroup([yv_scr[g, rsl, k * LANE:(k + 1) * LANE] for g in groups])
                for j in groups:
                    y_ref[S5_GPB * k + j, rsl, :] = pieces[j]
            return 0

        lax.fori_loop(0, rows // SUBLANE, scatter_rows, 0, unroll=2)


def _s5(hm3, wt, we, w2, al):
    bsz, t, _ = hm3.shape
    n_chunks = t // S5_L
    rows = n_chunks * bsz
    nbs = S5_ROWS // n_chunks
    xs = hm3[:, :, C_SU:C_SU + S5_WIDTH].reshape(rows, S5_L, S5_WIDTH).transpose(1, 0, 2)
    tok_idx = lambda q, s, g: (0, s, q)
    group = lambda q, s, g: (q * S5_GPB + g, 0, 0)
    ys = pl.pallas_call(
        functools.partial(_s5_kernel, n_chunks=n_chunks, nbs=nbs),
        grid=(S5_WIDTH // LANE, rows // S5_ROWS, S5_GPB),
        in_specs=[pl.BlockSpec((S5_L, S5_ROWS, LANE), tok_idx, pipeline_mode=pl.Buffered(1)),
                  pl.BlockSpec((None, S5_LW, S5_LW), group), pl.BlockSpec((None, S5_LW, S5_HW), group),
                  pl.BlockSpec((None, S5_HW, S5_LW), group),
                  pl.BlockSpec((None, 4, LANE), group)],
        out_specs=pl.BlockSpec((S5_L, S5_ROWS, LANE), tok_idx),
        out_shape=jax.ShapeDtypeStruct((S5_L, rows, S5_WIDTH), F32),
        scratch_shapes=[pltpu.VMEM((S5_GPB, S5_ROWS, S5_LW), F32), pltpu.VMEM((S5_GPB, S5_ROWS, S5_LW), F32),
                        pltpu.VMEM((4, S5_ROWS, LANE), F32), pltpu.VMEM((4, S5_ROWS, LANE), F32)],
        compiler_params=_cparams(("arbitrary", "arbitrary", "arbitrary")),
        name="s5",
    )(xs, wt, we, w2, al)
    return ys.transpose(1, 0, 2).reshape(bsz * t, S5_WIDTH)


def _na_bias_table(rpb):
    c = np.arange(GRID_W)
    c0 = np.clip(c - NA_KW // 2, 0, GRID_W - NA_KW)
    col_in = (c[None, :] >= c0[:, None]) & (c[None, :] < c0[:, None] + NA_KW)
    dc = np.clip(c[None, :] - c[:, None], -(NA_KW - 1), NA_KW - 1) + (NA_KW - 1)
    onehot = (dc[:, :, None] == np.arange(2 * NA_KW - 1)).astype(np.float32)
    by_col = jnp.einsum('hrm,qkm->hrqk', rpb.astype(F32), onehot, precision=lax.Precision.HIGHEST)
    tab = jnp.stack([by_col[:, NA_KH - 1 - dl:2 * NA_KH - 1 - dl] for dl in range(NA_KH)], axis=1)
    tab = jnp.where(col_in[None, None, None], tab, -1e30)
    return tab.transpose(0, 1, 3, 2, 4).reshape(NA_HEADS, NA_KH, GRID_W, NA_KH * GRID_W)


def _na_kernel(q_ref, k_ref, v_ref, tab_ref, o_ref, *, rows_per_step, n_rows):
    i = pl.program_id(2)
    w = GRID_W
    nk = NA_KH * w
    lane = lax.broadcasted_iota(jnp.int32, (2 * w, LANE), 1)
    rowi = lax.broadcasted_iota(jnp.int32, (2 * w, LANE), 0)
    own = (lane < NA_DH) == (rowi < w)
    low = lax.broadcasted_iota(jnp.int32, (w, LANE), 1) < NA_DH
    steps = range(rows_per_step)
    r = [i * rows_per_step + rr for rr in steps]
    r0 = [jnp.clip(x - NA_KH // 2, 0, n_rows - NA_KH) for x in r]
    krows = [pl.ds(pl.multiple_of(x * w, w), nk) for x in r0]
    q2 = [q_ref[rr * w:(rr + 1) * w, :] for rr in steps]
    q2 = [jnp.where(own, jnp.concatenate([q, q], axis=0), jnp.zeros((2 * w, LANE), q.dtype)) for q in q2]
    s = [lax.dot_general(q, k_ref[kr, :], (((1,), (1,)), ((), ())), preferred_element_type=F32)
         for q, kr in zip(q2, krows)]
    s = [jnp.concatenate([x[:w] + tab_ref[0, a - b], x[w:] + tab_ref[1, a - b]], axis=0) for x, a, b in zip(s, r, r0)]
    m = [jnp.max(x, axis=-1, keepdims=True) for x in s]
    p = [jnp.exp(x - y) for x, y in zip(s, m)]
    l = [jnp.sum(x, axis=-1, keepdims=True) for x in p]
    pv = [jnp.dot(x.astype(BF16), v_ref[kr, :], preferred_element_type=F32) / y for x, kr, y in zip(p, krows, l)]
    for rr, x in zip(steps, pv):
        o_ref[rr * w:(rr + 1) * w, :] = jnp.where(low, x[:w], x[w:]).astype(o_ref.dtype)


def _natten(na3, table, rows_per_step=8):
    bsz, t, _ = na3.shape
    n_rows = t // GRID_W
    tq = rows_per_step * GRID_W
    pairs = NA_HEADS // 2
    kcol = NA_WIDTH // LANE
    return pl.pallas_call(
        functools.partial(_na_kernel, rows_per_step=rows_per_step, n_rows=n_rows),
        grid=(pairs, bsz, t // tq),
        in_specs=[pl.BlockSpec((None, tq, LANE), lambda p, b, i: (b, i, p)),
                  pl.BlockSpec((None, t, LANE), lambda p, b, i: (b, 0, kcol + p)),
                  pl.BlockSpec((None, t, LANE), lambda p, b, i: (b, 0, 2 * kcol + p)),
                  pl.BlockSpec((2, NA_KH, GRID_W, NA_KH * GRID_W), lambda p, b, i: (p, 0, 0, 0))],
        out_specs=pl.BlockSpec((None, tq, LANE), lambda p, b, i: (b, i, p)),
        out_shape=jax.ShapeDtypeStruct((bsz, t, NA_WIDTH), BF16),
        compiler_params=_cparams(("parallel", "parallel", "parallel")),
        name="natten",
    )(na3, na3, na3, table)


def _lru_coefficients(d, cur, prev8, next8, first, last, cw, cb, wg_ref, gb, sp_lam, a_scr, b_scr):
    xc = _conv_centred(prev8, cur, next8, cw, first, last) + cb
    width = 2 * LRU_WIDTH
    gates = jnp.dot(xc.astype(BF16), wg_ref[:, d * width:(d + 1) * width], preferred_element_type=F32)
    gates = _sigmoid(gates + gb[:, d * width:(d + 1) * width])
    log_a = -LRU_C * gates[:, :LRU_WIDTH] * sp_lam[d:d + 1]
    a = jnp.exp(log_a)
    a_scr[d] = a
    b_scr[d] = jnp.sqrt(1.0 - a * a) * gates[:, LRU_WIDTH:] * xc


def _lru_scan(a_scr, b_scr, carry_ref, o_refs):
    n_groups = a_scr.shape[1] // SUBLANE

    def body(gi, hs):
        new = []
        for d in range(2):
            h = hs[d]
            grp = gi if d == 0 else n_groups - 1 - gi
            rows = pl.ds(pl.multiple_of(grp * SUBLANE, SUBLANE), SUBLANE)
            a8 = a_scr[d, rows, :]
            b8 = b_scr[d, rows, :]
            out = [None] * SUBLANE
            for r in (range(SUBLANE) if d == 0 else range(SUBLANE - 1, -1, -1)):
                h = a8[r:r + 1] * h + b8[r:r + 1]
                out[r] = h
            o_refs[d][rows, :] = jnp.concatenate(out, axis=0)
            new.append(h)
        return tuple(new)

    hf, hb = lax.fori_loop(0, n_groups, body, (carry_ref[0:1], carry_ref[1:2]))
    carry_ref[0:1] = hf
    carry_ref[1:2] = hb


def _lru_kernel(cur_f, prev_f, next_f, cur_b, prev_b, next_b, cw_ref, cb_ref, wg_ref, gb_ref, lam_ref,
                hf_ref, hb_ref, a_scr, b_scr, carry_ref):
    i = pl.program_id(1)
    nb = pl.num_programs(1)

    @pl.when(i == 0)
    def _():
        carry_ref[...] = jnp.zeros_like(carry_ref)

    cw = cw_ref[...]
    cb = cb_ref[...]
    gb = gb_ref[...]
    sp_lam = jax.nn.softplus(-lam_ref[...])
    _lru_coefficients(0, cur_f[...], prev_f[...], next_f[...], i == 0, i == nb - 1, cw, cb, wg_ref, gb, sp_lam,
                      a_scr, b_scr)
    _lru_coefficients(1, cur_b[...], prev_b[...], next_b[...], i == nb - 1, i == 0, cw, cb, wg_ref, gb, sp_lam,
                      a_scr, b_scr)
    _lru_scan(a_scr, b_scr, carry_ref, (hf_ref, hb_ref))


def _lru_gate_matrix(gate_w):
    eye = jnp.eye(LRU_BLOCKS, dtype=gate_w.dtype)
    full = jnp.einsum('dgncm,nk->ncdgkm', gate_w, eye)
    return full.reshape(LRU_WIDTH, 4 * LRU_WIDTH)


def _rglru(hm3, conv_w, conv_b, wg, gate_b, lam, tb=256):
    bsz, t, _ = hm3.shape
    nb = t // tb
    col = C_LX // LRU_WIDTH
    in_specs = list(_halo_specs(tb, LRU_WIDTH, col, nb, t, False)) + list(_halo_specs(tb, LRU_WIDTH, col, nb, t, True))
    in_specs += [_const_spec((LRU_CONV, LRU_WIDTH)), _const_spec((1, LRU_WIDTH)),
                 _const_spec((LRU_WIDTH, 4 * LRU_WIDTH)), _const_spec((1, 4 * LRU_WIDTH)), _const_spec((2, LRU_WIDTH))]
    out_specs = [pl.BlockSpec((None, tb, LRU_WIDTH), lambda b, i: (b, i, 0)),
                 pl.BlockSpec((None, tb, LRU_WIDTH), lambda b, i: (b, nb - 1 - i, 0))]
    return pl.pallas_call(
        _lru_kernel,
        grid=(bsz, nb),
        in_specs=in_specs,
        out_specs=out_specs,
        out_shape=[jax.ShapeDtypeStruct((bsz, t, LRU_WIDTH), F32)] * 2,
        scratch_shapes=[pltpu.VMEM((2, tb, LRU_WIDTH), F32), pltpu.VMEM((2, tb, LRU_WIDTH), F32),
                        pltpu.VMEM((2, LRU_WIDTH), F32)],
        compiler_params=_cparams(("parallel", "arbitrary")),
        name="rglru",
    )(hm3, hm3, hm3, hm3, hm3, hm3, conv_w, conv_b.reshape(1, -1), wg, gate_b.reshape(1, -1), lam)


def _merge_kernel(x_ref, of_ref, ob_ref, z_ref, ys_ref, su_ref, na_ref, hf_ref, hb_ref, lg_ref,
                  ng_ref, sd_ref, gw_ref, gbias_ref, wgt_ref, wbr_ref, wout_ref, lng_ref, lnb_ref, o_ref):
    x = x_ref[...]
    xb = x.astype(BF16)
    o = of_ref[...] + ob_ref[...]
    z = z_ref[...]
    parts = []
    for h in range(DN_HEADS):
        oh = o[:, h * DN_DV:(h + 1) * DN_DV]
        ms = jnp.mean(oh * oh, axis=-1, keepdims=True)
        parts.append(oh * lax.rsqrt(ms + EPS) * ng_ref[...])
    y_a = jnp.concatenate(parts, axis=1) * (z * _sigmoid(z))
    y = jax.nn.gelu(ys_ref[...] + sd_ref[...] * su_ref[...])
    y_b = y * _sigmoid(jnp.dot(y.astype(BF16), gw_ref[...], preferred_element_type=F32) + gbias_ref[...])
    y_d = (hf_ref[...] + hb_ref[...]) * jax.nn.gelu(lg_ref[...])
    ys = (y_a.astype(BF16), y_b.astype(BF16), na_ref[...], y_d.astype(BF16))
    acc = None
    for n in range(N_BRANCH):
        gate = _sigmoid(jnp.dot(xb, wgt_ref[:, n * D_MODEL:(n + 1) * D_MODEL], preferred_element_type=F32))
        term = gate * jnp.dot(ys[n], wbr_ref[n], preferred_element_type=F32)
        acc = term if acc is None else acc + term
    mix = jnp.dot(acc.astype(BF16), wout_ref[...], preferred_element_type=F32)
    o_ref[...] = _layer_norm(ALPHA * x + mix, lng_ref[...], lnb_ref[...])


def _merge(x, o_f, o_b, hm, y_s5, na_o, h_f, h_b, norm_g, s5_d, glu_w, glu_b, w_gate, w_branch, w_out, ln_g, ln_b,
           tm=256):
    n = x.shape[0]
    tok = lambda w, cb=0: pl.BlockSpec((tm, w), lambda i: (i, cb))
    in_specs = [tok(D_MODEL), tok(512), tok(512), tok(512, C_Z // 512), tok(512), tok(512, C_SU // 512), tok(512),
                tok(512), tok(512), tok(512, C_LG // 512),
                _const_spec((1, DN_DV)), _const_spec((1, S5_WIDTH)), _const_spec((S5_WIDTH, S5_WIDTH)),
                _const_spec((1, S5_WIDTH)), _const_spec((D_MODEL, N_BRANCH * D_MODEL)),
                _const_spec((N_BRANCH, BRANCH_W, D_MODEL)), _const_spec((D_MODEL, D_MODEL)),
                _const_spec((1, D_MODEL)), _const_spec((1, D_MODEL))]
    return pl.pallas_call(
        _merge_kernel,
        grid=(n // tm,),
        in_specs=in_specs,
        out_specs=pl.BlockSpec((tm, D_MODEL), lambda i: (i, 0)),
        out_shape=jax.ShapeDtypeStruct((n, D_MODEL), F32),
        compiler_params=_cparams(("parallel",)),
        name="merge",
    )(x, o_f, o_b, hm, y_s5, hm, na_o, h_f, h_b, hm, norm_g.reshape(1, -1), s5_d.reshape(1, -1), glu_w,
      glu_b.reshape(1, -1), w_gate, w_branch, w_out, ln_g.reshape(1, -1), ln_b.reshape(1, -1))


def _mlp_kernel(x_ref, w1_ref, b1_ref, w2_ref, b2_ref, g_ref, b_ref, o_ref):
    x = x_ref[...]
    xb = x.astype(BF16)
    acc = None
    for c0 in range(0, D_FF, D_MODEL):
        f = jnp.dot(xb, w1_ref[:, c0:c0 + D_MODEL], preferred_element_type=F32) + b1_ref[:, c0:c0 + D_MODEL]
        f = jnp.square(jnp.maximum(f, 0.0))
        term = jnp.dot(f.astype(BF16), w2_ref[c0:c0 + D_MODEL, :], preferred_element_type=F32)
        acc = term if acc is None else acc + term
    o_ref[...] = _layer_norm(ALPHA * x + acc + b2_ref[...], g_ref[...], b_ref[...])


def _mlp(x, w1, b1, w2, b2, g, b, tm=512):
    n = x.shape[0]
    return pl.pallas_call(
        _mlp_kernel,
        grid=(n // tm,),
        in_specs=[pl.BlockSpec((tm, D_MODEL), lambda i: (i, 0)), _const_spec((D_MODEL, D_FF)), _const_spec((1, D_FF)),
                  _const_spec((D_FF, D_MODEL)), _const_spec((1, D_MODEL)), _const_spec((1, D_MODEL)),
                  _const_spec((1, D_MODEL))],
        out_specs=pl.BlockSpec((tm, D_MODEL), lambda i: (i, 0)),
        out_shape=jax.ShapeDtypeStruct((n, D_MODEL), F32),
        compiler_params=_cparams(("parallel",)),
        name="mlp",
    )(x, w1, b1.reshape(1, -1), w2, b2.reshape(1, -1), g.reshape(1, -1), b.reshape(1, -1))


def _prepare_layer(l, p):
    w_in = p['w_in'][l]
    col = lambda i: w_in[:, _IN_OFFS[i]:_IN_OFFS[i] + _IN_SPLITS[i]]
    pad = jnp.zeros((D_MODEL, W_MAIN - C_AB - 16), F32)
    w_main = jnp.concatenate([col(0), col(1), col(2), col(3), col(6), col(10), col(11), col(4), col(5), pad], axis=1)
    w_na = jnp.concatenate([col(7) * (NA_DH ** -0.5), col(8), col(9)], axis=1)
    s5_wt, s5_we, s5_w2, s5_al = _s5_operators(p['s5_lambda_re'][l], p['s5_lambda_im'][l], p['s5_log_dt'][l],
                                        p['s5_b_re'][l], p['s5_b_im'][l], p['s5_c_re'][l], p['s5_c_im'][l])
    return dict(
        w_main=w_main.astype(BF16), w_na=w_na.astype(BF16), w_gate=col(12).astype(BF16),
        dn_conv_w=p['dn_conv_w'][l], dn_a_log=p['dn_a_log'][l], dn_dt_bias=p['dn_dt_bias'][l],
        dn_norm_g=p['dn_norm_g'][l],
        s5_wt=s5_wt, s5_we=s5_we, s5_w2=s5_w2, s5_al=s5_al, s5_d=p['s5_d'][l], glu_w=p['s5_glu_w'][l].astype(BF16),
        glu_b=p['s5_glu_b'][l],
        na_table=_na_bias_table(p['na_rpb'][l]),
        lru_conv_w=p['lru_conv_w'][l], lru_conv_b=p['lru_conv_b'][l],
        lru_wg=_lru_gate_matrix(p['lru_gate_w'][l]).astype(BF16), lru_gate_b=p['lru_gate_b'][l],
        lru_lambda=p['lru_lambda'][l],
        w_branch=p['w_branch'][l].astype(BF16), w_out=p['w_out'][l].astype(BF16),
        ln1_g=p['ln1_g'][l], ln1_b=p['ln1_b'][l],
        mlp_w1=p['mlp_w1'][l].astype(BF16), mlp_b1=p['mlp_b1'][l], mlp_w2=p['mlp_w2'][l].astype(BF16),
        mlp_b2=p['mlp_b2'][l], ln2_g=p['ln2_g'][l], ln2_b=p['ln2_b'][l])


def _layer(x, bsz, t, lw, ln_in=None):
    if ln_in is None:
        hm, na = _in_proj(x, lw['w_main'], lw['w_na'])
    else:
        x, hm, na = _in_proj(x, lw['w_main'], lw['w_na'], ln=ln_in)
    hm3 = hm.reshape(bsz, t, W_MAIN)
    o_f, o_b = _deltanet(hm3, lw['dn_conv_w'], lw['dn_a_log'], lw['dn_dt_bias'])
    y_s5 = _s5(hm3, lw['s5_wt'], lw['s5_we'], lw['s5_w2'], lw['s5_al'])
    na_o = _natten(na.reshape(bsz, t, 3 * NA_WIDTH), lw['na_table'])
    h_f, h_b = _rglru(hm3, lw['lru_conv_w'], lw['lru_conv_b'], lw['lru_wg'], lw['lru_gate_b'], lw['lru_lambda'])
    n = bsz * t
    x1 = _merge(x, o_f.reshape(n, 512), o_b.reshape(n, 512), hm, y_s5, na_o.reshape(n, NA_WIDTH),
                h_f.reshape(n, LRU_WIDTH), h_b.reshape(n, LRU_WIDTH), lw['dn_norm_g'], lw['s5_d'], lw['glu_w'],
                lw['glu_b'], lw['w_gate'], lw['w_branch'], lw['w_out'], lw['ln1_g'], lw['ln1_b'])
    return _mlp(x1, lw['mlp_w1'], lw['mlp_b1'], lw['mlp_w2'], lw['mlp_b2'], lw['ln2_g'], lw['ln2_b'])


def _trunk(x, ln_g, ln_b, layers):
    bsz, t, _ = x.shape
    h = x.reshape(bsz * t, D_MODEL)
    for l, lw in enumerate(layers):
        h = _layer(h, bsz, t, lw, ln_in=(ln_g, ln_b) if l == 0 else None)
    return h.reshape(bsz, t, D_MODEL)


def kernel(x_prompt, x_sample, ln_in_g, ln_in_b, w_in, dn_conv_w, dn_a_log, dn_dt_bias, dn_norm_g, s5_lambda_re,
           s5_lambda_im, s5_log_dt, s5_b_re, s5_b_im, s5_c_re, s5_c_im, s5_d, s5_glu_w, s5_glu_b, na_rpb, lru_conv_w,
           lru_conv_b, lru_gate_w, lru_gate_b, lru_lambda, w_branch, w_out, ln1_g, ln1_b, mlp_w1, mlp_b1, mlp_w2,
           mlp_b2, ln2_g, ln2_b):
    p = dict(w_in=w_in, dn_conv_w=dn_conv_w, dn_a_log=dn_a_log, dn_dt_bias=dn_dt_bias, dn_norm_g=dn_norm_g,
             s5_lambda_re=s5_lambda_re, s5_lambda_im=s5_lambda_im, s5_log_dt=s5_log_dt, s5_b_re=s5_b_re,
             s5_b_im=s5_b_im, s5_c_re=s5_c_re, s5_c_im=s5_c_im, s5_d=s5_d, s5_glu_w=s5_glu_w, s5_glu_b=s5_glu_b,
             na_rpb=na_rpb, lru_conv_w=lru_conv_w, lru_conv_b=lru_conv_b, lru_gate_w=lru_gate_w,
             lru_gate_b=lru_gate_b, lru_lambda=lru_lambda, w_branch=w_branch, w_out=w_out, ln1_g=ln1_g, ln1_b=ln1_b,
             mlp_w1=mlp_w1, mlp_b1=mlp_b1, mlp_w2=mlp_w2, mlp_b2=mlp_b2, ln2_g=ln2_g, ln2_b=ln2_b)
    layers = [_prepare_layer(l, p) for l in range(DEPTH)]
    return (_trunk(x_prompt, ln_in_g, ln_in_b, layers), _trunk(x_sample, ln_in_g, ln_in_b, layers))
```

```python
import functools
import math

import jax
import jax.numpy as jnp
import numpy as np
from jax import lax
from jax.experimental import pallas as pl
from jax.experimental.pallas import tpu as pltpu

F32 = jnp.float32
BF16 = jnp.bfloat16

D_MODEL = 1024
DEPTH = 2
GRID_W = 64
N_BRANCH = 4
BRANCH_W = 512
DN_HEADS = 4
DN_DK = 128
DN_DV = 128
DN_CONV = 4
DN_CHUNK = 64
S5_WIDTH = 512
S5_P = 16
S5_GROUPS = S5_WIDTH // S5_P
S5_N = 64
NA_HEADS = 8
NA_DH = 64
NA_WIDTH = NA_HEADS * NA_DH
NA_KH = 8
NA_KW = 16
LRU_WIDTH = 512
LRU_BLOCKS = 8
LRU_BW = LRU_WIDTH // LRU_BLOCKS
LRU_CONV = 4
LRU_C = 8.0
D_FF = 4 * D_MODEL
ALPHA = float((2 * DEPTH) ** 0.25)
EPS = 1e-5

_IN_SPLITS = (512, 512, 512, 512, 8, 8, 512, 512, 512, 512, 512, 512, 4096)
_IN_OFFS = tuple(sum(_IN_SPLITS[:i]) for i in range(len(_IN_SPLITS)))

LANE = 128
SUBLANE = 8
C_QKV = 0
C_Z = 1536
C_SU = 2048
C_LX = 2560
C_LG = 3072
C_AB = 3584
W_MAIN = 3712

S5_L = 32
S5_LW = S5_L * S5_P
S5_HW = 4 * LANE
S5_GPB = LANE // S5_P
S5_ROWS = 512

VMEM_LIMIT = 56 * 1024 * 1024


def _cparams(sem):
    return pltpu.CompilerParams(dimension_semantics=sem, vmem_limit_bytes=VMEM_LIMIT)


def _layer_norm(x, g, b):
    mu = jnp.mean(x, axis=-1, keepdims=True)
    xc = x - mu
    var = jnp.mean(xc * xc, axis=-1, keepdims=True)
    return xc * lax.rsqrt(var + EPS) * g + b


def _sigmoid(x):
    return 0.5 * jnp.tanh(0.5 * x) + 0.5


def _const_spec(shape):
    nd = len(shape)
    return pl.BlockSpec(shape, lambda *_: (0,) * nd, pipeline_mode=pl.Buffered(1))


def _project(x, wm_ref, wn_ref, hm_ref, na_ref):
    xb = x.astype(BF16)
    step = 4 * LANE
    for c0 in range(0, W_MAIN, step):
        c1 = min(c0 + step, W_MAIN)
        hm_ref[:, c0:c1] = jnp.dot(xb, wm_ref[:, c0:c1], preferred_element_type=F32)
    for c0 in range(0, 3 * NA_WIDTH, step):
        na_ref[:, c0:c0 + step] = jnp.dot(xb, wn_ref[:, c0:c0 + step], preferred_element_type=F32).astype(BF16)


def _proj_kernel(x_ref, wm_ref, wn_ref, hm_ref, na_ref):
    _project(x_ref[...], wm_ref, wn_ref, hm_ref, na_ref)


def _ln_proj_kernel(x_ref, g_ref, b_ref, wm_ref, wn_ref, xn_ref, hm_ref, na_ref):
    x = _layer_norm(x_ref[...], g_ref[...], b_ref[...])
    xn_ref[...] = x
    _project(x, wm_ref, wn_ref, hm_ref, na_ref)


def _in_proj(x, w_main, w_na, ln=None, tm=256):
    n = x.shape[0]
    tok = lambda w: pl.BlockSpec((tm, w), lambda i: (i, 0))
    w_specs = [_const_spec((D_MODEL, W_MAIN)), _const_spec((D_MODEL, 3 * NA_WIDTH))]
    out_specs = [tok(W_MAIN), tok(3 * NA_WIDTH)]
    out_shape = [jax.ShapeDtypeStruct((n, W_MAIN), F32), jax.ShapeDtypeStruct((n, 3 * NA_WIDTH), BF16)]
    if ln is None:
        return pl.pallas_call(
            _proj_kernel, grid=(n // tm,), in_specs=[tok(D_MODEL)] + w_specs, out_specs=out_specs,
            out_shape=out_shape, compiler_params=_cparams(("parallel",)), name="in_proj",
        )(x, w_main, w_na)
    return pl.pallas_call(
        _ln_proj_kernel, grid=(n // tm,),
        in_specs=[tok(D_MODEL), _const_spec((1, D_MODEL)), _const_spec((1, D_MODEL))] + w_specs,
        out_specs=[tok(D_MODEL)] + out_specs, out_shape=[jax.ShapeDtypeStruct((n, D_MODEL), F32)] + out_shape,
        compiler_params=_cparams(("parallel",)), name="ln_in_proj",
    )(x, ln[0].reshape(1, -1), ln[1].reshape(1, -1), w_main, w_na)


def _conv_centred(prev8, cur, next8, w, first, last):
    tb = cur.shape[0]
    prev8 = jnp.where(first, 0.0, prev8)
    next8 = jnp.where(last, 0.0, next8)
    xp = jnp.concatenate([prev8, cur, next8], axis=0)
    left = (w.shape[0] - 1) // 2
    acc = None
    for j in range(w.shape[0]):
        s = SUBLANE - left + j
        term = xp[s:s + tb] * w[j:j + 1]
        acc = term if acc is None else acc + term
    return acc


def _dot_bf(a, b):
    return jnp.dot(a.astype(BF16), b.astype(BF16), preferred_element_type=F32)


def _dot_nt_bf(a, b):
    return lax.dot_general(a.astype(BF16), b.astype(BF16), (((1,), (1,)), ((), ())), preferred_element_type=F32)


def _dn_pre_kernel(cur_ref, prev_ref, next_ref, cw_ref, o_ref):
    i = pl.program_id(1)
    nb = pl.num_programs(1)
    qkv = _conv_centred(prev_ref[...], cur_ref[...], next_ref[...], cw_ref[...], i == 0, i == nb - 1)
    qkv = qkv * _sigmoid(qkv)
    for h in range(DN_HEADS):
        q = qkv[:, h * DN_DK:(h + 1) * DN_DK]
        k = qkv[:, 512 + h * DN_DK:512 + (h + 1) * DN_DK]
        o_ref[:, h * DN_DK:(h + 1) * DN_DK] = q * (lax.rsqrt(jnp.sum(q * q, axis=-1, keepdims=True) + 1e-6)
                                                    * (DN_DK ** -0.5))
        o_ref[:, 512 + h * DN_DK:512 + (h + 1) * DN_DK] = k * lax.rsqrt(jnp.sum(k * k, axis=-1, keepdims=True) + 1e-6)
    o_ref[:, 1024:] = qkv[:, 1024:]


def _dn_pre(hm3, conv_w, tb=512):
    bsz, t, _ = hm3.shape
    nb = t // tb
    qkv_w = 3 * 512
    return pl.pallas_call(
        _dn_pre_kernel,
        grid=(bsz, nb),
        in_specs=list(_halo_specs(tb, qkv_w, 0, nb, t, False)) + [_const_spec((DN_CONV, qkv_w))],
        out_specs=pl.BlockSpec((None, tb, qkv_w), lambda b, i: (b, i, 0)),
        out_shape=jax.ShapeDtypeStruct((bsz, t, qkv_w), F32),
        compiler_params=_cparams(("parallel", "parallel")),
        name="deltanet_pre",
    )(hm3, hm3, hm3, conv_w)


def _dn_prepare(d, qkv, ab, alog, dtb):
    tb = qkv.shape[0]
    c = DN_CHUNK
    gates = -jnp.exp(alog) * jax.nn.softplus(ab + dtb)
    betas = _sigmoid(ab)

    row = lax.broadcasted_iota(jnp.int32, (c, c), 0)
    col = lax.broadcasted_iota(jnp.int32, (c, c), 1)
    incl = col <= row if d == 0 else col >= row
    strict = col < row if d == 0 else col > row
    eye = (row == col).astype(F32)
    edge = c - 1 if d == 0 else 0
    n_chunks = tb // c

    g_hi = gates.astype(BF16)
    rest = gates - g_hi.astype(F32)
    g_mid = rest.astype(BF16)
    g_lo = (rest - g_mid.astype(F32)).astype(BF16)
    g3 = jnp.concatenate([g_hi, g_mid, g_lo], axis=1)
    tri = incl.astype(BF16)
    gc_parts = [jnp.dot(tri, g3[ci * c:(ci + 1) * c], preferred_element_type=F32) for ci in range(n_chunks)]
    gc_all = jnp.concatenate([p[:, :LANE] + p[:, LANE:2 * LANE] + p[:, 2 * LANE:] for p in gc_parts], axis=0)

    units = [(ci, h) for ci in range(n_chunks) for h in range(DN_HEADS)]
    rows = lambda ci: slice(ci * c, (ci + 1) * c)
    lane_of = lambda h: d * DN_HEADS + h
    q_u = [qkv[rows(ci), h * DN_DK:(h + 1) * DN_DK] for ci, h in units]
    k_u = [qkv[rows(ci), 512 + h * DN_DK:512 + (h + 1) * DN_DK] for ci, h in units]
    v_u = [qkv[rows(ci), 1024 + h * DN_DV:1024 + (h + 1) * DN_DV] for ci, h in units]
    beta_u = [jnp.broadcast_to(betas[rows(ci), 8 + lane_of(h):9 + lane_of(h)], (c, DN_DK)) for ci, h in units]
    gcr_u = [jnp.broadcast_to(gc_all[rows(ci), lane_of(h):lane_of(h) + 1], (c, DN_DK)) for ci, h in units]
    gcl_u = [g.T[:c, :] for g in gcr_u]
    decay_u = [jnp.exp(jnp.where(incl, gr[:, :c] - gl, -1e30)) for gr, gl in zip(gcr_u, gcl_u)]
    egc_u = [jnp.exp(g) for g in gcr_u]
    glast_u = [g[edge:edge + 1, :] for g in gcr_u]
    kb_u = [k * b for k, b in zip(k_u, beta_u)]
    vb_u = [v * b for v, b in zip(v_u, beta_u)]
    kk_u = [_dot_nt_bf(jnp.concatenate([kb, q], axis=0), k) for kb, q, k in zip(kb_u, q_u, k_u)]
    x_u = [-jnp.where(strict, kk[:c] * dec, 0.0) for kk, dec in zip(kk_u, decay_u)]
    qk_u = [jnp.where(incl, kk[c:] * dec, 0.0).astype(BF16) for kk, dec in zip(kk_u, decay_u)]
    t_u = [eye + x for x in x_u]
    x_u = [_dot_bf(x, x) for x in x_u]
    for _ in range(4):
        both = [_dot_bf(jnp.concatenate([t, x], axis=0), x) for t, x in zip(t_u, x_u)]
        t_u = [t + b[:c] for t, b in zip(t_u, both)]
        x_u = [b[c:] for b in both]
    t_u = [t + _dot_bf(t, x) for t, x in zip(t_u, x_u)]
    sol_u = [_dot_bf(t, jnp.concatenate([vb, kb * e], axis=1)) for t, vb, kb, e in zip(t_u, vb_u, kb_u, egc_u)]
    u_u = [s[:, :DN_DV] for s in sol_u]
    wq_u = [jnp.concatenate([s[:, DN_DV:], q * e], axis=0).astype(BF16) for s, q, e in zip(sol_u, q_u, egc_u)]
    qkkd_u = [jnp.concatenate([qk, (k * jnp.exp(gl - g)).T.astype(BF16)], axis=0)
              for qk, k, gl, g in zip(qk_u, k_u, glast_u, gcr_u)]
    egl_u = [jnp.exp(gl) for gl in glast_u]
    return dict(zip(units, zip(u_u, wq_u, qkkd_u, egl_u)))


def _dn_kernel(cur_f, ab_f, cur_b, ab_b, alog_ref, dtb_ref, of_ref, ob_ref, s_ref):
    @pl.when(pl.program_id(1) == 0)
    def _():
        s_ref[...] = jnp.zeros_like(s_ref)

    alog = alog_ref[...]
    dtb = dtb_ref[...]
    prep = (_dn_prepare(0, cur_f[...], ab_f[...], alog, dtb), _dn_prepare(1, cur_b[...], ab_b[...], alog, dtb))
    c = DN_CHUNK
    n_chunks = cur_f.shape[0] // c
    o_refs = (of_ref, ob_ref)
    chains = [(d, h) for d in range(2) for h in range(DN_HEADS)]
    chunk_at = lambda d, step: step if d == 0 else n_chunks - 1 - step
    state = [s_ref[d * DN_HEADS + h] for d, h in chains]
    for step in range(n_chunks):
        ops = [prep[d][(chunk_at(d, step), h)] for d, h in chains]
        ws = [jnp.dot(wq, s.astype(BF16), preferred_element_type=F32) for (_, wq, _, _), s in zip(ops, state)]
        v_new = [(u - w[:c]).astype(BF16) for (u, _, _, _), w in zip(ops, ws)]
        upd = [jnp.dot(qkkd, vn, preferred_element_type=F32) for (_, _, qkkd, _), vn in zip(ops, v_new)]
        state = [s * egl + up[c:] for s, (_, _, _, egl), up in zip(state, ops, upd)]
        for (d, h), w, up in zip(chains, ws, upd):
            r0 = chunk_at(d, step) * c
            o_refs[d][r0:r0 + c, h * DN_DV:(h + 1) * DN_DV] = w[c:] + up[:c]
    for (d, h), s in zip(chains, state):
        s_ref[d * DN_HEADS + h] = s


def _halo_specs(tb, width, col_block, nb, t, reverse):
    per = tb // SUBLANE
    last8 = t // SUBLANE - 1
    if reverse:
        blk = lambda i: nb - 1 - i
    else:
        blk = lambda i: i
    cur = pl.BlockSpec((None, tb, width), lambda b, i: (b, blk(i), col_block))
    prev = pl.BlockSpec((None, SUBLANE, width), lambda b, i: (b, jnp.maximum(blk(i) * per - 1, 0), col_block))
    nxt = pl.BlockSpec((None, SUBLANE, width), lambda b, i: (b, jnp.minimum((blk(i) + 1) * per, last8), col_block))
    return cur, prev, nxt


def _deltanet(hm3, conv_w, a_log, dt_bias, tb=256):
    bsz, t, _ = hm3.shape
    nb = t // tb
    qkv_w = 3 * 512
    qkvn = _dn_pre(hm3, conv_w)
    alog_v = jnp.zeros((1, LANE), F32).at[0, :8].set(a_log.reshape(-1))
    dtb_v = jnp.zeros((1, LANE), F32).at[0, :8].set(dt_bias.reshape(-1))
    ab_col = C_AB // LANE
    fwd = lambda b, i: (b, i, 0)
    bwd = lambda b, i: (b, nb - 1 - i, 0)
    in_specs = [pl.BlockSpec((None, tb, qkv_w), fwd), pl.BlockSpec((None, tb, LANE), lambda b, i: (b, i, ab_col)),
                pl.BlockSpec((None, tb, qkv_w), bwd),
                pl.BlockSpec((None, tb, LANE), lambda b, i: (b, nb - 1 - i, ab_col)),
                _const_spec((1, LANE)), _const_spec((1, LANE))]
    out_specs = [pl.BlockSpec((None, tb, 512), fwd), pl.BlockSpec((None, tb, 512), bwd)]
    return pl.pallas_call(
        _dn_kernel,
        grid=(bsz, nb),
        in_specs=in_specs,
        out_specs=out_specs,
        out_shape=[jax.ShapeDtypeStruct((bsz, t, 512), F32)] * 2,
        scratch_shapes=[pltpu.VMEM((2 * DN_HEADS, DN_DK, DN_DV), F32)],
        compiler_params=_cparams(("parallel", "arbitrary")),
        name="deltanet",
    )(qkvn, hm3, qkvn, hm3, alog_v, dtb_v)


def _s5_discretise(lam_re, lam_im, log_dt, b_re, b_im):
    dt = jnp.exp(log_dt)[:, None]
    mag = jnp.exp(lam_re * dt)
    ab_re = mag * jnp.cos(lam_im * dt)
    ab_im = mag * jnp.sin(lam_im * dt)
    den = jnp.square(lam_re) + jnp.square(lam_im)
    nr, ni = ab_re - 1.0, ab_im
    kr = ((nr * lam_re + ni * lam_im) / den)[..., None]
    ki = ((ni * lam_re - nr * lam_im) / den)[..., None]
    return kr * b_re - ki * b_im, kr * b_im + ki * b_re


def _s5_operators(lam_re, lam_im, log_dt, b_re, b_im, c_re, c_im):
    hi = lax.Precision.HIGHEST
    L, G, N, P = S5_L, S5_GROUPS, S5_N, S5_P
    j = jnp.arange(L + 1, dtype=F32)[:, None, None]
    kcomb = 0.0
    e_cols, f_rows, al = [], [], []
    for d in range(2):
        dt = jnp.exp(log_dt[d])[:, None]
        bb_re, bb_im = _s5_discretise(lam_re[d], lam_im[d], log_dt[d], b_re, b_im)
        mag = jnp.exp(lam_re[d] * dt * j)
        ang = lam_im[d] * dt * j
        aj_re, aj_im = mag * jnp.cos(ang), mag * jnp.sin(ang)
        ca_re = c_re[None] * aj_re[:, :, None, :] - c_im[None] * aj_im[:, :, None, :]
        ca_im = c_re[None] * aj_im[:, :, None, :] + c_im[None] * aj_re[:, :, None, :]
        kj = (jnp.einsum('jgpn,gnq->jgpq', ca_re[:L], bb_re, precision=hi)
              - jnp.einsum('jgpn,gnq->jgpq', ca_im[:L], bb_im, precision=hi))
        zeros = jnp.zeros((L - 1,) + kj.shape[1:], F32)
        if d == 0:
            kcomb = kcomb + jnp.concatenate([zeros, kj], axis=0)
        else:
            kcomb = kcomb + jnp.concatenate([kj[::-1], zeros], axis=0)
        pw_re = aj_re[:L][::-1] if d == 0 else aj_re[:L]
        pw_im = aj_im[:L][::-1] if d == 0 else aj_im[:L]
        e_re = pw_re[..., None] * bb_re[None] - pw_im[..., None] * bb_im[None]
        e_im = pw_re[..., None] * bb_im[None] + pw_im[..., None] * bb_re[None]
        for e in (e_re, e_im):
            e = e.transpose(1, 0, 3, 2).reshape(G, L * P, N)
            e_cols.append(jnp.pad(e, ((0, 0), (0, 0), (0, LANE - N))))
        sel = slice(1, L + 1)
        fr = ca_re[sel] if d == 0 else ca_re[sel][::-1]
        fi = ca_im[sel] if d == 0 else ca_im[sel][::-1]
        for f in (fr, -fi):
            f = f.transpose(1, 3, 0, 2).reshape(G, N, L * P)
            f_rows.append(jnp.pad(f, ((0, 0), (0, LANE - N), (0, 0))))
        al += [jnp.pad(aj_re[L], ((0, 0), (0, LANE - N))), jnp.pad(aj_im[L], ((0, 0), (0, LANE - N)))]
    kc = kcomb.astype(BF16).transpose(1, 3, 0, 2)
    toep = jnp.stack([kc[:, :, L - 1 - s:2 * L - 1 - s, :] for s in range(L)], axis=1)
    toep = toep.reshape(G, L * P, L * P)
    we = jnp.concatenate(e_cols, axis=2).astype(BF16)
    w2 = jnp.concatenate(f_rows, axis=1).astype(BF16)
    return toep, we, w2, jnp.stack(al, axis=1)


def _s5_kernel(x_ref, wt_ref, we_ref, w2_ref, al_ref, y_ref, u_scr, yv_scr, hloc_ref, hin_ref, *, n_chunks, nbs):
    g8 = pl.program_id(2)
    rows = nbs * n_chunks
    blk = lax.broadcasted_iota(jnp.int32, (S5_GPB, LANE), 1) // S5_P
    lane_tiles = S5_L // S5_GPB
    groups = range(S5_GPB)

    def regroup(vregs):
        m = list(vregs)
        dist = S5_GPB // 2
        while dist:
            upper = (blk & dist) != 0
            nxt = list(m)
            for v in groups:
                if not v & dist:
                    w = v + dist
                    nxt[v] = jnp.where(upper, pltpu.roll(m[w], dist * S5_P, axis=1), m[v])
                    nxt[w] = jnp.where(upper, m[w], pltpu.roll(m[v], LANE - dist * S5_P, axis=1))
            m = nxt
            dist //= 2
        return m

    @pl.when(g8 == 0)
    def _():
        def gather_rows(rg, _):
            rsl = pl.ds(pl.multiple_of(rg * SUBLANE, SUBLANE), SUBLANE)
            for k in range(lane_tiles):
                pieces = regroup([x_ref[S5_GPB * k + j, rsl, :] for j in groups])
                for g in groups:
                    u_scr[g, rsl, k * LANE:(k + 1) * LANE] = pieces[g]
            return 0

        lax.fori_loop(0, rows // SUBLANE, gather_rows, 0, unroll=2)

    u = u_scr[g8].astype(BF16)
    y = jnp.dot(u, wt_ref[...], preferred_element_type=F32)
    hloc = jnp.dot(u, we_ref[...], preferred_element_type=F32)
    for part in range(4):
        hloc_ref[part] = hloc[:, part * LANE:(part + 1) * LANE]
    al = al_ref[...]
    a_re = (al[0:1], al[2:3])
    a_im = (al[1:2], al[3:4])

    def body(cidx, carry):
        new = []
        for d in range(2):
            cr, ci = carry[2 * d], carry[2 * d + 1]
            cc = cidx if d == 0 else n_chunks - 1 - cidx
            seqs = pl.ds(cc, nbs, stride=n_chunks)
            hin_ref[2 * d, seqs, :] = cr
            hin_ref[2 * d + 1, seqs, :] = ci
            lr = hloc_ref[2 * d, seqs, :]
            li = hloc_ref[2 * d + 1, seqs, :]
            new += [a_re[d] * cr - a_im[d] * ci + lr, a_re[d] * ci + a_im[d] * cr + li]
        return tuple(new)

    zero = jnp.zeros((nbs, LANE), F32)
    lax.fori_loop(0, n_chunks, body, (zero, zero, zero, zero))
    hin = jnp.concatenate([hin_ref[part] for part in range(4)], axis=1).astype(BF16)
    yv_scr[g8] = y + jnp.dot(hin, w2_ref[...], preferred_element_type=F32)

    @pl.when(g8 == S5_GPB - 1)
    def _():
        def scatter_rows(rg, _):
            rsl = pl.ds(pl.multiple_of(rg * SUBLANE, SUBLANE), SUBLANE)
            for k in range(lane_tiles):
                pieces = regroup([yv_scr[g, rsl, k * LANE:(k + 1) * LANE] for g in groups])
                for j in groups:
                    y_ref[S5_GPB * k + j, rsl, :] = pieces[j]
            return 0

        lax.fori_loop(0, rows // SUBLANE, scatter_rows, 0, unroll=2)


def _s5(hm3, wt, we, w2, al):
    bsz, t, _ = hm3.shape
    n_chunks = t // S5_L
    rows = n_chunks * bsz
    nbs = S5_ROWS // n_chunks
    xs = hm3[:, :, C_SU:C_SU + S5_WIDTH].reshape(rows, S5_L, S5_WIDTH).transpose(1, 0, 2)
    tok_idx = lambda q, s, g: (0, s, q)
    group = lambda q, s, g: (q * S5_GPB + g, 0, 0)
    ys = pl.pallas_call(
        functools.partial(_s5_kernel, n_chunks=n_chunks, nbs=nbs),
        grid=(S5_WIDTH // LANE, rows // S5_ROWS, S5_GPB),
        in_specs=[pl.BlockSpec((S5_L, S5_ROWS, LANE), tok_idx, pipeline_mode=pl.Buffered(1)),
                  pl.BlockSpec((None, S5_LW, S5_LW), group), pl.BlockSpec((None, S5_LW, S5_HW), group),
                  pl.BlockSpec((None, S5_HW, S5_LW), group),
                  pl.BlockSpec((None, 4, LANE), group)],
        out_specs=pl.BlockSpec((S5_L, S5_ROWS, LANE), tok_idx),
        out_shape=jax.ShapeDtypeStruct((S5_L, rows, S5_WIDTH), F32),
        scratch_shapes=[pltpu.VMEM((S5_GPB, S5_ROWS, S5_LW), F32), pltpu.VMEM((S5_GPB, S5_ROWS, S5_LW), F32),
                        pltpu.VMEM((4, S5_ROWS, LANE), F32), pltpu.VMEM((4, S5_ROWS, LANE), F32)],
        compiler_params=_cparams(("arbitrary", "arbitrary", "arbitrary")),
        name="s5",
    )(xs, wt, we, w2, al)
    return ys.transpose(1, 0, 2).reshape(bsz * t, S5_WIDTH)


def _na_bias_table(rpb):
    c = np.arange(GRID_W)
    c0 = np.clip(c - NA_KW // 2, 0, GRID_W - NA_KW)
    col_in = (c[None, :] >= c0[:, None]) & (c[None, :] < c0[:, None] + NA_KW)
    dc = np.clip(c[None, :] - c[:, None], -(NA_KW - 1), NA_KW - 1) + (NA_KW - 1)
    onehot = (dc[:, :, None] == np.arange(2 * NA_KW - 1)).astype(np.float32)
    by_col = jnp.einsum('hrm,qkm->hrqk', rpb.astype(F32), onehot, precision=lax.Precision.HIGHEST)
    tab = jnp.stack([by_col[:, NA_KH - 1 - dl:2 * NA_KH - 1 - dl] for dl in range(NA_KH)], axis=1)
    tab = jnp.where(col_in[None, None, None], tab, -1e30)
    return tab.transpose(0, 1, 3, 2, 4).reshape(NA_HEADS, NA_KH, GRID_W, NA_KH * GRID_W)


def _na_kernel(q_ref, k_ref, v_ref, tab_ref, o_ref, *, rows_per_step, n_rows):
    i = pl.program_id(2)
    w = GRID_W
    nk = NA_KH * w
    lane = lax.broadcasted_iota(jnp.int32, (2 * w, LANE), 1)
    rowi = lax.broadcasted_iota(jnp.int32, (2 * w, LANE), 0)
    own = (lane < NA_DH) == (rowi < w)
    low = lax.broadcasted_iota(jnp.int32, (w, LANE), 1) < NA_DH
    steps = range(rows_per_step)
    r = [i * rows_per_step + rr for rr in steps]
    r0 = [jnp.clip(x - NA_KH // 2, 0, n_rows - NA_KH) for x in r]
    krows = [pl.ds(pl.multiple_of(x * w, w), nk) for x in r0]
    q2 = [q_ref[rr * w:(rr + 1) * w, :] for rr in steps]
    q2 = [jnp.where(own, jnp.concatenate([q, q], axis=0), jnp.zeros((2 * w, LANE), q.dtype)) for q in q2]
    s = [lax.dot_general(q, k_ref[kr, :], (((1,), (1,)), ((), ())), preferred_element_type=F32)
         for q, kr in zip(q2, krows)]
    s = [jnp.concatenate([x[:w] + tab_ref[0, a - b], x[w:] + tab_ref[1, a - b]], axis=0) for x, a, b in zip(s, r, r0)]
    m = [jnp.max(x, axis=-1, keepdims=True) for x in s]
    p = [jnp.exp(x - y) for x, y in zip(s, m)]
    l = [jnp.sum(x, axis=-1, keepdims=True) for x in p]
    pv = [jnp.dot(x.astype(BF16), v_ref[kr, :], preferred_element_type=F32) / y for x, kr, y in zip(p, krows, l)]
    for rr, x in zip(steps, pv):
        o_ref[rr * w:(rr + 1) * w, :] = jnp.where(low, x[:w], x[w:]).astype(o_ref.dtype)


def _natten(na3, table, rows_per_step=16):
    bsz, t, _ = na3.shape
    n_rows = t // GRID_W
    tq = rows_per_step * GRID_W
    pairs = NA_HEADS // 2
    kcol = NA_WIDTH // LANE
    return pl.pallas_call(
        functools.partial(_na_kernel, rows_per_step=rows_per_step, n_rows=n_rows),
        grid=(pairs, bsz, t // tq),
        in_specs=[pl.BlockSpec((None, tq, LANE), lambda p, b, i: (b, i, p)),
                  pl.BlockSpec((None, t, LANE), lambda p, b, i: (b, 0, kcol + p)),
                  pl.BlockSpec((None, t, LANE), lambda p, b, i: (b, 0, 2 * kcol + p)),
                  pl.BlockSpec((2, NA_KH, GRID_W, NA_KH * GRID_W), lambda p, b, i: (p, 0, 0, 0))],
        out_specs=pl.BlockSpec((None, tq, LANE), lambda p, b, i: (b, i, p)),
        out_shape=jax.ShapeDtypeStruct((bsz, t, NA_WIDTH), BF16),
        compiler_params=_cparams(("parallel", "parallel", "parallel")),
        name="natten",
    )(na3, na3, na3, table)


def _lru_coefficients(d, cur, prev8, next8, first, last, cw, cb, wg_ref, gb, sp_lam, a_scr, b_scr):
    xc = _conv_centred(prev8, cur, next8, cw, first, last) + cb
    width = 2 * LRU_WIDTH
    gates = jnp.dot(xc.astype(BF16), wg_ref[:, d * width:(d + 1) * width], preferred_element_type=F32)
    gates = _sigmoid(gates + gb[:, d * width:(d + 1) * width])
    log_a = -LRU_C * gates[:, :LRU_WIDTH] * sp_lam[d:d + 1]
    a = jnp.exp(log_a)
    a_scr[d] = a
    b_scr[d] = jnp.sqrt(1.0 - a * a) * gates[:, LRU_WIDTH:] * xc


def _lru_scan(a_scr, b_scr, carry_ref, o_refs):
    n_groups = a_scr.shape[1] // SUBLANE
    sub = lax.broadcasted_iota(jnp.int32, (SUBLANE, LRU_WIDTH), 0)

    def body(gi, hs):
        new = []
        for d in range(2):
            h = hs[d]
            grp = gi if d == 0 else n_groups - 1 - gi
            rows = pl.ds(pl.multiple_of(grp * SUBLANE, SUBLANE), SUBLANE)
            a8 = a_scr[d, rows, :]
            b8 = b_scr[d, rows, :]
            for s in (1, 2, 4):
                shift = s if d == 0 else SUBLANE - s
                has_prev = sub >= s if d == 0 else sub < SUBLANE - s
                a_prev = pltpu.roll(a8, shift, axis=0)
                b_prev = pltpu.roll(b8, shift, axis=0)
                b8 = jnp.where(has_prev, a8 * b_prev + b8, b8)
                a8 = jnp.where(has_prev, a8 * a_prev, a8)
            out = a8 * h + b8
            o_refs[d][rows, :] = out
            last = SUBLANE - 1 if d == 0 else 0
            new.append(out[last:last + 1])
        return tuple(new)

    hf, hb = lax.fori_loop(0, n_groups, body, (carry_ref[0:1], carry_ref[1:2]))
    carry_ref[0:1] = hf
    carry_ref[1:2] = hb


def _lru_kernel(cur_f, prev_f, next_f, cur_b, prev_b, next_b, cw_ref, cb_ref, wg_ref, gb_ref, lam_ref,
                hf_ref, hb_ref, a_scr, b_scr, carry_ref):
    i = pl.program_id(1)
    nb = pl.num_programs(1)

    @pl.when(i == 0)
    def _():
        carry_ref[...] = jnp.zeros_like(carry_ref)

    cw = cw_ref[...]
    cb = cb_ref[...]
    gb = gb_ref[...]
    sp_lam = jax.nn.softplus(-lam_ref[...])
    _lru_coefficients(0, cur_f[...], prev_f[...], next_f[...], i == 0, i == nb - 1, cw, cb, wg_ref, gb, sp_lam,
                      a_scr, b_scr)
    _lru_coefficients(1, cur_b[...], prev_b[...], next_b[...], i == nb - 1, i == 0, cw, cb, wg_ref, gb, sp_lam,
                      a_scr, b_scr)
    _lru_scan(a_scr, b_scr, carry_ref, (hf_ref, hb_ref))


def _lru_gate_matrix(gate_w):
    eye = jnp.eye(LRU_BLOCKS, dtype=gate_w.dtype)
    full = jnp.einsum('dgncm,nk->ncdgkm', gate_w, eye)
    return full.reshape(LRU_WIDTH, 4 * LRU_WIDTH)


def _rglru(hm3, conv_w, conv_b, wg, gate_b, lam, tb=512):
    bsz, t, _ = hm3.shape
    nb = t // tb
    col = C_LX // LRU_WIDTH
    in_specs = list(_halo_specs(tb, LRU_WIDTH, col, nb, t, False)) + list(_halo_specs(tb, LRU_WIDTH, col, nb, t, True))
    in_specs += [_const_spec((LRU_CONV, LRU_WIDTH)), _const_spec((1, LRU_WIDTH)),
                 _const_spec((LRU_WIDTH, 4 * LRU_WIDTH)), _const_spec((1, 4 * LRU_WIDTH)), _const_spec((2, LRU_WIDTH))]
    out_specs = [pl.BlockSpec((None, tb, LRU_WIDTH), lambda b, i: (b, i, 0)),
                 pl.BlockSpec((None, tb, LRU_WIDTH), lambda b, i: (b, nb - 1 - i, 0))]
    return pl.pallas_call(
        _lru_kernel,
        grid=(bsz, nb),
        in_specs=in_specs,
        out_specs=out_specs,
        out_shape=[jax.ShapeDtypeStruct((bsz, t, LRU_WIDTH), F32)] * 2,
        scratch_shapes=[pltpu.VMEM((2, tb, LRU_WIDTH), F32), pltpu.VMEM((2, tb, LRU_WIDTH), F32),
                        pltpu.VMEM((2, LRU_WIDTH), F32)],
        compiler_params=_cparams(("parallel", "arbitrary")),
        name="rglru",
    )(hm3, hm3, hm3, hm3, hm3, hm3, conv_w, conv_b.reshape(1, -1), wg, gate_b.reshape(1, -1), lam)


def _merge_kernel(x_ref, of_ref, ob_ref, z_ref, ys_ref, su_ref, na_ref, hf_ref, hb_ref, lg_ref,
                  ng_ref, sd_ref, gw_ref, gbias_ref, wgt_ref, wbr_ref, wout_ref, lng_ref, lnb_ref, o_ref):
    x = x_ref[...]
    xb = x.astype(BF16)
    o = of_ref[...] + ob_ref[...]
    z = z_ref[...]
    parts = []
    for h in range(DN_HEADS):
        oh = o[:, h * DN_DV:(h + 1) * DN_DV]
        ms = jnp.mean(oh * oh, axis=-1, keepdims=True)
        parts.append(oh * lax.rsqrt(ms + EPS) * ng_ref[...])
    y_a = jnp.concatenate(parts, axis=1) * (z * _sigmoid(z))
    y = jax.nn.gelu(ys_ref[...] + sd_ref[...] * su_ref[...])
    y_b = y * _sigmoid(jnp.dot(y.astype(BF16), gw_ref[...], preferred_element_type=F32) + gbias_ref[...])
    y_d = (hf_ref[...] + hb_ref[...]) * jax.nn.gelu(lg_ref[...])
    ys = (y_a.astype(BF16), y_b.astype(BF16), na_ref[...], y_d.astype(BF16))
    acc = None
    for n in range(N_BRANCH):
        gate = _sigmoid(jnp.dot(xb, wgt_ref[:, n * D_MODEL:(n + 1) * D_MODEL], preferred_element_type=F32))
        term = gate * jnp.dot(ys[n], wbr_ref[n], preferred_element_type=F32)
        acc = term if acc is None else acc + term
    mix = jnp.dot(acc.astype(BF16), wout_ref[...], preferred_element_type=F32)
    o_ref[...] = _layer_norm(ALPHA * x + mix, lng_ref[...], lnb_ref[...])


def _merge(x, o_f, o_b, hm, y_s5, na_o, h_f, h_b, norm_g, s5_d, glu_w, glu_b, w_gate, w_branch, w_out, ln_g, ln_b,
           tm=256):
    n = x.shape[0]
    tok = lambda w, cb=0: pl.BlockSpec((tm, w), lambda i: (i, cb))
    in_specs = [tok(D_MODEL), tok(512), tok(512), tok(512, C_Z // 512), tok(512), tok(512, C_SU // 512), tok(512),
                tok(512), tok(512), tok(512, C_LG // 512),
                _const_spec((1, DN_DV)), _const_spec((1, S5_WIDTH)), _const_spec((S5_WIDTH, S5_WIDTH)),
                _const_spec((1, S5_WIDTH)), _const_spec((D_MODEL, N_BRANCH * D_MODEL)),
                _const_spec((N_BRANCH, BRANCH_W, D_MODEL)), _const_spec((D_MODEL, D_MODEL)),
                _const_spec((1, D_MODEL)), _const_spec((1, D_MODEL))]
    return pl.pallas_call(
        _merge_kernel,
        grid=(n // tm,),
        in_specs=in_specs,
        out_specs=pl.BlockSpec((tm, D_MODEL), lambda i: (i, 0)),
        out_shape=jax.ShapeDtypeStruct((n, D_MODEL), F32),
        compiler_params=_cparams(("parallel",)),
        name="merge",
    )(x, o_f, o_b, hm, y_s5, hm, na_o, h_f, h_b, hm, norm_g.reshape(1, -1), s5_d.reshape(1, -1), glu_w,
      glu_b.reshape(1, -1), w_gate, w_branch, w_out, ln_g.reshape(1, -1), ln_b.reshape(1, -1))


def _mlp_kernel(x_ref, w1_ref, b1_ref, w2_ref, b2_ref, g_ref, b_ref, o_ref):
    x = x_ref[...]
    xb = x.astype(BF16)
    acc = None
    for c0 in range(0, D_FF, D_MODEL):
        f = jnp.dot(xb, w1_ref[:, c0:c0 + D_MODEL], preferred_element_type=F32) + b1_ref[:, c0:c0 + D_MODEL]
        f = jnp.square(jnp.maximum(f, 0.0))
        term = jnp.dot(f.astype(BF16), w2_ref[c0:c0 + D_MODEL, :], preferred_element_type=F32)
        acc = term if acc is None else acc + term
    o_ref[...] = _layer_norm(ALPHA * x + acc + b2_ref[...], g_ref[...], b_ref[...])


def _mlp(x, w1, b1, w2, b2, g, b, tm=512):
    n = x.shape[0]
    return pl.pallas_call(
        _mlp_kernel,
        grid=(n // tm,),
        in_specs=[pl.BlockSpec((tm, D_MODEL), lambda i: (i, 0)), _const_spec((D_MODEL, D_FF)), _const_spec((1, D_FF)),
                  _const_spec((D_FF, D_MODEL)), _const_spec((1, D_MODEL)), _const_spec((1, D_MODEL)),
                  _const_spec((1, D_MODEL))],
        out_specs=pl.BlockSpec((tm, D_MODEL), lambda i: (i, 0)),
        out_shape=jax.ShapeDtypeStruct((n, D_MODEL), F32),
        compiler_params=_cparams(("parallel",)),
        name="mlp",
    )(x, w1, b1.reshape(1, -1), w2, b2.reshape(1, -1), g.reshape(1, -1), b.reshape(1, -1))


def _prepare_layer(l, p):
    w_in = p['w_in'][l]
    col = lambda i: w_in[:, _IN_OFFS[i]:_IN_OFFS[i] + _IN_SPLITS[i]]
    pad = jnp.zeros((D_MODEL, W_MAIN - C_AB - 16), F32)
    w_main = jnp.concatenate([col(0), col(1), col(2), col(3), col(6), col(10), col(11), col(4), col(5), pad], axis=1)
    w_na = jnp.concatenate([col(7) * (NA_DH ** -0.5), col(8), col(9)], axis=1)
    s5_wt, s5_we, s5_w2, s5_al = _s5_operators(p['s5_lambda_re'][l], p['s5_lambda_im'][l], p['s5_log_dt'][l],
                                        p['s5_b_re'][l], p['s5_b_im'][l], p['s5_c_re'][l], p['s5_c_im'][l])
    return dict(
        w_main=w_main.astype(BF16), w_na=w_na.astype(BF16), w_gate=col(12).astype(BF16),
        dn_conv_w=p['dn_conv_w'][l], dn_a_log=p['dn_a_log'][l], dn_dt_bias=p['dn_dt_bias'][l],
        dn_norm_g=p['dn_norm_g'][l],
        s5_wt=s5_wt, s5_we=s5_we, s5_w2=s5_w2, s5_al=s5_al, s5_d=p['s5_d'][l], glu_w=p['s5_glu_w'][l].astype(BF16),
        glu_b=p['s5_glu_b'][l],
        na_table=_na_bias_table(p['na_rpb'][l]),
        lru_conv_w=p['lru_conv_w'][l], lru_conv_b=p['lru_conv_b'][l],
        lru_wg=_lru_gate_matrix(p['lru_gate_w'][l]).astype(BF16), lru_gate_b=p['lru_gate_b'][l],
        lru_lambda=p['lru_lambda'][l],
        w_branch=p['w_branch'][l].astype(BF16), w_out=p['w_out'][l].astype(BF16),
        ln1_g=p['ln1_g'][l], ln1_b=p['ln1_b'][l],
        mlp_w1=p['mlp_w1'][l].astype(BF16), mlp_b1=p['mlp_b1'][l], mlp_w2=p['mlp_w2'][l].astype(BF16),
        mlp_b2=p['mlp_b2'][l], ln2_g=p['ln2_g'][l], ln2_b=p['ln2_b'][l])


def _layer(x, bsz, t, lw, ln_in=None):
    if ln_in is None:
        hm, na = _in_proj(x, lw['w_main'], lw['w_na'])
    else:
        x, hm, na = _in_proj(x, lw['w_main'], lw['w_na'], ln=ln_in)
    hm3 = hm.reshape(bsz, t, W_MAIN)
    o_f, o_b = _deltanet(hm3, lw['dn_conv_w'], lw['dn_a_log'], lw['dn_dt_bias'])
    y_s5 = _s5(hm3, lw['s5_wt'], lw['s5_we'], lw['s5_w2'], lw['s5_al'])
    na_o = _natten(na.reshape(bsz, t, 3 * NA_WIDTH), lw['na_table'])
    h_f, h_b = _rglru(hm3, lw['lru_conv_w'], lw['lru_conv_b'], lw['lru_wg'], lw['lru_gate_b'], lw['lru_lambda'])
    n = bsz * t
    x1 = _merge(x, o_f.reshape(n, 512), o_b.reshape(n, 512), hm, y_s5, na_o.reshape(n, NA_WIDTH),
                h_f.reshape(n, LRU_WIDTH), h_b.reshape(n, LRU_WIDTH), lw['dn_norm_g'], lw['s5_d'], lw['glu_w'],
                lw['glu_b'], lw['w_gate'], lw['w_branch'], lw['w_out'], lw['ln1_g'], lw['ln1_b'])
    return _mlp(x1, lw['mlp_w1'], lw['mlp_b1'], lw['mlp_w2'], lw['mlp_b2'], lw['ln2_g'], lw['ln2_b'])


def _trunk(x, ln_g, ln_b, layers):
    bsz, t, _ = x.shape
    h = x.reshape(bsz * t, D_MODEL)
    for l, lw in enumerate(layers):
        h = _layer(h, bsz, t, lw, ln_in=(ln_g, ln_b) if l == 0 else None)
    return h.reshape(bsz, t, D_MODEL)


def kernel(x_prompt, x_sample, ln_in_g, ln_in_b, w_in, dn_conv_w, dn_a_log, dn_dt_bias, dn_norm_g, s5_lambda_re,
           s5_lambda_im, s5_log_dt, s5_b_re, s5_b_im, s5_c_re, s5_c_im, s5_d, s5_glu_w, s5_glu_b, na_rpb, lru_conv_w,
           lru_conv_b, lru_gate_w, lru_gate_b, lru_lambda, w_branch, w_out, ln1_g, ln1_b, mlp_w1, mlp_b1, mlp_w2,
           mlp_b2, ln2_g, ln2_b):
    p = dict(w_in=w_in, dn_conv_w=dn_conv_w, dn_a_log=dn_a_log, dn_dt_bias=dn_dt_bias, dn_norm_g=dn_norm_g,
             s5_lambda_re=s5_lambda_re, s5_lambda_im=s5_lambda_im, s5_log_dt=s5_log_dt, s5_b_re=s5_b_re,
             s5_b_im=s5_b_im, s5_c_re=s5_c_re, s5_c_im=s5_c_im, s5_d=s5_d, s5_glu_w=s5_glu_w, s5_glu_b=s5_glu_b,
             na_rpb=na_rpb, lru_conv_w=lru_conv_w, lru_conv_b=lru_conv_b, lru_gate_w=lru_gate_w,
             lru_gate_b=lru_gate_b, lru_lambda=lru_lambda, w_branch=w_branch, w_out=w_out, ln1_g=ln1_g, ln1_b=ln1_b,
             mlp_w1=mlp_w1, mlp_b1=mlp_b1, mlp_w2=mlp_w2, mlp_b2=mlp_b2, ln2_g=ln2_g, ln2_b=ln2_b)
    layers = [_prepare_layer(l, p) for l in range(DEPTH)]
    return (_trunk(x_prompt, ln_in_g, ln_in_b, layers), _trunk(x_sample, ln_in_g, ln_in_b, layers))
```

```python
import functools
import math

import jax
import jax.numpy as jnp
import numpy as np
from jax import lax
from jax.experimental import pallas as pl
from jax.experimental.pallas import tpu as pltpu

F32 = jnp.float32
BF16 = jnp.bfloat16

D_MODEL = 1024
DEPTH = 2
GRID_W = 64
N_BRANCH = 4
BRANCH_W = 512
DN_HEADS = 4
DN_DK = 128
DN_DV = 128
DN_CONV = 4
DN_CHUNK = 64
S5_WIDTH = 512
S5_P = 16
S5_GROUPS = S5_WIDTH // S5_P
S5_N = 64
NA_HEADS = 8
NA_DH = 64
NA_WIDTH = NA_HEADS * NA_DH
NA_KH = 8
NA_KW = 16
LRU_WIDTH = 512
LRU_BLOCKS = 8
LRU_BW = LRU_WIDTH // LRU_BLOCKS
LRU_CONV = 4
LRU_C = 8.0
D_FF = 4 * D_MODEL
ALPHA = float((2 * DEPTH) ** 0.25)
EPS = 1e-5

_IN_SPLITS = (512, 512, 512, 512, 8, 8, 512, 512, 512, 512, 512, 512, 4096)
_IN_OFFS = tuple(sum(_IN_SPLITS[:i]) for i in range(len(_IN_SPLITS)))

LANE = 128
SUBLANE = 8
C_QKV = 0
C_Z = 1536
C_SU = 2048
C_LX = 2560
C_LG = 3072
C_AB = 3584
W_MAIN = 3712

S5_L = 32
S5_LW = S5_L * S5_P
S5_HW = 4 * LANE
S5_GPB = LANE // S5_P
S5_ROWS = 512

VMEM_LIMIT = 56 * 1024 * 1024


def _cparams(sem):
    return pltpu.CompilerParams(dimension_semantics=sem, vmem_limit_bytes=VMEM_LIMIT)


def _layer_norm(x, g, b):
    mu = jnp.mean(x, axis=-1, keepdims=True)
    xc = x - mu
    var = jnp.mean(xc * xc, axis=-1, keepdims=True)
    return xc * lax.rsqrt(var + EPS) * g + b


def _sigmoid(x):
    return 0.5 * jnp.tanh(0.5 * x) + 0.5


def _const_spec(shape):
    nd = len(shape)
    return pl.BlockSpec(shape, lambda *_: (0,) * nd, pipeline_mode=pl.Buffered(1))


def _project(x, wm_ref, wn_ref, hm_ref, na_ref):
    xb = x.astype(BF16)
    step = 4 * LANE
    for c0 in range(0, W_MAIN, step):
        c1 = min(c0 + step, W_MAIN)
        hm_ref[:, c0:c1] = jnp.dot(xb, wm_ref[:, c0:c1], preferred_element_type=F32)
    for c0 in range(0, 3 * NA_WIDTH, step):
        na_ref[:, c0:c0 + step] = jnp.dot(xb, wn_ref[:, c0:c0 + step], preferred_element_type=F32).astype(BF16)


def _proj_kernel(x_ref, wm_ref, wn_ref, hm_ref, na_ref):
    _project(x_ref[...], wm_ref, wn_ref, hm_ref, na_ref)


def _ln_proj_kernel(x_ref, g_ref, b_ref, wm_ref, wn_ref, xn_ref, hm_ref, na_ref):
    x = _layer_norm(x_ref[...], g_ref[...], b_ref[...])
    xn_ref[...] = x
    _project(x, wm_ref, wn_ref, hm_ref, na_ref)


def _in_proj(x, w_main, w_na, ln=None, tm=256):
    n = x.shape[0]
    tok = lambda w: pl.BlockSpec((tm, w), lambda i: (i, 0))
    w_specs = [_const_spec((D_MODEL, W_MAIN)), _const_spec((D_MODEL, 3 * NA_WIDTH))]
    out_specs = [tok(W_MAIN), tok(3 * NA_WIDTH)]
    out_shape = [jax.ShapeDtypeStruct((n, W_MAIN), F32), jax.ShapeDtypeStruct((n, 3 * NA_WIDTH), BF16)]
    if ln is None:
        return pl.pallas_call(
            _proj_kernel, grid=(n // tm,), in_specs=[tok(D_MODEL)] + w_specs, out_specs=out_specs,
            out_shape=out_shape, compiler_params=_cparams(("parallel",)), name="in_proj",
        )(x, w_main, w_na)
    return pl.pallas_call(
        _ln_proj_kernel, grid=(n // tm,),
        in_specs=[tok(D_MODEL), _const_spec((1, D_MODEL)), _const_spec((1, D_MODEL))] + w_specs,
        out_specs=[tok(D_MODEL)] + out_specs, out_shape=[jax.ShapeDtypeStruct((n, D_MODEL), F32)] + out_shape,
        compiler_params=_cparams(("parallel",)), name="ln_in_proj",
    )(x, ln[0].reshape(1, -1), ln[1].reshape(1, -1), w_main, w_na)


def _conv_centred(prev8, cur, next8, w, first, last):
    tb = cur.shape[0]
    prev8 = jnp.where(first, 0.0, prev8)
    next8 = jnp.where(last, 0.0, next8)
    xp = jnp.concatenate([prev8, cur, next8], axis=0)
    left = (w.shape[0] - 1) // 2
    acc = None
    for j in range(w.shape[0]):
        s = SUBLANE - left + j
        term = xp[s:s + tb] * w[j:j + 1]
        acc = term if acc is None else acc + term
    return acc


def _dot_bf(a, b):
    return jnp.dot(a.astype(BF16), b.astype(BF16), preferred_element_type=F32)


def _dot_nt_bf(a, b):
    return lax.dot_general(a.astype(BF16), b.astype(BF16), (((1,), (1,)), ((), ())), preferred_element_type=F32)


def _dn_pre_kernel(cur_ref, prev_ref, next_ref, cw_ref, o_ref):
    i = pl.program_id(1)
    nb = pl.num_programs(1)
    qkv = _conv_centred(prev_ref[...], cur_ref[...], next_ref[...], cw_ref[...], i == 0, i == nb - 1)
    qkv = qkv * _sigmoid(qkv)
    for h in range(DN_HEADS):
        q = qkv[:, h * DN_DK:(h + 1) * DN_DK]
        k = qkv[:, 512 + h * DN_DK:512 + (h + 1) * DN_DK]
        o_ref[:, h * DN_DK:(h + 1) * DN_DK] = q * (lax.rsqrt(jnp.sum(q * q, axis=-1, keepdims=True) + 1e-6)
                                                    * (DN_DK ** -0.5))
        o_ref[:, 512 + h * DN_DK:512 + (h + 1) * DN_DK] = k * lax.rsqrt(jnp.sum(k * k, axis=-1, keepdims=True) + 1e-6)
    o_ref[:, 1024:] = qkv[:, 1024:]


def _dn_pre(hm3, conv_w, tb=512):
    bsz, t, _ = hm3.shape
    nb = t // tb
    qkv_w = 3 * 512
    return pl.pallas_call(
        _dn_pre_kernel,
        grid=(bsz, nb),
        in_specs=list(_halo_specs(tb, qkv_w, 0, nb, t, False)) + [_const_spec((DN_CONV, qkv_w))],
        out_specs=pl.BlockSpec((None, tb, qkv_w), lambda b, i: (b, i, 0)),
        out_shape=jax.ShapeDtypeStruct((bsz, t, qkv_w), F32),
        compiler_params=_cparams(("parallel", "parallel")),
        name="deltanet_pre",
    )(hm3, hm3, hm3, conv_w)


def _dn_prepare(d, qkv, ab, alog, dtb):
    tb = qkv.shape[0]
    c = DN_CHUNK
    gates = -jnp.exp(alog) * jax.nn.softplus(ab + dtb)
    betas = _sigmoid(ab)

    row = lax.broadcasted_iota(jnp.int32, (c, c), 0)
    col = lax.broadcasted_iota(jnp.int32, (c, c), 1)
    incl = col <= row if d == 0 else col >= row
    strict = col < row if d == 0 else col > row
    eye = (row == col).astype(F32)
    edge = c - 1 if d == 0 else 0
    n_chunks = tb // c

    g_hi = gates.astype(BF16)
    rest = gates - g_hi.astype(F32)
    g_mid = rest.astype(BF16)
    g_lo = (rest - g_mid.astype(F32)).astype(BF16)
    g3 = jnp.concatenate([g_hi, g_mid, g_lo], axis=1)
    tri = incl.astype(BF16)
    gc_parts = [jnp.dot(tri, g3[ci * c:(ci + 1) * c], preferred_element_type=F32) for ci in range(n_chunks)]
    gc_all = jnp.concatenate([p[:, :LANE] + p[:, LANE:2 * LANE] + p[:, 2 * LANE:] for p in gc_parts], axis=0)

    units = [(ci, h) for ci in range(n_chunks) for h in range(DN_HEADS)]
    rows = lambda ci: slice(ci * c, (ci + 1) * c)
    lane_of = lambda h: d * DN_HEADS + h
    q_u = [qkv[rows(ci), h * DN_DK:(h + 1) * DN_DK] for ci, h in units]
    k_u = [qkv[rows(ci), 512 + h * DN_DK:512 + (h + 1) * DN_DK] for ci, h in units]
    v_u = [qkv[rows(ci), 1024 + h * DN_DV:1024 + (h + 1) * DN_DV] for ci, h in units]
    beta_u = [jnp.broadcast_to(betas[rows(ci), 8 + lane_of(h):9 + lane_of(h)], (c, DN_DK)) for ci, h in units]
    gcr_u = [jnp.broadcast_to(gc_all[rows(ci), lane_of(h):lane_of(h) + 1], (c, DN_DK)) for ci, h in units]
    gcl_u = [g.T[:c, :] for g in gcr_u]
    decay_u = [jnp.exp(jnp.where(incl, gr[:, :c] - gl, -1e30)) for gr, gl in zip(gcr_u, gcl_u)]
    egc_u = [jnp.exp(g) for g in gcr_u]
    glast_u = [g[edge:edge + 1, :] for g in gcr_u]
    kb_u = [k * b for k, b in zip(k_u, beta_u)]
    vb_u = [v * b for v, b in zip(v_u, beta_u)]
    kk_u = [_dot_nt_bf(jnp.concatenate([kb, q], axis=0), k) for kb, q, k in zip(kb_u, q_u, k_u)]
    x_u = [-jnp.where(strict, kk[:c] * dec, 0.0) for kk, dec in zip(kk_u, decay_u)]
    qk_u = [jnp.where(incl, kk[c:] * dec, 0.0).astype(BF16) for kk, dec in zip(kk_u, decay_u)]
    t_u = [eye + x for x in x_u]
    x_u = [_dot_bf(x, x) for x in x_u]
    for _ in range(4):
        both = [_dot_bf(jnp.concatenate([t, x], axis=0), x) for t, x in zip(t_u, x_u)]
        t_u = [t + b[:c] for t, b in zip(t_u, both)]
        x_u = [b[c:] for b in both]
    t_u = [t + _dot_bf(t, x) for t, x in zip(t_u, x_u)]
    sol_u = [_dot_bf(t, jnp.concatenate([vb, kb * e], axis=1)) for t, vb, kb, e in zip(t_u, vb_u, kb_u, egc_u)]
    u_u = [s[:, :DN_DV] for s in sol_u]
    wq_u = [jnp.concatenate([s[:, DN_DV:], q * e], axis=0).astype(BF16) for s, q, e in zip(sol_u, q_u, egc_u)]
    qkkd_u = [jnp.concatenate([qk, (k * jnp.exp(gl - g)).T.astype(BF16)], axis=0)
              for qk, k, gl, g in zip(qk_u, k_u, glast_u, gcr_u)]
    egl_u = [jnp.exp(gl) for gl in glast_u]
    return dict(zip(units, zip(u_u, wq_u, qkkd_u, egl_u)))


def _dn_kernel(cur_f, ab_f, cur_b, ab_b, alog_ref, dtb_ref, of_ref, ob_ref, s_ref):
    @pl.when(pl.program_id(1) == 0)
    def _():
        s_ref[...] = jnp.zeros_like(s_ref)

    alog = alog_ref[...]
    dtb = dtb_ref[...]
    prep = (_dn_prepare(0, cur_f[...], ab_f[...], alog, dtb), _dn_prepare(1, cur_b[...], ab_b[...], alog, dtb))
    c = DN_CHUNK
    n_chunks = cur_f.shape[0] // c
    o_refs = (of_ref, ob_ref)
    chains = [(d, h) for d in range(2) for h in range(DN_HEADS)]
    chunk_at = lambda d, step: step if d == 0 else n_chunks - 1 - step
    state = [s_ref[d * DN_HEADS + h] for d, h in chains]
    for step in range(n_chunks):
        ops = [prep[d][(chunk_at(d, step), h)] for d, h in chains]
        ws = [jnp.dot(wq, s.astype(BF16), preferred_element_type=F32) for (_, wq, _, _), s in zip(ops, state)]
        v_new = [(u - w[:c]).astype(BF16) for (u, _, _, _), w in zip(ops, ws)]
        upd = [jnp.dot(qkkd, vn, preferred_element_type=F32) for (_, _, qkkd, _), vn in zip(ops, v_new)]
        state = [s * egl + up[c:] for s, (_, _, _, egl), up in zip(state, ops, upd)]
        for (d, h), w, up in zip(chains, ws, upd):
            r0 = chunk_at(d, step) * c
            o_refs[d][r0:r0 + c, h * DN_DV:(h + 1) * DN_DV] = w[c:] + up[:c]
    for (d, h), s in zip(chains, state):
        s_ref[d * DN_HEADS + h] = s


def _halo_specs(tb, width, col_block, nb, t, reverse):
    per = tb // SUBLANE
    last8 = t // SUBLANE - 1
    if reverse:
        blk = lambda i: nb - 1 - i
    else:
        blk = lambda i: i
    cur = pl.BlockSpec((None, tb, width), lambda b, i: (b, blk(i), col_block))
    prev = pl.BlockSpec((None, SUBLANE, width), lambda b, i: (b, jnp.maximum(blk(i) * per - 1, 0), col_block))
    nxt = pl.BlockSpec((None, SUBLANE, width), lambda b, i: (b, jnp.minimum((blk(i) + 1) * per, last8), col_block))
    return cur, prev, nxt


def _deltanet(hm3, conv_w, a_log, dt_bias, tb=256):
    bsz, t, _ = hm3.shape
    nb = t // tb
    qkv_w = 3 * 512
    qkvn = _dn_pre(hm3, conv_w)
    alog_v = jnp.zeros((1, LANE), F32).at[0, :8].set(a_log.reshape(-1))
    dtb_v = jnp.zeros((1, LANE), F32).at[0, :8].set(dt_bias.reshape(-1))
    ab_col = C_AB // LANE
    fwd = lambda b, i: (b, i, 0)
    bwd = lambda b, i: (b, nb - 1 - i, 0)
    in_specs = [pl.BlockSpec((None, tb, qkv_w), fwd), pl.BlockSpec((None, tb, LANE), lambda b, i: (b, i, ab_col)),
                pl.BlockSpec((None, tb, qkv_w), bwd),
                pl.BlockSpec((None, tb, LANE), lambda b, i: (b, nb - 1 - i, ab_col)),
                _const_spec((1, LANE)), _const_spec((1, LANE))]
    out_specs = [pl.BlockSpec((None, tb, 512), fwd), pl.BlockSpec((None, tb, 512), bwd)]
    return pl.pallas_call(
        _dn_kernel,
        grid=(bsz, nb),
        in_specs=in_specs,
        out_specs=out_specs,
        out_shape=[jax.ShapeDtypeStruct((bsz, t, 512), F32)] * 2,
        scratch_shapes=[pltpu.VMEM((2 * DN_HEADS, DN_DK, DN_DV), F32)],
        compiler_params=_cparams(("parallel", "arbitrary")),
        name="deltanet",
    )(qkvn, hm3, qkvn, hm3, alog_v, dtb_v)


def _s5_discretise(lam_re, lam_im, log_dt, b_re, b_im):
    dt = jnp.exp(log_dt)[:, None]
    mag = jnp.exp(lam_re * dt)
    ab_re = mag * jnp.cos(lam_im * dt)
    ab_im = mag * jnp.sin(lam_im * dt)
    den = jnp.square(lam_re) + jnp.square(lam_im)
    nr, ni = ab_re - 1.0, ab_im
    kr = ((nr * lam_re + ni * lam_im) / den)[..., None]
    ki = ((ni * lam_re - nr * lam_im) / den)[..., None]
    return kr * b_re - ki * b_im, kr * b_im + ki * b_re


def _s5_operators(lam_re, lam_im, log_dt, b_re, b_im, c_re, c_im):
    hi = lax.Precision.HIGHEST
    L, G, N, P = S5_L, S5_GROUPS, S5_N, S5_P
    j = jnp.arange(L + 1, dtype=F32)[:, None, None]
    kcomb = 0.0
    e_cols, f_rows, al = [], [], []
    for d in range(2):
        dt = jnp.exp(log_dt[d])[:, None]
        bb_re, bb_im = _s5_discretise(lam_re[d], lam_im[d], log_dt[d], b_re, b_im)
        mag = jnp.exp(lam_re[d] * dt * j)
        ang = lam_im[d] * dt * j
        aj_re, aj_im = mag * jnp.cos(ang), mag * jnp.sin(ang)
        ca_re = c_re[None] * aj_re[:, :, None, :] - c_im[None] * aj_im[:, :, None, :]
        ca_im = c_re[None] * aj_im[:, :, None, :] + c_im[None] * aj_re[:, :, None, :]
        kj = (jnp.einsum('jgpn,gnq->jgpq', ca_re[:L], bb_re, precision=hi)
              - jnp.einsum('jgpn,gnq->jgpq', ca_im[:L], bb_im, precision=hi))
        zeros = jnp.zeros((L - 1,) + kj.shape[1:], F32)
        if d == 0:
            kcomb = kcomb + jnp.concatenate([zeros, kj], axis=0)
        else:
            kcomb = kcomb + jnp.concatenate([kj[::-1], zeros], axis=0)
        pw_re = aj_re[:L][::-1] if d == 0 else aj_re[:L]
        pw_im = aj_im[:L][::-1] if d == 0 else aj_im[:L]
        e_re = pw_re[..., None] * bb_re[None] - pw_im[..., None] * bb_im[None]
        e_im = pw_re[..., None] * bb_im[None] + pw_im[..., None] * bb_re[None]
        for e in (e_re, e_im):
            e = e.transpose(1, 0, 3, 2).reshape(G, L * P, N)
            e_cols.append(jnp.pad(e, ((0, 0), (0, 0), (0, LANE - N))))
        sel = slice(1, L + 1)
        fr = ca_re[sel] if d == 0 else ca_re[sel][::-1]
        fi = ca_im[sel] if d == 0 else ca_im[sel][::-1]
        for f in (fr, -fi):
            f = f.transpose(1, 3, 0, 2).reshape(G, N, L * P)
            f_rows.append(jnp.pad(f, ((0, 0), (0, LANE - N), (0, 0))))
        al += [jnp.pad(aj_re[L], ((0, 0), (0, LANE - N))), jnp.pad(aj_im[L], ((0, 0), (0, LANE - N)))]
    kc = kcomb.astype(BF16).transpose(1, 3, 0, 2).reshape(G, P, (2 * L - 1) * P)
    toep = jnp.stack([kc[:, :, (L - 1 - s) * P:(2 * L - 1 - s) * P] for s in range(L)], axis=1)
    toep = toep.reshape(G, L * P, L * P)
    we = jnp.concatenate(e_cols, axis=2).astype(BF16)
    w2 = jnp.concatenate(f_rows, axis=1).astype(BF16)
    return toep, we, w2, jnp.stack(al, axis=1)


def _s5_kernel(x_ref, wt_ref, we_ref, w2_ref, al_ref, y_ref, u_scr, yv_scr, hloc_ref, hin_ref, *, n_chunks, nbs):
    g8 = pl.program_id(2)
    rows = nbs * n_chunks
    blk = lax.broadcasted_iota(jnp.int32, (S5_GPB, LANE), 1) // S5_P
    lane_tiles = S5_L // S5_GPB
    groups = range(S5_GPB)

    def position(rg, s):
        return pl.ds(rg * (SUBLANE * S5_L) + s, SUBLANE, stride=S5_L)

    def regroup(vregs):
        m = list(vregs)
        dist = S5_GPB // 2
        while dist:
            upper = (blk & dist) != 0
            nxt = list(m)
            for v in groups:
                if not v & dist:
                    w = v + dist
                    nxt[v] = jnp.where(upper, pltpu.roll(m[w], dist * S5_P, axis=1), m[v])
                    nxt[w] = jnp.where(upper, m[w], pltpu.roll(m[v], LANE - dist * S5_P, axis=1))
            m = nxt
            dist //= 2
        return m

    @pl.when(g8 == 0)
    def _():
        def gather_rows(rg, _):
            rsl = pl.ds(pl.multiple_of(rg * SUBLANE, SUBLANE), SUBLANE)
            for k in range(lane_tiles):
                pieces = regroup([x_ref[position(rg, S5_GPB * k + j), :] for j in groups])
                for g in groups:
                    u_scr[g, rsl, k * LANE:(k + 1) * LANE] = pieces[g]
            return 0

        lax.fori_loop(0, rows // SUBLANE, gather_rows, 0, unroll=2)

    u = u_scr[g8].astype(BF16)
    y = jnp.dot(u, wt_ref[...], preferred_element_type=F32)
    hloc = jnp.dot(u, we_ref[...], preferred_element_type=F32)
    for part in range(4):
        hloc_ref[part] = hloc[:, part * LANE:(part + 1) * LANE]
    al = al_ref[...]
    a_re = (al[0:1], al[2:3])
    a_im = (al[1:2], al[3:4])

    def body(cidx, carry):
        new = []
        for d in range(2):
            cr, ci = carry[2 * d], carry[2 * d + 1]
            cc = cidx if d == 0 else n_chunks - 1 - cidx
            seqs = pl.ds(cc, nbs, stride=n_chunks)
            hin_ref[2 * d, seqs, :] = cr
            hin_ref[2 * d + 1, seqs, :] = ci
            lr = hloc_ref[2 * d, seqs, :]
            li = hloc_ref[2 * d + 1, seqs, :]
            new += [a_re[d] * cr - a_im[d] * ci + lr, a_re[d] * ci + a_im[d] * cr + li]
        return tuple(new)

    zero = jnp.zeros((nbs, LANE), F32)
    lax.fori_loop(0, n_chunks, body, (zero, zero, zero, zero))
    hin = jnp.concatenate([hin_ref[part] for part in range(4)], axis=1).astype(BF16)
    yv_scr[g8] = y + jnp.dot(hin, w2_ref[...], preferred_element_type=F32)

    @pl.when(g8 == S5_GPB - 1)
    def _():
        def scatter_rows(rg, _):
            rsl = pl.ds(pl.multiple_of(rg * SUBLANE, SUBLANE), SUBLANE)
            for k in range(lane_tiles):
                pieces = regroup([yv_scr[g, rsl, k * LANE:(k + 1) * LANE] for g in groups])
                for j in groups:
                    y_ref[position(rg, S5_GPB * k + j), :] = pieces[j]
            return 0

        lax.fori_loop(0, rows // SUBLANE, scatter_rows, 0, unroll=2)


def _s5(hm, t, wt, we, w2, al):
    n = hm.shape[0]
    n_chunks = t // S5_L
    nbs = S5_ROWS // n_chunks
    tokens = S5_ROWS * S5_L
    group = lambda q, s, g: (q * S5_GPB + g, 0, 0)
    return pl.pallas_call(
        functools.partial(_s5_kernel, n_chunks=n_chunks, nbs=nbs),
        grid=(S5_WIDTH // LANE, n // tokens, S5_GPB),
        in_specs=[pl.BlockSpec((tokens, LANE), lambda q, s, g: (s, C_SU // LANE + q), pipeline_mode=pl.Buffered(1)),
                  pl.BlockSpec((None, S5_LW, S5_LW), group), pl.BlockSpec((None, S5_LW, S5_HW), group),
                  pl.BlockSpec((None, S5_HW, S5_LW), group),
                  pl.BlockSpec((None, 4, LANE), group)],
        out_specs=pl.BlockSpec((tokens, LANE), lambda q, s, g: (s, q)),
        out_shape=jax.ShapeDtypeStruct((n, S5_WIDTH), F32),
        scratch_shapes=[pltpu.VMEM((S5_GPB, S5_ROWS, S5_LW), F32), pltpu.VMEM((S5_GPB, S5_ROWS, S5_LW), F32),
                        pltpu.VMEM((4, S5_ROWS, LANE), F32), pltpu.VMEM((4, S5_ROWS, LANE), F32)],
        compiler_params=_cparams(("arbitrary", "arbitrary", "arbitrary")),
        name="s5",
    )(hm, wt, we, w2, al)


def _na_bias_table(rpb):
    c = np.arange(GRID_W)
    c0 = np.clip(c - NA_KW // 2, 0, GRID_W - NA_KW)
    col_in = (c[None, :] >= c0[:, None]) & (c[None, :] < c0[:, None] + NA_KW)
    dc = np.clip(c[None, :] - c[:, None], -(NA_KW - 1), NA_KW - 1) + (NA_KW - 1)
    onehot = (dc[:, :, None] == np.arange(2 * NA_KW - 1)).astype(np.float32)
    by_col = jnp.einsum('hrm,qkm->hrqk', rpb.astype(F32), onehot, precision=lax.Precision.HIGHEST)
    tab = jnp.stack([by_col[:, NA_KH - 1 - dl:2 * NA_KH - 1 - dl] for dl in range(NA_KH)], axis=1)
    tab = jnp.where(col_in[None, None, None], tab, -1e30)
    return tab.transpose(0, 1, 3, 2, 4).reshape(NA_HEADS, NA_KH, GRID_W, NA_KH * GRID_W)


def _na_kernel(q_ref, k_ref, v_ref, tab_ref, o_ref, *, rows_per_step, n_rows):
    i = pl.program_id(2)
    w = GRID_W
    nk = NA_KH * w
    lane = lax.broadcasted_iota(jnp.int32, (2 * w, LANE), 1)
    rowi = lax.broadcasted_iota(jnp.int32, (2 * w, LANE), 0)
    own = (lane < NA_DH) == (rowi < w)
    low = lax.broadcasted_iota(jnp.int32, (w, LANE), 1) < NA_DH
    steps = range(rows_per_step)
    r = [i * rows_per_step + rr for rr in steps]
    r0 = [jnp.clip(x - NA_KH // 2, 0, n_rows - NA_KH) for x in r]
    krows = [pl.ds(pl.multiple_of(x * w, w), nk) for x in r0]
    q2 = [q_ref[rr * w:(rr + 1) * w, :] for rr in steps]
    q2 = [jnp.where(own, jnp.concatenate([q, q], axis=0), jnp.zeros((2 * w, LANE), q.dtype)) for q in q2]
    s = [lax.dot_general(q, k_ref[kr, :], (((1,), (1,)), ((), ())), preferred_element_type=F32)
         for q, kr in zip(q2, krows)]
    s = [jnp.concatenate([x[:w] + tab_ref[0, a - b], x[w:] + tab_ref[1, a - b]], axis=0) for x, a, b in zip(s, r, r0)]
    m = [jnp.max(x, axis=-1, keepdims=True) for x in s]
    p = [jnp.exp(x - y) for x, y in zip(s, m)]
    l = [jnp.sum(x, axis=-1, keepdims=True) for x in p]
    pv = [jnp.dot(x.astype(BF16), v_ref[kr, :], preferred_element_type=F32) / y for x, kr, y in zip(p, krows, l)]
    for rr, x in zip(steps, pv):
        o_ref[rr * w:(rr + 1) * w, :] = jnp.where(low, x[:w], x[w:]).astype(o_ref.dtype)


def _natten(na3, table, rows_per_step=16):
    bsz, t, _ = na3.shape
    n_rows = t // GRID_W
    tq = rows_per_step * GRID_W
    pairs = NA_HEADS // 2
    kcol = NA_WIDTH // LANE
    return pl.pallas_call(
        functools.partial(_na_kernel, rows_per_step=rows_per_step, n_rows=n_rows),
        grid=(pairs, bsz, t // tq),
        in_specs=[pl.BlockSpec((None, tq, LANE), lambda p, b, i: (b, i, p)),
                  pl.BlockSpec((None, t, LANE), lambda p, b, i: (b, 0, kcol + p)),
                  pl.BlockSpec((None, t, LANE), lambda p, b, i: (b, 0, 2 * kcol + p)),
                  pl.BlockSpec((2, NA_KH, GRID_W, NA_KH * GRID_W), lambda p, b, i: (p, 0, 0, 0))],
        out_specs=pl.BlockSpec((None, tq, LANE), lambda p, b, i: (b, i, p)),
        out_shape=jax.ShapeDtypeStruct((bsz, t, NA_WIDTH), BF16),
        compiler_params=_cparams(("parallel", "parallel", "parallel")),
        name="natten",
    )(na3, na3, na3, table)


def _lru_coefficients(d, cur, prev8, next8, first, last, cw, cb, wg_ref, gb, sp_lam, a_scr, b_scr):
    xc = _conv_centred(prev8, cur, next8, cw, first, last) + cb
    width = 2 * LRU_WIDTH
    gates = jnp.dot(xc.astype(BF16), wg_ref[:, d * width:(d + 1) * width], preferred_element_type=F32)
    gates = _sigmoid(gates + gb[:, d * width:(d + 1) * width])
    log_a = -LRU_C * gates[:, :LRU_WIDTH] * sp_lam[d:d + 1]
    a = jnp.exp(log_a)
    a_scr[d] = a
    b_scr[d] = jnp.sqrt(1.0 - a * a) * gates[:, LRU_WIDTH:] * xc


def _lru_scan(a_scr, b_scr, carry_ref, o_refs):
    n_groups = a_scr.shape[1] // SUBLANE
    sub = lax.broadcasted_iota(jnp.int32, (SUBLANE, LRU_WIDTH), 0)

    def body(gi, hs):
        new = []
        for d in range(2):
            h = hs[d]
            grp = gi if d == 0 else n_groups - 1 - gi
            rows = pl.ds(pl.multiple_of(grp * SUBLANE, SUBLANE), SUBLANE)
            a8 = a_scr[d, rows, :]
            b8 = b_scr[d, rows, :]
            for s in (1, 2, 4):
                shift = s if d == 0 else SUBLANE - s
                has_prev = sub >= s if d == 0 else sub < SUBLANE - s
                a_prev = pltpu.roll(a8, shift, axis=0)
                b_prev = pltpu.roll(b8, shift, axis=0)
                b8 = jnp.where(has_prev, a8 * b_prev + b8, b8)
                a8 = jnp.where(has_prev, a8 * a_prev, a8)
            out = a8 * h + b8
            o_refs[d][rows, :] = out
            last = SUBLANE - 1 if d == 0 else 0
            new.append(out[last:last + 1])
        return tuple(new)

    hf, hb = lax.fori_loop(0, n_groups, body, (carry_ref[0:1], carry_ref[1:2]))
    carry_ref[0:1] = hf
    carry_ref[1:2] = hb


def _lru_kernel(cur_f, prev_f, next_f, cur_b, prev_b, next_b, cw_ref, cb_ref, wg_ref, gb_ref, lam_ref,
                hf_ref, hb_ref, a_scr, b_scr, carry_ref):
    i = pl.program_id(1)
    nb = pl.num_programs(1)

    @pl.when(i == 0)
    def _():
        carry_ref[...] = jnp.zeros_like(carry_ref)

    cw = cw_ref[...]
    cb = cb_ref[...]
    gb = gb_ref[...]
    sp_lam = jax.nn.softplus(-lam_ref[...])
    _lru_coefficients(0, cur_f[...], prev_f[...], next_f[...], i == 0, i == nb - 1, cw, cb, wg_ref, gb, sp_lam,
                      a_scr, b_scr)
    _lru_coefficients(1, cur_b[...], prev_b[...], next_b[...], i == nb - 1, i == 0, cw, cb, wg_ref, gb, sp_lam,
                      a_scr, b_scr)
    _lru_scan(a_scr, b_scr, carry_ref, (hf_ref, hb_ref))


def _lru_gate_matrix(gate_w):
    eye = jnp.eye(LRU_BLOCKS, dtype=gate_w.dtype)
    full = jnp.einsum('dgncm,nk->ncdgkm', gate_w, eye)
    return full.reshape(LRU_WIDTH, 4 * LRU_WIDTH)


def _rglru(hm3, conv_w, conv_b, wg, gate_b, lam, tb=512):
    bsz, t, _ = hm3.shape
    nb = t // tb
    col = C_LX // LRU_WIDTH
    in_specs = list(_halo_specs(tb, LRU_WIDTH, col, nb, t, False)) + list(_halo_specs(tb, LRU_WIDTH, col, nb, t, True))
    in_specs += [_const_spec((LRU_CONV, LRU_WIDTH)), _const_spec((1, LRU_WIDTH)),
                 _const_spec((LRU_WIDTH, 4 * LRU_WIDTH)), _const_spec((1, 4 * LRU_WIDTH)), _const_spec((2, LRU_WIDTH))]
    out_specs = [pl.BlockSpec((None, tb, LRU_WIDTH), lambda b, i: (b, i, 0)),
                 pl.BlockSpec((None, tb, LRU_WIDTH), lambda b, i: (b, nb - 1 - i, 0))]
    return pl.pallas_call(
        _lru_kernel,
        grid=(bsz, nb),
        in_specs=in_specs,
        out_specs=out_specs,
        out_shape=[jax.ShapeDtypeStruct((bsz, t, LRU_WIDTH), F32)] * 2,
        scratch_shapes=[pltpu.VMEM((2, tb, LRU_WIDTH), F32), pltpu.VMEM((2, tb, LRU_WIDTH), F32),
                        pltpu.VMEM((2, LRU_WIDTH), F32)],
        compiler_params=_cparams(("parallel", "arbitrary")),
        name="rglru",
    )(hm3, hm3, hm3, hm3, hm3, hm3, conv_w, conv_b.reshape(1, -1), wg, gate_b.reshape(1, -1), lam)


def _merge_kernel(x_ref, of_ref, ob_ref, z_ref, ys_ref, su_ref, na_ref, hf_ref, hb_ref, lg_ref,
                  ng_ref, sd_ref, gw_ref, gbias_ref, wgt_ref, wbr_ref, wout_ref, lng_ref, lnb_ref, o_ref):
    x = x_ref[...]
    xb = x.astype(BF16)
    o = of_ref[...] + ob_ref[...]
    z = z_ref[...]
    parts = []
    for h in range(DN_HEADS):
        oh = o[:, h * DN_DV:(h + 1) * DN_DV]
        ms = jnp.mean(oh * oh, axis=-1, keepdims=True)
        parts.append(oh * lax.rsqrt(ms + EPS) * ng_ref[...])
    y_a = jnp.concatenate(parts, axis=1) * (z * _sigmoid(z))
    y = jax.nn.gelu(ys_ref[...] + sd_ref[...] * su_ref[...])
    y_b = y * _sigmoid(jnp.dot(y.astype(BF16), gw_ref[...], preferred_element_type=F32) + gbias_ref[...])
    y_d = (hf_ref[...] + hb_ref[...]) * jax.nn.gelu(lg_ref[...])
    ys = (y_a.astype(BF16), y_b.astype(BF16), na_ref[...], y_d.astype(BF16))
    acc = None
    for n in range(N_BRANCH):
        gate = _sigmoid(jnp.dot(xb, wgt_ref[:, n * D_MODEL:(n + 1) * D_MODEL], preferred_element_type=F32))
        term = gate * jnp.dot(ys[n], wbr_ref[n], preferred_element_type=F32)
        acc = term if acc is None else acc + term
    mix = jnp.dot(acc.astype(BF16), wout_ref[...], preferred_element_type=F32)
    o_ref[...] = _layer_norm(ALPHA * x + mix, lng_ref[...], lnb_ref[...])


def _merge(x, o_f, o_b, hm, y_s5, na_o, h_f, h_b, norm_g, s5_d, glu_w, glu_b, w_gate, w_branch, w_out, ln_g, ln_b,
           tm=256):
    n = x.shape[0]
    tok = lambda w, cb=0: pl.BlockSpec((tm, w), lambda i: (i, cb))
    in_specs = [tok(D_MODEL), tok(512), tok(512), tok(512, C_Z // 512), tok(512), tok(512, C_SU // 512), tok(512),
                tok(512), tok(512), tok(512, C_LG // 512),
                _const_spec((1, DN_DV)), _const_spec((1, S5_WIDTH)), _const_spec((S5_WIDTH, S5_WIDTH)),
                _const_spec((1, S5_WIDTH)), _const_spec((D_MODEL, N_BRANCH * D_MODEL)),
                _const_spec((N_BRANCH, BRANCH_W, D_MODEL)), _const_spec((D_MODEL, D_MODEL)),
                _const_spec((1, D_MODEL)), _const_spec((1, D_MODEL))]
    return pl.pallas_call(
        _merge_kernel,
        grid=(n // tm,),
        in_specs=in_specs,
        out_specs=pl.BlockSpec((tm, D_MODEL), lambda i: (i, 0)),
        out_shape=jax.ShapeDtypeStruct((n, D_MODEL), F32),
        compiler_params=_cparams(("parallel",)),
        name="merge",
    )(x, o_f, o_b, hm, y_s5, hm, na_o, h_f, h_b, hm, norm_g.reshape(1, -1), s5_d.reshape(1, -1), glu_w,
      glu_b.reshape(1, -1), w_gate, w_branch, w_out, ln_g.reshape(1, -1), ln_b.reshape(1, -1))


def _mlp_kernel(x_ref, w1_ref, b1_ref, w2_ref, b2_ref, g_ref, b_ref, o_ref):
    x = x_ref[...]
    xb = x.astype(BF16)
    acc = None
    for c0 in range(0, D_FF, D_MODEL):
        f = jnp.dot(xb, w1_ref[:, c0:c0 + D_MODEL], preferred_element_type=F32) + b1_ref[:, c0:c0 + D_MODEL]
        f = jnp.square(jnp.maximum(f, 0.0))
        term = jnp.dot(f.astype(BF16), w2_ref[c0:c0 + D_MODEL, :], preferred_element_type=F32)
        acc = term if acc is None else acc + term
    o_ref[...] = _layer_norm(ALPHA * x + acc + b2_ref[...], g_ref[...], b_ref[...])


def _mlp(x, w1, b1, w2, b2, g, b, tm=512):
    n = x.shape[0]
    return pl.pallas_call(
        _mlp_kernel,
        grid=(n // tm,),
        in_specs=[pl.BlockSpec((tm, D_MODEL), lambda i: (i, 0)), _const_spec((D_MODEL, D_FF)), _const_spec((1, D_FF)),
                  _const_spec((D_FF, D_MODEL)), _const_spec((1, D_MODEL)), _const_spec((1, D_MODEL)),
                  _const_spec((1, D_MODEL))],
        out_specs=pl.BlockSpec((tm, D_MODEL), lambda i: (i, 0)),
        out_shape=jax.ShapeDtypeStruct((n, D_MODEL), F32),
        compiler_params=_cparams(("parallel",)),
        name="mlp",
    )(x, w1, b1.reshape(1, -1), w2, b2.reshape(1, -1), g.reshape(1, -1), b.reshape(1, -1))


def _prepare_layer(l, p):
    w_in = p['w_in'][l]
    col = lambda i: w_in[:, _IN_OFFS[i]:_IN_OFFS[i] + _IN_SPLITS[i]]
    pad = jnp.zeros((D_MODEL, W_MAIN - C_AB - 16), F32)
    w_main = jnp.concatenate([col(0), col(1), col(2), col(3), col(6), col(10), col(11), col(4), col(5), pad], axis=1)
    w_na = jnp.concatenate([col(7) * (NA_DH ** -0.5), col(8), col(9)], axis=1)
    s5_wt, s5_we, s5_w2, s5_al = _s5_operators(p['s5_lambda_re'][l], p['s5_lambda_im'][l], p['s5_log_dt'][l],
                                        p['s5_b_re'][l], p['s5_b_im'][l], p['s5_c_re'][l], p['s5_c_im'][l])
    return dict(
        w_main=w_main.astype(BF16), w_na=w_na.astype(BF16), w_gate=col(12).astype(BF16),
        dn_conv_w=p['dn_conv_w'][l], dn_a_log=p['dn_a_log'][l], dn_dt_bias=p['dn_dt_bias'][l],
        dn_norm_g=p['dn_norm_g'][l],
        s5_wt=s5_wt, s5_we=s5_we, s5_w2=s5_w2, s5_al=s5_al, s5_d=p['s5_d'][l], glu_w=p['s5_glu_w'][l].astype(BF16),
        glu_b=p['s5_glu_b'][l],
        na_table=_na_bias_table(p['na_rpb'][l]),
        lru_conv_w=p['lru_conv_w'][l], lru_conv_b=p['lru_conv_b'][l],
        lru_wg=_lru_gate_matrix(p['lru_gate_w'][l]).astype(BF16), lru_gate_b=p['lru_gate_b'][l],
        lru_lambda=p['lru_lambda'][l],
        w_branch=p['w_branch'][l].astype(BF16), w_out=p['w_out'][l].astype(BF16),
        ln1_g=p['ln1_g'][l], ln1_b=p['ln1_b'][l],
        mlp_w1=p['mlp_w1'][l].astype(BF16), mlp_b1=p['mlp_b1'][l], mlp_w2=p['mlp_w2'][l].astype(BF16),
        mlp_b2=p['mlp_b2'][l], ln2_g=p['ln2_g'][l], ln2_b=p['ln2_b'][l])


def _layer(x, bsz, t, lw, ln_in=None):
    if ln_in is None:
        hm, na = _in_proj(x, lw['w_main'], lw['w_na'])
    else:
        x, hm, na = _in_proj(x, lw['w_main'], lw['w_na'], ln=ln_in)
    hm3 = hm.reshape(bsz, t, W_MAIN)
    o_f, o_b = _deltanet(hm3, lw['dn_conv_w'], lw['dn_a_log'], lw['dn_dt_bias'])
    y_s5 = _s5(hm, t, lw['s5_wt'], lw['s5_we'], lw['s5_w2'], lw['s5_al'])
    na_o = _natten(na.reshape(bsz, t, 3 * NA_WIDTH), lw['na_table'])
    h_f, h_b = _rglru(hm3, lw['lru_conv_w'], lw['lru_conv_b'], lw['lru_wg'], lw['lru_gate_b'], lw['lru_lambda'])
    n = bsz * t
    x1 = _merge(x, o_f.reshape(n, 512), o_b.reshape(n, 512), hm, y_s5, na_o.reshape(n, NA_WIDTH),
                h_f.reshape(n, LRU_WIDTH), h_b.reshape(n, LRU_WIDTH), lw['dn_norm_g'], lw['s5_d'], lw['glu_w'],
                lw['glu_b'], lw['w_gate'], lw['w_branch'], lw['w_out'], lw['ln1_g'], lw['ln1_b'])
    return _mlp(x1, lw['mlp_w1'], lw['mlp_b1'], lw['mlp_w2'], lw['mlp_b2'], lw['ln2_g'], lw['ln2_b'])


def _trunk(x, ln_g, ln_b, layers):
    bsz, t, _ = x.shape
    h = x.reshape(bsz * t, D_MODEL)
    for l, lw in enumerate(layers):
        h = _layer(h, bsz, t, lw, ln_in=(ln_g, ln_b) if l == 0 else None)
    return h.reshape(bsz, t, D_MODEL)


def kernel(x_prompt, x_sample, ln_in_g, ln_in_b, w_in, dn_conv_w, dn_a_log, dn_dt_bias, dn_norm_g, s5_lambda_re,
           s5_lambda_im, s5_log_dt, s5_b_re, s5_b_im, s5_c_re, s5_c_im, s5_d, s5_glu_w, s5_glu_b, na_rpb, lru_conv_w,
           lru_conv_b, lru_gate_w, lru_gate_b, lru_lambda, w_branch, w_out, ln1_g, ln1_b, mlp_w1, mlp_b1, mlp_w2,
           mlp_b2, ln2_g, ln2_b):
    p = dict(w_in=w_in, dn_conv_w=dn_conv_w, dn_a_log=dn_a_log, dn_dt_bias=dn_dt_bias, dn_norm_g=dn_norm_g,
             s5_lambda_re=s5_lambda_re, s5_lambda_im=s5_lambda_im, s5_log_dt=s5_log_dt, s5_b_re=s5_b_re,
             s5_b_im=s5_b_im, s5_c_re=s5_c_re, s5_c_im=s5_c_im, s5_d=s5_d, s5_glu_w=s5_glu_w, s5_glu_b=s5_glu_b,
             na_rpb=na_rpb, lru_conv_w=lru_conv_w, lru_conv_b=lru_conv_b, lru_gate_w=lru_gate_w,
             lru_gate_b=lru_gate_b, lru_lambda=lru_lambda, w_branch=w_branch, w_out=w_out, ln1_g=ln1_g, ln1_b=ln1_b,
             mlp_w1=mlp_w1, mlp_b1=mlp_b1, mlp_w2=mlp_w2, mlp_b2=mlp_b2, ln2_g=ln2_g, ln2_b=ln2_b)
    layers = [_prepare_layer(l, p) for l in range(DEPTH)]
    return (_trunk(x_prompt, ln_in_g, ln_in_b, layers), _trunk(x_sample, ln_in_g, ln_in_b, layers))
```

```python
import functools
import math

import jax
import jax.numpy as jnp
import numpy as np
from jax import lax
from jax.experimental import pallas as pl
from jax.experimental.pallas import tpu as pltpu

F32 = jnp.float32
BF16 = jnp.bfloat16

D_MODEL = 1024
DEPTH = 2
GRID_W = 64
N_BRANCH = 4
BRANCH_W = 512
DN_HEADS = 4
DN_DK = 128
DN_DV = 128
DN_CONV = 4
DN_CHUNK = 64
S5_WIDTH = 512
S5_P = 16
S5_GROUPS = S5_WIDTH // S5_P
S5_N = 64
NA_HEADS = 8
NA_DH = 64
NA_WIDTH = NA_HEADS * NA_DH
NA_KH = 8
NA_KW = 16
LRU_WIDTH = 512
LRU_BLOCKS = 8
LRU_BW = LRU_WIDTH // LRU_BLOCKS
LRU_CONV = 4
LRU_C = 8.0
D_FF = 4 * D_MODEL
ALPHA = float((2 * DEPTH) ** 0.25)
EPS = 1e-5

_IN_SPLITS = (512, 512, 512, 512, 8, 8, 512, 512, 512, 512, 512, 512, 4096)
_IN_OFFS = tuple(sum(_IN_SPLITS[:i]) for i in range(len(_IN_SPLITS)))

LANE = 128
SUBLANE = 8
C_QKV = 0
C_Z = 1536
C_SU = 2048
C_LX = 2560
C_LG = 3072
C_AB = 3584
W_MAIN = 3712

S5_L = 32
S5_LW = S5_L * S5_P
S5_HW = 4 * LANE
S5_GPB = LANE // S5_P
S5_ROWS = 512

VMEM_LIMIT = 56 * 1024 * 1024


def _cparams(sem):
    return pltpu.CompilerParams(dimension_semantics=sem, vmem_limit_bytes=VMEM_LIMIT)


def _layer_norm(x, g, b):
    mu = jnp.mean(x, axis=-1, keepdims=True)
    xc = x - mu
    var = jnp.mean(xc * xc, axis=-1, keepdims=True)
    return xc * lax.rsqrt(var + EPS) * g + b


def _sigmoid(x):
    return 0.5 * jnp.tanh(0.5 * x) + 0.5


def _const_spec(shape):
    nd = len(shape)
    return pl.BlockSpec(shape, lambda *_: (0,) * nd, pipeline_mode=pl.Buffered(1))


def _project(x, wm_ref, wn_ref, hm_ref, na_ref):
    xb = x.astype(BF16)
    step = 4 * LANE
    for c0 in range(0, W_MAIN, step):
        c1 = min(c0 + step, W_MAIN)
        hm_ref[:, c0:c1] = jnp.dot(xb, wm_ref[:, c0:c1], preferred_element_type=F32)
    for c0 in range(0, 3 * NA_WIDTH, step):
        na_ref[:, c0:c0 + step] = jnp.dot(xb, wn_ref[:, c0:c0 + step], preferred_element_type=F32).astype(BF16)


def _proj_kernel(x_ref, wm_ref, wn_ref, hm_ref, na_ref):
    _project(x_ref[...], wm_ref, wn_ref, hm_ref, na_ref)


def _ln_proj_kernel(x_ref, g_ref, b_ref, wm_ref, wn_ref, xn_ref, hm_ref, na_ref):
    x = _layer_norm(x_ref[...], g_ref[...], b_ref[...])
    xn_ref[...] = x
    _project(x, wm_ref, wn_ref, hm_ref, na_ref)


def _in_proj(x, w_main, w_na, ln=None, tm=256):
    n = x.shape[0]
    tok = lambda w: pl.BlockSpec((tm, w), lambda i: (i, 0))
    w_specs = [_const_spec((D_MODEL, W_MAIN)), _const_spec((D_MODEL, 3 * NA_WIDTH))]
    out_specs = [tok(W_MAIN), tok(3 * NA_WIDTH)]
    out_shape = [jax.ShapeDtypeStruct((n, W_MAIN), F32), jax.ShapeDtypeStruct((n, 3 * NA_WIDTH), BF16)]
    if ln is None:
        return pl.pallas_call(
            _proj_kernel, grid=(n // tm,), in_specs=[tok(D_MODEL)] + w_specs, out_specs=out_specs,
            out_shape=out_shape, compiler_params=_cparams(("parallel",)), name="in_proj",
        )(x, w_main, w_na)
    return pl.pallas_call(
        _ln_proj_kernel, grid=(n // tm,),
        in_specs=[tok(D_MODEL), _const_spec((1, D_MODEL)), _const_spec((1, D_MODEL))] + w_specs,
        out_specs=[tok(D_MODEL)] + out_specs, out_shape=[jax.ShapeDtypeStruct((n, D_MODEL), F32)] + out_shape,
        compiler_params=_cparams(("parallel",)), name="ln_in_proj",
    )(x, ln[0].reshape(1, -1), ln[1].reshape(1, -1), w_main, w_na)


def _conv_centred(prev8, cur, next8, w, first, last):
    tb = cur.shape[0]
    prev8 = jnp.where(first, 0.0, prev8)
    next8 = jnp.where(last, 0.0, next8)
    xp = jnp.concatenate([prev8, cur, next8], axis=0)
    left = (w.shape[0] - 1) // 2
    acc = None
    for j in range(w.shape[0]):
        s = SUBLANE - left + j
        term = xp[s:s + tb] * w[j:j + 1]
        acc = term if acc is None else acc + term
    return acc


def _dot_bf(a, b):
    return jnp.dot(a.astype(BF16), b.astype(BF16), preferred_element_type=F32)


def _dot_nt_bf(a, b):
    return lax.dot_general(a.astype(BF16), b.astype(BF16), (((1,), (1,)), ((), ())), preferred_element_type=F32)


def _dn_pre_kernel(cur_ref, prev_ref, next_ref, cw_ref, o_ref):
    i = pl.program_id(1)
    nb = pl.num_programs(1)
    qkv = _conv_centred(prev_ref[...], cur_ref[...], next_ref[...], cw_ref[...], i == 0, i == nb - 1)
    qkv = qkv * _sigmoid(qkv)
    for h in range(DN_HEADS):
        q = qkv[:, h * DN_DK:(h + 1) * DN_DK]
        k = qkv[:, 512 + h * DN_DK:512 + (h + 1) * DN_DK]
        o_ref[:, h * DN_DK:(h + 1) * DN_DK] = q * (lax.rsqrt(jnp.sum(q * q, axis=-1, keepdims=True) + 1e-6)
                                                    * (DN_DK ** -0.5))
        o_ref[:, 512 + h * DN_DK:512 + (h + 1) * DN_DK] = k * lax.rsqrt(jnp.sum(k * k, axis=-1, keepdims=True) + 1e-6)
    o_ref[:, 1024:] = qkv[:, 1024:]


def _dn_pre(hm3, conv_w, tb=512):
    bsz, t, _ = hm3.shape
    nb = t // tb
    qkv_w = 3 * 512
    return pl.pallas_call(
        _dn_pre_kernel,
        grid=(bsz, nb),
        in_specs=list(_halo_specs(tb, qkv_w, 0, nb, t, False)) + [_const_spec((DN_CONV, qkv_w))],
        out_specs=pl.BlockSpec((None, tb, qkv_w), lambda b, i: (b, i, 0)),
        out_shape=jax.ShapeDtypeStruct((bsz, t, qkv_w), F32),
        compiler_params=_cparams(("parallel", "parallel")),
        name="deltanet_pre",
    )(hm3, hm3, hm3, conv_w)


def _dn_prepare(d, qkv, ab, alog, dtb):
    tb = qkv.shape[0]
    c = DN_CHUNK
    gates = -jnp.exp(alog) * jax.nn.softplus(ab + dtb)
    betas = _sigmoid(ab)

    row = lax.broadcasted_iota(jnp.int32, (c, c), 0)
    col = lax.broadcasted_iota(jnp.int32, (c, c), 1)
    incl = col <= row if d == 0 else col >= row
    strict = col < row if d == 0 else col > row
    eye = (row == col).astype(F32)
    edge = c - 1 if d == 0 else 0
    n_chunks = tb // c

    g_hi = gates.astype(BF16)
    rest = gates - g_hi.astype(F32)
    g_mid = rest.astype(BF16)
    g_lo = (rest - g_mid.astype(F32)).astype(BF16)
    g3 = jnp.concatenate([g_hi, g_mid, g_lo], axis=1)
    tri = incl.astype(BF16)
    gc_parts = [jnp.dot(tri, g3[ci * c:(ci + 1) * c], preferred_element_type=F32) for ci in range(n_chunks)]
    gc_all = jnp.concatenate([p[:, :LANE] + p[:, LANE:2 * LANE] + p[:, 2 * LANE:] for p in gc_parts], axis=0)

    units = [(ci, h) for ci in range(n_chunks) for h in range(DN_HEADS)]
    rows = lambda ci: slice(ci * c, (ci + 1) * c)
    lane_of = lambda h: d * DN_HEADS + h
    q_u = [qkv[rows(ci), h * DN_DK:(h + 1) * DN_DK] for ci, h in units]
    k_u = [qkv[rows(ci), 512 + h * DN_DK:512 + (h + 1) * DN_DK] for ci, h in units]
    v_u = [qkv[rows(ci), 1024 + h * DN_DV:1024 + (h + 1) * DN_DV] for ci, h in units]
    beta_u = [jnp.broadcast_to(betas[rows(ci), 8 + lane_of(h):9 + lane_of(h)], (c, DN_DK)) for ci, h in units]
    gcr_u = [jnp.broadcast_to(gc_all[rows(ci), lane_of(h):lane_of(h) + 1], (c, DN_DK)) for ci, h in units]
    gcl_u = [g.T[:c, :] for g in gcr_u]
    decay_u = [jnp.exp(jnp.where(incl, gr[:, :c] - gl, -1e30)) for gr, gl in zip(gcr_u, gcl_u)]
    egc_u = [jnp.exp(g) for g in gcr_u]
    glast_u = [g[edge:edge + 1, :] for g in gcr_u]
    kb_u = [k * b for k, b in zip(k_u, beta_u)]
    vb_u = [v * b for v, b in zip(v_u, beta_u)]
    kk_u = [_dot_nt_bf(jnp.concatenate([kb, q], axis=0), k) for kb, q, k in zip(kb_u, q_u, k_u)]
    x_u = [-jnp.where(strict, kk[:c] * dec, 0.0) for kk, dec in zip(kk_u, decay_u)]
    qk_u = [jnp.where(incl, kk[c:] * dec, 0.0).astype(BF16) for kk, dec in zip(kk_u, decay_u)]
    t_u = [eye + x for x in x_u]
    x_u = [_dot_bf(x, x) for x in x_u]
    for _ in range(4):
        both = [_dot_bf(jnp.concatenate([t, x], axis=0), x) for t, x in zip(t_u, x_u)]
        t_u = [t + b[:c] for t, b in zip(t_u, both)]
        x_u = [b[c:] for b in both]
    t_u = [t + _dot_bf(t, x) for t, x in zip(t_u, x_u)]
    sol_u = [_dot_bf(t, jnp.concatenate([vb, kb * e], axis=1)) for t, vb, kb, e in zip(t_u, vb_u, kb_u, egc_u)]
    u_u = [s[:, :DN_DV] for s in sol_u]
    wq_u = [jnp.concatenate([s[:, DN_DV:], q * e], axis=0).astype(BF16) for s, q, e in zip(sol_u, q_u, egc_u)]
    qkkd_u = [jnp.concatenate([qk, (k * jnp.exp(gl - g)).T.astype(BF16)], axis=0)
              for qk, k, gl, g in zip(qk_u, k_u, glast_u, gcr_u)]
    egl_u = [jnp.exp(gl) for gl in glast_u]
    return dict(zip(units, zip(u_u, wq_u, qkkd_u, egl_u)))


def _dn_kernel(cur_f, ab_f, cur_b, ab_b, alog_ref, dtb_ref, of_ref, ob_ref, s_ref):
    @pl.when(pl.program_id(1) == 0)
    def _():
        s_ref[...] = jnp.zeros_like(s_ref)

    alog = alog_ref[...]
    dtb = dtb_ref[...]
    prep = (_dn_prepare(0, cur_f[...], ab_f[...], alog, dtb), _dn_prepare(1, cur_b[...], ab_b[...], alog, dtb))
    c = DN_CHUNK
    n_chunks = cur_f.shape[0] // c
    o_refs = (of_ref, ob_ref)
    chains = [(d, h) for d in range(2) for h in range(DN_HEADS)]
    chunk_at = lambda d, step: step if d == 0 else n_chunks - 1 - step
    state = [s_ref[d * DN_HEADS + h] for d, h in chains]
    for step in range(n_chunks):
        ops = [prep[d][(chunk_at(d, step), h)] for d, h in chains]
        ws = [jnp.dot(wq, s.astype(BF16), preferred_element_type=F32) for (_, wq, _, _), s in zip(ops, state)]
        v_new = [(u - w[:c]).astype(BF16) for (u, _, _, _), w in zip(ops, ws)]
        upd = [jnp.dot(qkkd, vn, preferred_element_type=F32) for (_, _, qkkd, _), vn in zip(ops, v_new)]
        state = [s * egl + up[c:] for s, (_, _, _, egl), up in zip(state, ops, upd)]
        for (d, h), w, up in zip(chains, ws, upd):
            r0 = chunk_at(d, step) * c
            o_refs[d][r0:r0 + c, h * DN_DV:(h + 1) * DN_DV] = w[c:] + up[:c]
    for (d, h), s in zip(chains, state):
        s_ref[d * DN_HEADS + h] = s


def _halo_specs(tb, width, col_block, nb, t, reverse):
    per = tb // SUBLANE
    last8 = t // SUBLANE - 1
    if reverse:
        blk = lambda i: nb - 1 - i
    else:
        blk = lambda i: i
    cur = pl.BlockSpec((None, tb, width), lambda b, i: (b, blk(i), col_block))
    prev = pl.BlockSpec((None, SUBLANE, width), lambda b, i: (b, jnp.maximum(blk(i) * per - 1, 0), col_block))
    nxt = pl.BlockSpec((None, SUBLANE, width), lambda b, i: (b, jnp.minimum((blk(i) + 1) * per, last8), col_block))
    return cur, prev, nxt


def _deltanet(hm3, conv_w, a_log, dt_bias, tb=512):
    bsz, t, _ = hm3.shape
    nb = t // tb
    qkv_w = 3 * 512
    qkvn = _dn_pre(hm3, conv_w)
    alog_v = jnp.zeros((1, LANE), F32).at[0, :8].set(a_log.reshape(-1))
    dtb_v = jnp.zeros((1, LANE), F32).at[0, :8].set(dt_bias.reshape(-1))
    ab_col = C_AB // LANE
    fwd = lambda b, i: (b, i, 0)
    bwd = lambda b, i: (b, nb - 1 - i, 0)
    in_specs = [pl.BlockSpec((None, tb, qkv_w), fwd), pl.BlockSpec((None, tb, LANE), lambda b, i: (b, i, ab_col)),
                pl.BlockSpec((None, tb, qkv_w), bwd),
                pl.BlockSpec((None, tb, LANE), lambda b, i: (b, nb - 1 - i, ab_col)),
                _const_spec((1, LANE)), _const_spec((1, LANE))]
    out_specs = [pl.BlockSpec((None, tb, 512), fwd), pl.BlockSpec((None, tb, 512), bwd)]
    return pl.pallas_call(
        _dn_kernel,
        grid=(bsz, nb),
        in_specs=in_specs,
        out_specs=out_specs,
        out_shape=[jax.ShapeDtypeStruct((bsz, t, 512), F32)] * 2,
        scratch_shapes=[pltpu.VMEM((2 * DN_HEADS, DN_DK, DN_DV), F32)],
        compiler_params=_cparams(("parallel", "arbitrary")),
        name="deltanet",
    )(qkvn, hm3, qkvn, hm3, alog_v, dtb_v)


def _s5_discretise(lam_re, lam_im, log_dt, b_re, b_im):
    dt = jnp.exp(log_dt)[:, None]
    mag = jnp.exp(lam_re * dt)
    ab_re = mag * jnp.cos(lam_im * dt)
    ab_im = mag * jnp.sin(lam_im * dt)
    den = jnp.square(lam_re) + jnp.square(lam_im)
    nr, ni = ab_re - 1.0, ab_im
    kr = ((nr * lam_re + ni * lam_im) / den)[..., None]
    ki = ((ni * lam_re - nr * lam_im) / den)[..., None]
    return kr * b_re - ki * b_im, kr * b_im + ki * b_re


def _s5_operators(lam_re, lam_im, log_dt, b_re, b_im, c_re, c_im):
    hi = lax.Precision.HIGHEST
    L, G, N, P = S5_L, S5_GROUPS, S5_N, S5_P
    j = jnp.arange(L + 1, dtype=F32)[:, None, None]
    kcomb = 0.0
    e_cols, f_rows, al = [], [], []
    for d in range(2):
        dt = jnp.exp(log_dt[d])[:, None]
        bb_re, bb_im = _s5_discretise(lam_re[d], lam_im[d], log_dt[d], b_re, b_im)
        mag = jnp.exp(lam_re[d] * dt * j)
        ang = lam_im[d] * dt * j
        aj_re, aj_im = mag * jnp.cos(ang), mag * jnp.sin(ang)
        ca_re = c_re[None] * aj_re[:, :, None, :] - c_im[None] * aj_im[:, :, None, :]
        ca_im = c_re[None] * aj_im[:, :, None, :] + c_im[None] * aj_re[:, :, None, :]
        kj = (jnp.einsum('jgpn,gnq->jgpq', ca_re[:L], bb_re, precision=hi)
              - jnp.einsum('jgpn,gnq->jgpq', ca_im[:L], bb_im, precision=hi))
        zeros = jnp.zeros((L - 1,) + kj.shape[1:], F32)
        if d == 0:
            kcomb = kcomb + jnp.concatenate([zeros, kj], axis=0)
        else:
            kcomb = kcomb + jnp.concatenate([kj[::-1], zeros], axis=0)
        pw_re = aj_re[:L][::-1] if d == 0 else aj_re[:L]
        pw_im = aj_im[:L][::-1] if d == 0 else aj_im[:L]
        e_re = pw_re[..., None] * bb_re[None] - pw_im[..., None] * bb_im[None]
        e_im = pw_re[..., None] * bb_im[None] + pw_im[..., None] * bb_re[None]
        for e in (e_re, e_im):
            e = e.transpose(1, 0, 3, 2).reshape(G, L * P, N)
            e_cols.append(jnp.pad(e, ((0, 0), (0, 0), (0, LANE - N))))
        sel = slice(1, L + 1)
        fr = ca_re[sel] if d == 0 else ca_re[sel][::-1]
        fi = ca_im[sel] if d == 0 else ca_im[sel][::-1]
        for f in (fr, -fi):
            f = f.transpose(1, 3, 0, 2).reshape(G, N, L * P)
            f_rows.append(jnp.pad(f, ((0, 0), (0, LANE - N), (0, 0))))
        al += [jnp.pad(aj_re[L], ((0, 0), (0, LANE - N))), jnp.pad(aj_im[L], ((0, 0), (0, LANE - N)))]
    kc = kcomb.astype(BF16).transpose(1, 3, 0, 2).reshape(G, P, (2 * L - 1) * P)
    toep = jnp.stack([kc[:, :, (L - 1 - s) * P:(2 * L - 1 - s) * P] for s in range(L)], axis=1)
    toep = toep.reshape(G, L * P, L * P)
    we = jnp.concatenate(e_cols, axis=2).astype(BF16)
    w2 = jnp.concatenate(f_rows, axis=1).astype(BF16)
    return toep, we, w2, jnp.stack(al, axis=1)


def _s5_kernel(x_ref, wt_ref, we_ref, w2_ref, al_ref, y_ref, u_scr, yv_scr, hloc_ref, hin_ref, *, n_chunks, nbs):
    g8 = pl.program_id(2)
    rows = nbs * n_chunks
    blk = lax.broadcasted_iota(jnp.int32, (S5_GPB, LANE), 1) // S5_P
    lane_tiles = S5_L // S5_GPB
    groups = range(S5_GPB)

    def position(rg, s):
        return pl.ds(rg * (SUBLANE * S5_L) + s, SUBLANE, stride=S5_L)

    def regroup(vregs):
        m = list(vregs)
        dist = S5_GPB // 2
        while dist:
            upper = (blk & dist) != 0
            nxt = list(m)
            for v in groups:
                if not v & dist:
                    w = v + dist
                    nxt[v] = jnp.where(upper, pltpu.roll(m[w], dist * S5_P, axis=1), m[v])
                    nxt[w] = jnp.where(upper, m[w], pltpu.roll(m[v], LANE - dist * S5_P, axis=1))
            m = nxt
            dist //= 2
        return m

    @pl.when(g8 == 0)
    def _():
        def gather_rows(rg, _):
            rsl = pl.ds(pl.multiple_of(rg * SUBLANE, SUBLANE), SUBLANE)
            for k in range(lane_tiles):
                pieces = regroup([x_ref[position(rg, S5_GPB * k + j), :] for j in groups])
                for g in groups:
                    u_scr[g, rsl, k * LANE:(k + 1) * LANE] = pieces[g]
            return 0

        lax.fori_loop(0, rows // SUBLANE, gather_rows, 0, unroll=2)

    u = u_scr[g8].astype(BF16)
    y = jnp.dot(u, wt_ref[...], preferred_element_type=F32)
    hloc = jnp.dot(u, we_ref[...], preferred_element_type=F32)
    for part in range(4):
        hloc_ref[part] = hloc[:, part * LANE:(part + 1) * LANE]
    al = al_ref[...]
    a_re = (al[0:1], al[2:3])
    a_im = (al[1:2], al[3:4])

    def body(cidx, carry):
        new = []
        for d in range(2):
            cr, ci = carry[2 * d], carry[2 * d + 1]
            cc = cidx if d == 0 else n_chunks - 1 - cidx
            seqs = pl.ds(cc, nbs, stride=n_chunks)
            hin_ref[2 * d, seqs, :] = cr
            hin_ref[2 * d + 1, seqs, :] = ci
            lr = hloc_ref[2 * d, seqs, :]
            li = hloc_ref[2 * d + 1, seqs, :]
            new += [a_re[d] * cr - a_im[d] * ci + lr, a_re[d] * ci + a_im[d] * cr + li]
        return tuple(new)

    zero = jnp.zeros((nbs, LANE), F32)
    lax.fori_loop(0, n_chunks, body, (zero, zero, zero, zero))
    hin = jnp.concatenate([hin_ref[part] for part in range(4)], axis=1).astype(BF16)
    yv_scr[g8] = y + jnp.dot(hin, w2_ref[...], preferred_element_type=F32)

    @pl.when(g8 == S5_GPB - 1)
    def _():
        def scatter_rows(rg, _):
            rsl = pl.ds(pl.multiple_of(rg * SUBLANE, SUBLANE), SUBLANE)
            for k in range(lane_tiles):
                pieces = regroup([yv_scr[g, rsl, k * LANE:(k + 1) * LANE] for g in groups])
                for j in groups:
                    y_ref[position(rg, S5_GPB * k + j), :] = pieces[j]
            return 0

        lax.fori_loop(0, rows // SUBLANE, scatter_rows, 0, unroll=2)


def _s5(hm, t, wt, we, w2, al):
    n = hm.shape[0]
    n_chunks = t // S5_L
    nbs = S5_ROWS // n_chunks
    tokens = S5_ROWS * S5_L
    group = lambda q, s, g: (q * S5_GPB + g, 0, 0)
    return pl.pallas_call(
        functools.partial(_s5_kernel, n_chunks=n_chunks, nbs=nbs),
        grid=(S5_WIDTH // LANE, n // tokens, S5_GPB),
        in_specs=[pl.BlockSpec((tokens, LANE), lambda q, s, g: (s, C_SU // LANE + q), pipeline_mode=pl.Buffered(1)),
                  pl.BlockSpec((None, S5_LW, S5_LW), group), pl.BlockSpec((None, S5_LW, S5_HW), group),
                  pl.BlockSpec((None, S5_HW, S5_LW), group),
                  pl.BlockSpec((None, 4, LANE), group)],
        out_specs=pl.BlockSpec((tokens, LANE), lambda q, s, g: (s, q)),
        out_shape=jax.ShapeDtypeStruct((n, S5_WIDTH), F32),
        scratch_shapes=[pltpu.VMEM((S5_GPB, S5_ROWS, S5_LW), F32), pltpu.VMEM((S5_GPB, S5_ROWS, S5_LW), F32),
                        pltpu.VMEM((4, S5_ROWS, LANE), F32), pltpu.VMEM((4, S5_ROWS, LANE), F32)],
        compiler_params=_cparams(("arbitrary", "arbitrary", "arbitrary")),
        name="s5",
    )(hm, wt, we, w2, al)


def _na_bias_table(rpb):
    c = np.arange(GRID_W)
    c0 = np.clip(c - NA_KW // 2, 0, GRID_W - NA_KW)
    col_in = (c[None, :] >= c0[:, None]) & (c[None, :] < c0[:, None] + NA_KW)
    dc = np.clip(c[None, :] - c[:, None], -(NA_KW - 1), NA_KW - 1) + (NA_KW - 1)
    onehot = (dc[:, :, None] == np.arange(2 * NA_KW - 1)).astype(np.float32)
    by_col = jnp.einsum('hrm,qkm->hrqk', rpb.astype(F32), onehot, precision=lax.Precision.HIGHEST)
    tab = jnp.stack([by_col[:, NA_KH - 1 - dl:2 * NA_KH - 1 - dl] for dl in range(NA_KH)], axis=1)
    tab = jnp.where(col_in[None, None, None], tab, -1e30)
    return tab.transpose(0, 1, 3, 2, 4).reshape(NA_HEADS, NA_KH, GRID_W, NA_KH * GRID_W)


def _na_kernel(q_ref, k_ref, v_ref, tab_ref, o_ref, *, rows_per_step, n_rows):
    i = pl.program_id(2)
    w = GRID_W
    nk = NA_KH * w
    lane = lax.broadcasted_iota(jnp.int32, (2 * w, LANE), 1)
    rowi = lax.broadcasted_iota(jnp.int32, (2 * w, LANE), 0)
    own = (lane < NA_DH) == (rowi < w)
    low = lax.broadcasted_iota(jnp.int32, (w, LANE), 1) < NA_DH
    steps = range(rows_per_step)
    r = [i * rows_per_step + rr for rr in steps]
    r0 = [jnp.clip(x - NA_KH // 2, 0, n_rows - NA_KH) for x in r]
    krows = [pl.ds(pl.multiple_of(x * w, w), nk) for x in r0]
    q2 = [q_ref[rr * w:(rr + 1) * w, :] for rr in steps]
    q2 = [jnp.where(own, jnp.concatenate([q, q], axis=0), jnp.zeros((2 * w, LANE), q.dtype)) for q in q2]
    s = [lax.dot_general(q, k_ref[kr, :], (((1,), (1,)), ((), ())), preferred_element_type=F32)
         for q, kr in zip(q2, krows)]
    s = [jnp.concatenate([x[:w] + tab_ref[0, a - b], x[w:] + tab_ref[1, a - b]], axis=0) for x, a, b in zip(s, r, r0)]
    m = [jnp.max(x, axis=-1, keepdims=True) for x in s]
    p = [jnp.exp(x - y) for x, y in zip(s, m)]
    l = [jnp.sum(x, axis=-1, keepdims=True) for x in p]
    pv = [jnp.dot(x.astype(BF16), v_ref[kr, :], preferred_element_type=F32) / y for x, kr, y in zip(p, krows, l)]
    for rr, x in zip(steps, pv):
        o_ref[rr * w:(rr + 1) * w, :] = jnp.where(low, x[:w], x[w:]).astype(o_ref.dtype)


def _natten(na3, table, rows_per_step=16):
    bsz, t, _ = na3.shape
    n_rows = t // GRID_W
    tq = rows_per_step * GRID_W
    pairs = NA_HEADS // 2
    kcol = NA_WIDTH // LANE
    return pl.pallas_call(
        functools.partial(_na_kernel, rows_per_step=rows_per_step, n_rows=n_rows),
        grid=(pairs, bsz, t // tq),
        in_specs=[pl.BlockSpec((None, tq, LANE), lambda p, b, i: (b, i, p)),
                  pl.BlockSpec((None, t, LANE), lambda p, b, i: (b, 0, kcol + p)),
                  pl.BlockSpec((None, t, LANE), lambda p, b, i: (b, 0, 2 * kcol + p)),
                  pl.BlockSpec((2, NA_KH, GRID_W, NA_KH * GRID_W), lambda p, b, i: (p, 0, 0, 0))],
        out_specs=pl.BlockSpec((None, tq, LANE), lambda p, b, i: (b, i, p)),
        out_shape=jax.ShapeDtypeStruct((bsz, t, NA_WIDTH), BF16),
        compiler_params=_cparams(("parallel", "parallel", "parallel")),
        name="natten",
    )(na3, na3, na3, table)


def _lru_coefficients(d, cur, prev8, next8, first, last, cw, cb, wg_ref, gb, sp_lam, a_scr, b_scr):
    xc = _conv_centred(prev8, cur, next8, cw, first, last) + cb
    width = 2 * LRU_WIDTH
    gates = jnp.dot(xc.astype(BF16), wg_ref[:, d * width:(d + 1) * width], preferred_element_type=F32)
    gates = _sigmoid(gates + gb[:, d * width:(d + 1) * width])
    log_a = -LRU_C * gates[:, :LRU_WIDTH] * sp_lam[d:d + 1]
    a = jnp.exp(log_a)
    a_scr[d] = a
    b_scr[d] = jnp.sqrt(1.0 - a * a) * gates[:, LRU_WIDTH:] * xc


def _lru_scan(a_scr, b_scr, carry_ref, o_refs):
    n_groups = a_scr.shape[1] // SUBLANE
    sub = lax.broadcasted_iota(jnp.int32, (SUBLANE, LRU_WIDTH), 0)

    def body(gi, hs):
        new = []
        for d in range(2):
            h = hs[d]
            grp = gi if d == 0 else n_groups - 1 - gi
            rows = pl.ds(pl.multiple_of(grp * SUBLANE, SUBLANE), SUBLANE)
            a8 = a_scr[d, rows, :]
            b8 = b_scr[d, rows, :]
            for s in (1, 2, 4):
                shift = s if d == 0 else SUBLANE - s
                has_prev = sub >= s if d == 0 else sub < SUBLANE - s
                a_prev = pltpu.roll(a8, shift, axis=0)
                b_prev = pltpu.roll(b8, shift, axis=0)
                b8 = jnp.where(has_prev, a8 * b_prev + b8, b8)
                a8 = jnp.where(has_prev, a8 * a_prev, a8)
            out = a8 * h + b8
            o_refs[d][rows, :] = out
            last = SUBLANE - 1 if d == 0 else 0
            new.append(out[last:last + 1])
        return tuple(new)

    hf, hb = lax.fori_loop(0, n_groups, body, (carry_ref[0:1], carry_ref[1:2]))
    carry_ref[0:1] = hf
    carry_ref[1:2] = hb


def _lru_kernel(cur_f, prev_f, next_f, cur_b, prev_b, next_b, cw_ref, cb_ref, wg_ref, gb_ref, lam_ref,
                hf_ref, hb_ref, a_scr, b_scr, carry_ref):
    i = pl.program_id(1)
    nb = pl.num_programs(1)

    @pl.when(i == 0)
    def _():
        carry_ref[...] = jnp.zeros_like(carry_ref)

    cw = cw_ref[...]
    cb = cb_ref[...]
    gb = gb_ref[...]
    sp_lam = jax.nn.softplus(-lam_ref[...])
    _lru_coefficients(0, cur_f[...], prev_f[...], next_f[...], i == 0, i == nb - 1, cw, cb, wg_ref, gb, sp_lam,
                      a_scr, b_scr)
    _lru_coefficients(1, cur_b[...], prev_b[...], next_b[...], i == nb - 1, i == 0, cw, cb, wg_ref, gb, sp_lam,
                      a_scr, b_scr)
    _lru_scan(a_scr, b_scr, carry_ref, (hf_ref, hb_ref))


def _lru_gate_matrix(gate_w):
    eye = jnp.eye(LRU_BLOCKS, dtype=gate_w.dtype)
    full = jnp.einsum('dgncm,nk->ncdgkm', gate_w, eye)
    return full.reshape(LRU_WIDTH, 4 * LRU_WIDTH)


def _rglru(hm3, conv_w, conv_b, wg, gate_b, lam, tb=512):
    bsz, t, _ = hm3.shape
    nb = t // tb
    col = C_LX // LRU_WIDTH
    in_specs = list(_halo_specs(tb, LRU_WIDTH, col, nb, t, False)) + list(_halo_specs(tb, LRU_WIDTH, col, nb, t, True))
    in_specs += [_const_spec((LRU_CONV, LRU_WIDTH)), _const_spec((1, LRU_WIDTH)),
                 _const_spec((LRU_WIDTH, 4 * LRU_WIDTH)), _const_spec((1, 4 * LRU_WIDTH)), _const_spec((2, LRU_WIDTH))]
    out_specs = [pl.BlockSpec((None, tb, LRU_WIDTH), lambda b, i: (b, i, 0)),
                 pl.BlockSpec((None, tb, LRU_WIDTH), lambda b, i: (b, nb - 1 - i, 0))]
    return pl.pallas_call(
        _lru_kernel,
        grid=(bsz, nb),
        in_specs=in_specs,
        out_specs=out_specs,
        out_shape=[jax.ShapeDtypeStruct((bsz, t, LRU_WIDTH), F32)] * 2,
        scratch_shapes=[pltpu.VMEM((2, tb, LRU_WIDTH), F32), pltpu.VMEM((2, tb, LRU_WIDTH), F32),
                        pltpu.VMEM((2, LRU_WIDTH), F32)],
        compiler_params=_cparams(("parallel", "arbitrary")),
        name="rglru",
    )(hm3, hm3, hm3, hm3, hm3, hm3, conv_w, conv_b.reshape(1, -1), wg, gate_b.reshape(1, -1), lam)


def _merge_kernel(x_ref, of_ref, ob_ref, z_ref, ys_ref, su_ref, na_ref, hf_ref, hb_ref, lg_ref,
                  ng_ref, sd_ref, gw_ref, gbias_ref, wgt_ref, wbr_ref, wout_ref, lng_ref, lnb_ref, o_ref):
    x = x_ref[...]
    xb = x.astype(BF16)
    o = of_ref[...] + ob_ref[...]
    z = z_ref[...]
    parts = []
    for h in range(DN_HEADS):
        oh = o[:, h * DN_DV:(h + 1) * DN_DV]
        ms = jnp.mean(oh * oh, axis=-1, keepdims=True)
        parts.append(oh * lax.rsqrt(ms + EPS) * ng_ref[...])
    y_a = jnp.concatenate(parts, axis=1) * (z * _sigmoid(z))
    y = jax.nn.gelu(ys_ref[...] + sd_ref[...] * su_ref[...])
    y_b = y * _sigmoid(jnp.dot(y.astype(BF16), gw_ref[...], preferred_element_type=F32) + gbias_ref[...])
    y_d = (hf_ref[...] + hb_ref[...]) * jax.nn.gelu(lg_ref[...])
    ys = (y_a.astype(BF16), y_b.astype(BF16), na_ref[...], y_d.astype(BF16))
    acc = None
    for n in range(N_BRANCH):
        gate = _sigmoid(jnp.dot(xb, wgt_ref[:, n * D_MODEL:(n + 1) * D_MODEL], preferred_element_type=F32))
        term = gate * jnp.dot(ys[n], wbr_ref[n], preferred_element_type=F32)
        acc = term if acc is None else acc + term
    mix = jnp.dot(acc.astype(BF16), wout_ref[...], preferred_element_type=F32)
    o_ref[...] = _layer_norm(ALPHA * x + mix, lng_ref[...], lnb_ref[...])


def _merge(x, o_f, o_b, hm, y_s5, na_o, h_f, h_b, norm_g, s5_d, glu_w, glu_b, w_gate, w_branch, w_out, ln_g, ln_b,
           tm=256):
    n = x.shape[0]
    tok = lambda w, cb=0: pl.BlockSpec((tm, w), lambda i: (i, cb))
    in_specs = [tok(D_MODEL), tok(512), tok(512), tok(512, C_Z // 512), tok(512), tok(512, C_SU // 512), tok(512),
                tok(512), tok(512), tok(512, C_LG // 512),
                _const_spec((1, DN_DV)), _const_spec((1, S5_WIDTH)), _const_spec((S5_WIDTH, S5_WIDTH)),
                _const_spec((1, S5_WIDTH)), _const_spec((D_MODEL, N_BRANCH * D_MODEL)),
                _const_spec((N_BRANCH, BRANCH_W, D_MODEL)), _const_spec((D_MODEL, D_MODEL)),
                _const_spec((1, D_MODEL)), _const_spec((1, D_MODEL))]
    return pl.pallas_call(
        _merge_kernel,
        grid=(n // tm,),
        in_specs=in_specs,
        out_specs=pl.BlockSpec((tm, D_MODEL), lambda i: (i, 0)),
        out_shape=jax.ShapeDtypeStruct((n, D_MODEL), F32),
        compiler_params=_cparams(("parallel",)),
        name="merge",
    )(x, o_f, o_b, hm, y_s5, hm, na_o, h_f, h_b, hm, norm_g.reshape(1, -1), s5_d.reshape(1, -1), glu_w,
      glu_b.reshape(1, -1), w_gate, w_branch, w_out, ln_g.reshape(1, -1), ln_b.reshape(1, -1))


def _mlp_kernel(x_ref, w1_ref, b1_ref, w2_ref, b2_ref, g_ref, b_ref, o_ref):
    x = x_ref[...]
    xb = x.astype(BF16)
    acc = None
    for c0 in range(0, D_FF, D_MODEL):
        f = jnp.dot(xb, w1_ref[:, c0:c0 + D_MODEL], preferred_element_type=F32) + b1_ref[:, c0:c0 + D_MODEL]
        f = jnp.square(jnp.maximum(f, 0.0))
        term = jnp.dot(f.astype(BF16), w2_ref[c0:c0 + D_MODEL, :], preferred_element_type=F32)
        acc = term if acc is None else acc + term
    o_ref[...] = _layer_norm(ALPHA * x + acc + b2_ref[...], g_ref[...], b_ref[...])


def _mlp(x, w1, b1, w2, b2, g, b, tm=512):
    n = x.shape[0]
    return pl.pallas_call(
        _mlp_kernel,
        grid=(n // tm,),
        in_specs=[pl.BlockSpec((tm, D_MODEL), lambda i: (i, 0)), _const_spec((D_MODEL, D_FF)), _const_spec((1, D_FF)),
                  _const_spec((D_FF, D_MODEL)), _const_spec((1, D_MODEL)), _const_spec((1, D_MODEL)),
                  _const_spec((1, D_MODEL))],
        out_specs=pl.BlockSpec((tm, D_MODEL), lambda i: (i, 0)),
        out_shape=jax.ShapeDtypeStruct((n, D_MODEL), F32),
        compiler_params=_cparams(("parallel",)),
        name="mlp",
    )(x, w1, b1.reshape(1, -1), w2, b2.reshape(1, -1), g.reshape(1, -1), b.reshape(1, -1))


def _prepare_layer(l, p):
    w_in = p['w_in'][l]
    col = lambda i: w_in[:, _IN_OFFS[i]:_IN_OFFS[i] + _IN_SPLITS[i]]
    pad = jnp.zeros((D_MODEL, W_MAIN - C_AB - 16), F32)
    w_main = jnp.concatenate([col(0), col(1), col(2), col(3), col(6), col(10), col(11), col(4), col(5), pad], axis=1)
    w_na = jnp.concatenate([col(7) * (NA_DH ** -0.5), col(8), col(9)], axis=1)
    s5_wt, s5_we, s5_w2, s5_al = _s5_operators(p['s5_lambda_re'][l], p['s5_lambda_im'][l], p['s5_log_dt'][l],
                                        p['s5_b_re'][l], p['s5_b_im'][l], p['s5_c_re'][l], p['s5_c_im'][l])
    return dict(
        w_main=w_main.astype(BF16), w_na=w_na.astype(BF16), w_gate=col(12).astype(BF16),
        dn_conv_w=p['dn_conv_w'][l], dn_a_log=p['dn_a_log'][l], dn_dt_bias=p['dn_dt_bias'][l],
        dn_norm_g=p['dn_norm_g'][l],
        s5_wt=s5_wt, s5_we=s5_we, s5_w2=s5_w2, s5_al=s5_al, s5_d=p['s5_d'][l], glu_w=p['s5_glu_w'][l].astype(BF16),
        glu_b=p['s5_glu_b'][l],
        na_table=_na_bias_table(p['na_rpb'][l]),
        lru_conv_w=p['lru_conv_w'][l], lru_conv_b=p['lru_conv_b'][l],
        lru_wg=_lru_gate_matrix(p['lru_gate_w'][l]).astype(BF16), lru_gate_b=p['lru_gate_b'][l],
        lru_lambda=p['lru_lambda'][l],
        w_branch=p['w_branch'][l].astype(BF16), w_out=p['w_out'][l].astype(BF16),
        ln1_g=p['ln1_g'][l], ln1_b=p['ln1_b'][l],
        mlp_w1=p['mlp_w1'][l].astype(BF16), mlp_b1=p['mlp_b1'][l], mlp_w2=p['mlp_w2'][l].astype(BF16),
        mlp_b2=p['mlp_b2'][l], ln2_g=p['ln2_g'][l], ln2_b=p['ln2_b'][l])


def _layer(x, bsz, t, lw, ln_in=None):
    if ln_in is None:
        hm, na = _in_proj(x, lw['w_main'], lw['w_na'])
    else:
        x, hm, na = _in_proj(x, lw['w_main'], lw['w_na'], ln=ln_in)
    hm3 = hm.reshape(bsz, t, W_MAIN)
    o_f, o_b = _deltanet(hm3, lw['dn_conv_w'], lw['dn_a_log'], lw['dn_dt_bias'])
    y_s5 = _s5(hm, t, lw['s5_wt'], lw['s5_we'], lw['s5_w2'], lw['s5_al'])
    na_o = _natten(na.reshape(bsz, t, 3 * NA_WIDTH), lw['na_table'])
    h_f, h_b = _rglru(hm3, lw['lru_conv_w'], lw['lru_conv_b'], lw['lru_wg'], lw['lru_gate_b'], lw['lru_lambda'])
    n = bsz * t
    x1 = _merge(x, o_f.reshape(n, 512), o_b.reshape(n, 512), hm, y_s5, na_o.reshape(n, NA_WIDTH),
                h_f.reshape(n, LRU_WIDTH), h_b.reshape(n, LRU_WIDTH), lw['dn_norm_g'], lw['s5_d'], lw['glu_w'],
                lw['glu_b'], lw['w_gate'], lw['w_branch'], lw['w_out'], lw['ln1_g'], lw['ln1_b'])
    return _mlp(x1, lw['mlp_w1'], lw['mlp_b1'], lw['mlp_w2'], lw['mlp_b2'], lw['ln2_g'], lw['ln2_b'])


def _trunk(x, ln_g, ln_b, layers):
    bsz, t, _ = x.shape
    h = x.reshape(bsz * t, D_MODEL)
    for l, lw in enumerate(layers):
        h = _layer(h, bsz, t, lw, ln_in=(ln_g, ln_b) if l == 0 else None)
    return h.reshape(bsz, t, D_MODEL)


def kernel(x_prompt, x_sample, ln_in_g, ln_in_b, w_in, dn_conv_w, dn_a_log, dn_dt_bias, dn_norm_g, s5_lambda_re,
           s5_lambda_im, s5_log_dt, s5_b_re, s5_b_im, s5_c_re, s5_c_im, s5_d, s5_glu_w, s5_glu_b, na_rpb, lru_conv_w,
           lru_conv_b, lru_gate_w, lru_gate_b, lru_lambda, w_branch, w_out, ln1_g, ln1_b, mlp_w1, mlp_b1, mlp_w2,
           mlp_b2, ln2_g, ln2_b):
    p = dict(w_in=w_in, dn_conv_w=dn_conv_w, dn_a_log=dn_a_log, dn_dt_bias=dn_dt_bias, dn_norm_g=dn_norm_g,
             s5_lambda_re=s5_lambda_re, s5_lambda_im=s5_lambda_im, s5_log_dt=s5_log_dt, s5_b_re=s5_b_re,
             s5_b_im=s5_b_im, s5_c_re=s5_c_re, s5_c_im=s5_c_im, s5_d=s5_d, s5_glu_w=s5_glu_w, s5_glu_b=s5_glu_b,
             na_rpb=na_rpb, lru_conv_w=lru_conv_w, lru_conv_b=lru_conv_b, lru_gate_w=lru_gate_w,
             lru_gate_b=lru_gate_b, lru_lambda=lru_lambda, w_branch=w_branch, w_out=w_out, ln1_g=ln1_g, ln1_b=ln1_b,
             mlp_w1=mlp_w1, mlp_b1=mlp_b1, mlp_w2=mlp_w2, mlp_b2=mlp_b2, ln2_g=ln2_g, ln2_b=ln2_b)
    layers = [_prepare_layer(l, p) for l in range(DEPTH)]
    return (_trunk(x_prompt, ln_in_g, ln_in_b, layers), _trunk(x_sample, ln_in_g, ln_in_b, layers))
```

```python
import functools
import math

import jax
import jax.numpy as jnp
import numpy as np
from jax import lax
from jax.experimental import pallas as pl
from jax.experimental.pallas import tpu as pltpu

F32 = jnp.float32
BF16 = jnp.bfloat16

D_MODEL = 1024
DEPTH = 2
GRID_W = 64
N_BRANCH = 4
BRANCH_W = 512
DN_HEADS = 4
DN_DK = 128
DN_DV = 128
DN_CONV = 4
DN_CHUNK = 64
S5_WIDTH = 512
S5_P = 16
S5_GROUPS = S5_WIDTH // S5_P
S5_N = 64
NA_HEADS = 8
NA_DH = 64
NA_WIDTH = NA_HEADS * NA_DH
NA_KH = 8
NA_KW = 16
LRU_WIDTH = 512
LRU_BLOCKS = 8
LRU_BW = LRU_WIDTH // LRU_BLOCKS
LRU_CONV = 4
LRU_C = 8.0
D_FF = 4 * D_MODEL
ALPHA = float((2 * DEPTH) ** 0.25)
EPS = 1e-5

_IN_SPLITS = (512, 512, 512, 512, 8, 8, 512, 512, 512, 512, 512, 512, 4096)
_IN_OFFS = tuple(sum(_IN_SPLITS[:i]) for i in range(len(_IN_SPLITS)))

LANE = 128
SUBLANE = 8
C_QKV = 0
C_Z = 1536
C_SU = 2048
C_LX = 2560
C_LG = 3072
C_AB = 3584
W_MAIN = 3712

S5_L = 32
S5_LW = S5_L * S5_P
S5_HW = 4 * LANE
S5_GPB = LANE // S5_P
S5_ROWS = 512

VMEM_LIMIT = 56 * 1024 * 1024


def _cparams(sem):
    return pltpu.CompilerParams(dimension_semantics=sem, vmem_limit_bytes=VMEM_LIMIT)


def _layer_norm(x, g, b):
    mu = jnp.mean(x, axis=-1, keepdims=True)
    xc = x - mu
    var = jnp.mean(xc * xc, axis=-1, keepdims=True)
    return xc * lax.rsqrt(var + EPS) * g + b


def _sigmoid(x):
    return 0.5 * jnp.tanh(0.5 * x) + 0.5


def _const_spec(shape):
    nd = len(shape)
    return pl.BlockSpec(shape, lambda *_: (0,) * nd, pipeline_mode=pl.Buffered(1))


def _project(x, wm_ref, wn_ref, hm_ref, na_ref):
    xb = x.astype(BF16)
    step = 4 * LANE
    for c0 in range(0, W_MAIN, step):
        c1 = min(c0 + step, W_MAIN)
        hm_ref[:, c0:c1] = jnp.dot(xb, wm_ref[:, c0:c1], preferred_element_type=F32)
    for c0 in range(0, 3 * NA_WIDTH, step):
        na_ref[:, c0:c0 + step] = jnp.dot(xb, wn_ref[:, c0:c0 + step], preferred_element_type=F32).astype(BF16)


def _proj_kernel(x_ref, wm_ref, wn_ref, hm_ref, na_ref):
    _project(x_ref[...], wm_ref, wn_ref, hm_ref, na_ref)


def _ln_proj_kernel(x_ref, g_ref, b_ref, wm_ref, wn_ref, xn_ref, hm_ref, na_ref):
    x = _layer_norm(x_ref[...], g_ref[...], b_ref[...])
    xn_ref[...] = x
    _project(x, wm_ref, wn_ref, hm_ref, na_ref)


def _in_proj(x, w_main, w_na, ln=None, tm=256):
    n = x.shape[0]
    tok = lambda w: pl.BlockSpec((tm, w), lambda i: (i, 0))
    w_specs = [_const_spec((D_MODEL, W_MAIN)), _const_spec((D_MODEL, 3 * NA_WIDTH))]
    out_specs = [tok(W_MAIN), tok(3 * NA_WIDTH)]
    out_shape = [jax.ShapeDtypeStruct((n, W_MAIN), F32), jax.ShapeDtypeStruct((n, 3 * NA_WIDTH), BF16)]
    if ln is None:
        return pl.pallas_call(
            _proj_kernel, grid=(n // tm,), in_specs=[tok(D_MODEL)] + w_specs, out_specs=out_specs,
            out_shape=out_shape, compiler_params=_cparams(("parallel",)), name="in_proj",
        )(x, w_main, w_na)
    return pl.pallas_call(
        _ln_proj_kernel, grid=(n // tm,),
        in_specs=[tok(D_MODEL), _const_spec((1, D_MODEL)), _const_spec((1, D_MODEL))] + w_specs,
        out_specs=[tok(D_MODEL)] + out_specs, out_shape=[jax.ShapeDtypeStruct((n, D_MODEL), F32)] + out_shape,
        compiler_params=_cparams(("parallel",)), name="ln_in_proj",
    )(x, ln[0].reshape(1, -1), ln[1].reshape(1, -1), w_main, w_na)


def _conv_centred(prev8, cur, next8, w, first, last):
    tb = cur.shape[0]
    prev8 = jnp.where(first, 0.0, prev8)
    next8 = jnp.where(last, 0.0, next8)
    xp = jnp.concatenate([prev8, cur, next8], axis=0)
    left = (w.shape[0] - 1) // 2
    acc = None
    for j in range(w.shape[0]):
        s = SUBLANE - left + j
        term = xp[s:s + tb] * w[j:j + 1]
        acc = term if acc is None else acc + term
    return acc


def _dot_bf(a, b):
    return jnp.dot(a.astype(BF16), b.astype(BF16), preferred_element_type=F32)


def _dot_nt_bf(a, b):
    return lax.dot_general(a.astype(BF16), b.astype(BF16), (((1,), (1,)), ((), ())), preferred_element_type=F32)


def _dn_pre_kernel(cur_ref, prev_ref, next_ref, cw_ref, o_ref):
    i = pl.program_id(1)
    nb = pl.num_programs(1)
    qkv = _conv_centred(prev_ref[...], cur_ref[...], next_ref[...], cw_ref[...], i == 0, i == nb - 1)
    qkv = qkv * _sigmoid(qkv)
    for h in range(DN_HEADS):
        q = qkv[:, h * DN_DK:(h + 1) * DN_DK]
        k = qkv[:, 512 + h * DN_DK:512 + (h + 1) * DN_DK]
        o_ref[:, h * DN_DK:(h + 1) * DN_DK] = q * (lax.rsqrt(jnp.sum(q * q, axis=-1, keepdims=True) + 1e-6)
                                                    * (DN_DK ** -0.5))
        o_ref[:, 512 + h * DN_DK:512 + (h + 1) * DN_DK] = k * lax.rsqrt(jnp.sum(k * k, axis=-1, keepdims=True) + 1e-6)
    o_ref[:, 1024:] = qkv[:, 1024:]


def _dn_pre(hm3, conv_w, tb=512):
    bsz, t, _ = hm3.shape
    nb = t // tb
    qkv_w = 3 * 512
    return pl.pallas_call(
        _dn_pre_kernel,
        grid=(bsz, nb),
        in_specs=list(_halo_specs(tb, qkv_w, 0, nb, t, False)) + [_const_spec((DN_CONV, qkv_w))],
        out_specs=pl.BlockSpec((None, tb, qkv_w), lambda b, i: (b, i, 0)),
        out_shape=jax.ShapeDtypeStruct((bsz, t, qkv_w), F32),
        compiler_params=_cparams(("parallel", "parallel")),
        name="deltanet_pre",
    )(hm3, hm3, hm3, conv_w)


def _dn_prepare(d, qkv, ab, alog, dtb):
    tb = qkv.shape[0]
    c = DN_CHUNK
    gates = -jnp.exp(alog) * jax.nn.softplus(ab + dtb)
    betas = _sigmoid(ab)

    row = lax.broadcasted_iota(jnp.int32, (c, c), 0)
    col = lax.broadcasted_iota(jnp.int32, (c, c), 1)
    incl = col <= row if d == 0 else col >= row
    strict = col < row if d == 0 else col > row
    eye = (row == col).astype(F32)
    edge = c - 1 if d == 0 else 0
    n_chunks = tb // c

    g_hi = gates.astype(BF16)
    rest = gates - g_hi.astype(F32)
    g_mid = rest.astype(BF16)
    g_lo = (rest - g_mid.astype(F32)).astype(BF16)
    g3 = jnp.concatenate([g_hi, g_mid, g_lo], axis=1)
    tri = incl.astype(BF16)
    gc_parts = [jnp.dot(tri, g3[ci * c:(ci + 1) * c], preferred_element_type=F32) for ci in range(n_chunks)]
    gc_all = jnp.concatenate([p[:, :LANE] + p[:, LANE:2 * LANE] + p[:, 2 * LANE:] for p in gc_parts], axis=0)

    units = [(ci, h) for ci in range(n_chunks) for h in range(DN_HEADS)]
    rows = lambda ci: slice(ci * c, (ci + 1) * c)
    lane_of = lambda h: d * DN_HEADS + h
    q_u = [qkv[rows(ci), h * DN_DK:(h + 1) * DN_DK] for ci, h in units]
    k_u = [qkv[rows(ci), 512 + h * DN_DK:512 + (h + 1) * DN_DK] for ci, h in units]
    v_u = [qkv[rows(ci), 1024 + h * DN_DV:1024 + (h + 1) * DN_DV] for ci, h in units]
    beta_u = [jnp.broadcast_to(betas[rows(ci), 8 + lane_of(h):9 + lane_of(h)], (c, DN_DK)) for ci, h in units]
    gcr_u = [jnp.broadcast_to(gc_all[rows(ci), lane_of(h):lane_of(h) + 1], (c, DN_DK)) for ci, h in units]
    gcl_u = [g.T[:c, :] for g in gcr_u]
    decay_u = [jnp.exp(jnp.where(incl, gr[:, :c] - gl, -1e30)) for gr, gl in zip(gcr_u, gcl_u)]
    egc_u = [jnp.exp(g) for g in gcr_u]
    glast_u = [g[edge:edge + 1, :] for g in gcr_u]
    kb_u = [k * b for k, b in zip(k_u, beta_u)]
    vb_u = [v * b for v, b in zip(v_u, beta_u)]
    kk_u = [_dot_nt_bf(jnp.concatenate([kb, q], axis=0), k) for kb, q, k in zip(kb_u, q_u, k_u)]
    x_u = [-jnp.where(strict, kk[:c] * dec, 0.0) for kk, dec in zip(kk_u, decay_u)]
    qk_u = [jnp.where(incl, kk[c:] * dec, 0.0).astype(BF16) for kk, dec in zip(kk_u, decay_u)]
    t_u = [eye + x for x in x_u]
    x_u = [_dot_bf(x, x) for x in x_u]
    for _ in range(4):
        both = [_dot_bf(jnp.concatenate([t, x], axis=0), x) for t, x in zip(t_u, x_u)]
        t_u = [t + b[:c] for t, b in zip(t_u, both)]
        x_u = [b[c:] for b in both]
    t_u = [t + _dot_bf(t, x) for t, x in zip(t_u, x_u)]
    sol_u = [_dot_bf(t, jnp.concatenate([vb, kb * e], axis=1)) for t, vb, kb, e in zip(t_u, vb_u, kb_u, egc_u)]
    u_u = [s[:, :DN_DV] for s in sol_u]
    wq_u = [jnp.concatenate([s[:, DN_DV:], q * e], axis=0).astype(BF16) for s, q, e in zip(sol_u, q_u, egc_u)]
    qkkd_u = [jnp.concatenate([qk, (k * jnp.exp(gl - g)).T.astype(BF16)], axis=0)
              for qk, k, gl, g in zip(qk_u, k_u, glast_u, gcr_u)]
    egl_u = [jnp.exp(gl) for gl in glast_u]
    return dict(zip(units, zip(u_u, wq_u, qkkd_u, egl_u)))


def _dn_kernel(cur_f, ab_f, cur_b, ab_b, alog_ref, dtb_ref, of_ref, ob_ref, s_ref):
    @pl.when(pl.program_id(1) == 0)
    def _():
        s_ref[...] = jnp.zeros_like(s_ref)

    alog = alog_ref[...]
    dtb = dtb_ref[...]
    prep = (_dn_prepare(0, cur_f[...], ab_f[...], alog, dtb), _dn_prepare(1, cur_b[...], ab_b[...], alog, dtb))
    c = DN_CHUNK
    n_chunks = cur_f.shape[0] // c
    o_refs = (of_ref, ob_ref)
    chains = [(d, h) for d in range(2) for h in range(DN_HEADS)]
    chunk_at = lambda d, step: step if d == 0 else n_chunks - 1 - step
    state = [s_ref[d * DN_HEADS + h] for d, h in chains]
    for step in range(n_chunks):
        ops = [prep[d][(chunk_at(d, step), h)] for d, h in chains]
        ws = [jnp.dot(wq, s.astype(BF16), preferred_element_type=F32) for (_, wq, _, _), s in zip(ops, state)]
        v_new = [(u - w[:c]).astype(BF16) for (u, _, _, _), w in zip(ops, ws)]
        upd = [jnp.dot(qkkd, vn, preferred_element_type=F32) for (_, _, qkkd, _), vn in zip(ops, v_new)]
        state = [s * egl + up[c:] for s, (_, _, _, egl), up in zip(state, ops, upd)]
        for (d, h), w, up in zip(chains, ws, upd):
            r0 = chunk_at(d, step) * c
            o_refs[d][r0:r0 + c, h * DN_DV:(h + 1) * DN_DV] = w[c:] + up[:c]
    for (d, h), s in zip(chains, state):
        s_ref[d * DN_HEADS + h] = s


def _halo_specs(tb, width, col_block, nb, t, reverse):
    per = tb // SUBLANE
    last8 = t // SUBLANE - 1
    if reverse:
        blk = lambda i: nb - 1 - i
    else:
        blk = lambda i: i
    cur = pl.BlockSpec((None, tb, width), lambda b, i: (b, blk(i), col_block))
    prev = pl.BlockSpec((None, SUBLANE, width), lambda b, i: (b, jnp.maximum(blk(i) * per - 1, 0), col_block))
    nxt = pl.BlockSpec((None, SUBLANE, width), lambda b, i: (b, jnp.minimum((blk(i) + 1) * per, last8), col_block))
    return cur, prev, nxt


def _deltanet(hm3, conv_w, a_log, dt_bias, tb=512):
    bsz, t, _ = hm3.shape
    nb = t // tb
    qkv_w = 3 * 512
    qkvn = _dn_pre(hm3, conv_w)
    alog_v = jnp.zeros((1, LANE), F32).at[0, :8].set(a_log.reshape(-1))
    dtb_v = jnp.zeros((1, LANE), F32).at[0, :8].set(dt_bias.reshape(-1))
    ab_col = C_AB // LANE
    fwd = lambda b, i: (b, i, 0)
    bwd = lambda b, i: (b, nb - 1 - i, 0)
    in_specs = [pl.BlockSpec((None, tb, qkv_w), fwd), pl.BlockSpec((None, tb, LANE), lambda b, i: (b, i, ab_col)),
                pl.BlockSpec((None, tb, qkv_w), bwd),
                pl.BlockSpec((None, tb, LANE), lambda b, i: (b, nb - 1 - i, ab_col)),
                _const_spec((1, LANE)), _const_spec((1, LANE))]
    out_specs = [pl.BlockSpec((None, tb, 512), fwd), pl.BlockSpec((None, tb, 512), bwd)]
    return pl.pallas_call(
        _dn_kernel,
        grid=(bsz, nb),
        in_specs=in_specs,
        out_specs=out_specs,
        out_shape=[jax.ShapeDtypeStruct((bsz, t, 512), F32)] * 2,
        scratch_shapes=[pltpu.VMEM((2 * DN_HEADS, DN_DK, DN_DV), F32)],
        compiler_params=_cparams(("parallel", "arbitrary")),
        name="deltanet",
    )(qkvn, hm3, qkvn, hm3, alog_v, dtb_v)


def _s5_discretise(lam_re, lam_im, log_dt, b_re, b_im):
    dt = jnp.exp(log_dt)[:, None]
    mag = jnp.exp(lam_re * dt)
    ab_re = mag * jnp.cos(lam_im * dt)
    ab_im = mag * jnp.sin(lam_im * dt)
    den = jnp.square(lam_re) + jnp.square(lam_im)
    nr, ni = ab_re - 1.0, ab_im
    kr = ((nr * lam_re + ni * lam_im) / den)[..., None]
    ki = ((ni * lam_re - nr * lam_im) / den)[..., None]
    return kr * b_re - ki * b_im, kr * b_im + ki * b_re


def _s5_operators(lam_re, lam_im, log_dt, b_re, b_im, c_re, c_im):
    hi = lax.Precision.HIGHEST
    L, G, N, P = S5_L, S5_GROUPS, S5_N, S5_P
    j = jnp.arange(L + 1, dtype=F32)[:, None, None]
    kcomb = 0.0
    e_cols, f_rows, al = [], [], []
    for d in range(2):
        dt = jnp.exp(log_dt[d])[:, None]
        bb_re, bb_im = _s5_discretise(lam_re[d], lam_im[d], log_dt[d], b_re, b_im)
        mag = jnp.exp(lam_re[d] * dt * j)
        ang = lam_im[d] * dt * j
        aj_re, aj_im = mag * jnp.cos(ang), mag * jnp.sin(ang)
        ca_re = c_re[None] * aj_re[:, :, None, :] - c_im[None] * aj_im[:, :, None, :]
        ca_im = c_re[None] * aj_im[:, :, None, :] + c_im[None] * aj_re[:, :, None, :]
        kj = (jnp.einsum('jgpn,gnq->jgpq', ca_re[:L], bb_re, precision=hi)
              - jnp.einsum('jgpn,gnq->jgpq', ca_im[:L], bb_im, precision=hi))
        zeros = jnp.zeros((L - 1,) + kj.shape[1:], F32)
        if d == 0:
            kcomb = kcomb + jnp.concatenate([zeros, kj], axis=0)
        else:
            kcomb = kcomb + jnp.concatenate([kj[::-1], zeros], axis=0)
        pw_re = aj_re[:L][::-1] if d == 0 else aj_re[:L]
        pw_im = aj_im[:L][::-1] if d == 0 else aj_im[:L]
        e_re = pw_re[..., None] * bb_re[None] - pw_im[..., None] * bb_im[None]
        e_im = pw_re[..., None] * bb_im[None] + pw_im[..., None] * bb_re[None]
        for e in (e_re, e_im):
            e = e.transpose(1, 0, 3, 2).reshape(G, L * P, N)
            e_cols.append(jnp.pad(e, ((0, 0), (0, 0), (0, LANE - N))))
        sel = slice(1, L + 1)
        fr = ca_re[sel] if d == 0 else ca_re[sel][::-1]
        fi = ca_im[sel] if d == 0 else ca_im[sel][::-1]
        for f in (fr, -fi):
            f = f.transpose(1, 3, 0, 2).reshape(G, N, L * P)
            f_rows.append(jnp.pad(f, ((0, 0), (0, LANE - N), (0, 0))))
        al += [jnp.pad(aj_re[L], ((0, 0), (0, LANE - N))), jnp.pad(aj_im[L], ((0, 0), (0, LANE - N)))]
    kc = kcomb.astype(BF16).transpose(1, 3, 0, 2).reshape(G, P, (2 * L - 1) * P)
    toep = jnp.stack([kc[:, :, (L - 1 - s) * P:(2 * L - 1 - s) * P] for s in range(L)], axis=1)
    toep = toep.reshape(G, L * P, L * P)
    we = jnp.concatenate(e_cols, axis=2).astype(BF16)
    w2 = jnp.concatenate(f_rows, axis=1).astype(BF16)
    return toep, we, w2, jnp.stack(al, axis=1)


def _s5_kernel(x_ref, wt_ref, we_ref, w2_ref, al_ref, y_ref, u_scr, yv_scr, hloc_ref, hin_ref, *, n_chunks, nbs):
    g8 = pl.program_id(2)
    rows = nbs * n_chunks
    blk = lax.broadcasted_iota(jnp.int32, (S5_GPB, LANE), 1) // S5_P
    lane_tiles = S5_L // S5_GPB
    groups = range(S5_GPB)

    def position(rg, s):
        return pl.ds(rg * (SUBLANE * S5_L) + s, SUBLANE, stride=S5_L)

    def regroup(vregs):
        m = list(vregs)
        dist = S5_GPB // 2
        while dist:
            upper = (blk & dist) != 0
            nxt = list(m)
            for v in groups:
                if not v & dist:
                    w = v + dist
                    nxt[v] = jnp.where(upper, pltpu.roll(m[w], dist * S5_P, axis=1), m[v])
                    nxt[w] = jnp.where(upper, m[w], pltpu.roll(m[v], LANE - dist * S5_P, axis=1))
            m = nxt
            dist //= 2
        return m

    @pl.when(g8 == 0)
    def _():
        def gather_rows(rg, _):
            rsl = pl.ds(pl.multiple_of(rg * SUBLANE, SUBLANE), SUBLANE)
            for k in range(lane_tiles):
                pieces = regroup([x_ref[position(rg, S5_GPB * k + j), :] for j in groups])
                for g in groups:
                    u_scr[g, rsl, k * LANE:(k + 1) * LANE] = pieces[g]
            return 0

        lax.fori_loop(0, rows // SUBLANE, gather_rows, 0, unroll=2)

    u = u_scr[g8].astype(BF16)
    y = jnp.dot(u, wt_ref[...], preferred_element_type=F32)
    hloc = jnp.dot(u, we_ref[...], preferred_element_type=F32)
    for part in range(4):
        hloc_ref[part] = hloc[:, part * LANE:(part + 1) * LANE]
    al = al_ref[...]
    a_re = (al[0:1], al[2:3])
    a_im = (al[1:2], al[3:4])

    def body(cidx, carry):
        new = []
        for d in range(2):
            cr, ci = carry[2 * d], carry[2 * d + 1]
            cc = cidx if d == 0 else n_chunks - 1 - cidx
            seqs = pl.ds(cc, nbs, stride=n_chunks)
            hin_ref[2 * d, seqs, :] = cr
            hin_ref[2 * d + 1, seqs, :] = ci
            lr = hloc_ref[2 * d, seqs, :]
            li = hloc_ref[2 * d + 1, seqs, :]
            new += [a_re[d] * cr - a_im[d] * ci + lr, a_re[d] * ci + a_im[d] * cr + li]
        return tuple(new)

    zero = jnp.zeros((nbs, LANE), F32)
    lax.fori_loop(0, n_chunks, body, (zero, zero, zero, zero), unroll=4)
    hin = jnp.concatenate([hin_ref[part] for part in range(4)], axis=1).astype(BF16)
    yv_scr[g8] = y + jnp.dot(hin, w2_ref[...], preferred_element_type=F32)

    @pl.when(g8 == S5_GPB - 1)
    def _():
        def scatter_rows(rg, _):
            rsl = pl.ds(pl.multiple_of(rg * SUBLANE, SUBLANE), SUBLANE)
            for k in range(lane_tiles):
                pieces = regroup([yv_scr[g, rsl, k * LANE:(k + 1) * LANE] for g in groups])
                for j in groups:
                    y_ref[position(rg, S5_GPB * k + j), :] = pieces[j]
            return 0

        lax.fori_loop(0, rows // SUBLANE, scatter_rows, 0, unroll=2)


def _s5(hm, t, wt, we, w2, al):
    n = hm.shape[0]
    n_chunks = t // S5_L
    nbs = S5_ROWS // n_chunks
    tokens = S5_ROWS * S5_L
    group = lambda q, s, g: (q * S5_GPB + g, 0, 0)
    return pl.pallas_call(
        functools.partial(_s5_kernel, n_chunks=n_chunks, nbs=nbs),
        grid=(S5_WIDTH // LANE, n // tokens, S5_GPB),
        in_specs=[pl.BlockSpec((tokens, LANE), lambda q, s, g: (s, C_SU // LANE + q), pipeline_mode=pl.Buffered(1)),
                  pl.BlockSpec((None, S5_LW, S5_LW), group), pl.BlockSpec((None, S5_LW, S5_HW), group),
                  pl.BlockSpec((None, S5_HW, S5_LW), group),
                  pl.BlockSpec((None, 4, LANE), group)],
        out_specs=pl.BlockSpec((tokens, LANE), lambda q, s, g: (s, q)),
        out_shape=jax.ShapeDtypeStruct((n, S5_WIDTH), F32),
        scratch_shapes=[pltpu.VMEM((S5_GPB, S5_ROWS, S5_LW), F32), pltpu.VMEM((S5_GPB, S5_ROWS, S5_LW), F32),
                        pltpu.VMEM((4, S5_ROWS, LANE), F32), pltpu.VMEM((4, S5_ROWS, LANE), F32)],
        compiler_params=_cparams(("arbitrary", "arbitrary", "arbitrary")),
        name="s5",
    )(hm, wt, we, w2, al)


def _na_bias_table(rpb):
    c = np.arange(GRID_W)
    c0 = np.clip(c - NA_KW // 2, 0, GRID_W - NA_KW)
    col_in = (c[None, :] >= c0[:, None]) & (c[None, :] < c0[:, None] + NA_KW)
    dc = np.clip(c[None, :] - c[:, None], -(NA_KW - 1), NA_KW - 1) + (NA_KW - 1)
    onehot = (dc[:, :, None] == np.arange(2 * NA_KW - 1)).astype(np.float32)
    by_col = jnp.einsum('hrm,qkm->hrqk', rpb.astype(F32), onehot, precision=lax.Precision.HIGHEST)
    tab = jnp.stack([by_col[:, NA_KH - 1 - dl:2 * NA_KH - 1 - dl] for dl in range(NA_KH)], axis=1)
    tab = jnp.where(col_in[None, None, None], tab, -1e30)
    return tab.transpose(0, 1, 3, 2, 4).reshape(NA_HEADS, NA_KH, GRID_W, NA_KH * GRID_W)


def _na_kernel(q_ref, k_ref, v_ref, tab_ref, o_ref, *, rows_per_step, n_rows):
    i = pl.program_id(2)
    w = GRID_W
    nk = NA_KH * w
    lane = lax.broadcasted_iota(jnp.int32, (2 * w, LANE), 1)
    rowi = lax.broadcasted_iota(jnp.int32, (2 * w, LANE), 0)
    own = (lane < NA_DH) == (rowi < w)
    low = lax.broadcasted_iota(jnp.int32, (w, LANE), 1) < NA_DH
    steps = range(rows_per_step)
    r = [i * rows_per_step + rr for rr in steps]
    r0 = [jnp.clip(x - NA_KH // 2, 0, n_rows - NA_KH) for x in r]
    krows = [pl.ds(pl.multiple_of(x * w, w), nk) for x in r0]
    q2 = [q_ref[rr * w:(rr + 1) * w, :] for rr in steps]
    q2 = [jnp.where(own, jnp.concatenate([q, q], axis=0), jnp.zeros((2 * w, LANE), q.dtype)) for q in q2]
    s = [lax.dot_general(q, k_ref[kr, :], (((1,), (1,)), ((), ())), preferred_element_type=F32)
         for q, kr in zip(q2, krows)]
    s = [jnp.concatenate([x[:w] + tab_ref[0, a - b], x[w:] + tab_ref[1, a - b]], axis=0) for x, a, b in zip(s, r, r0)]
    m = [jnp.max(x, axis=-1, keepdims=True) for x in s]
    p = [jnp.exp(x - y) for x, y in zip(s, m)]
    l = [jnp.sum(x, axis=-1, keepdims=True) for x in p]
    pv = [jnp.dot(x.astype(BF16), v_ref[kr, :], preferred_element_type=F32) / y for x, kr, y in zip(p, krows, l)]
    for rr, x in zip(steps, pv):
        o_ref[rr * w:(rr + 1) * w, :] = jnp.where(low, x[:w], x[w:]).astype(o_ref.dtype)


def _natten(na3, table, rows_per_step=32):
    bsz, t, _ = na3.shape
    n_rows = t // GRID_W
    tq = rows_per_step * GRID_W
    pairs = NA_HEADS // 2
    kcol = NA_WIDTH // LANE
    return pl.pallas_call(
        functools.partial(_na_kernel, rows_per_step=rows_per_step, n_rows=n_rows),
        grid=(pairs, bsz, t // tq),
        in_specs=[pl.BlockSpec((None, tq, LANE), lambda p, b, i: (b, i, p)),
                  pl.BlockSpec((None, t, LANE), lambda p, b, i: (b, 0, kcol + p)),
                  pl.BlockSpec((None, t, LANE), lambda p, b, i: (b, 0, 2 * kcol + p)),
                  pl.BlockSpec((2, NA_KH, GRID_W, NA_KH * GRID_W), lambda p, b, i: (p, 0, 0, 0))],
        out_specs=pl.BlockSpec((None, tq, LANE), lambda p, b, i: (b, i, p)),
        out_shape=jax.ShapeDtypeStruct((bsz, t, NA_WIDTH), BF16),
        compiler_params=_cparams(("parallel", "parallel", "parallel")),
        name="natten",
    )(na3, na3, na3, table)


def _lru_coefficients(d, cur, prev8, next8, first, last, cw, cb, wg_ref, gb, sp_lam, a_scr, b_scr):
    xc = _conv_centred(prev8, cur, next8, cw, first, last) + cb
    width = 2 * LRU_WIDTH
    gates = jnp.dot(xc.astype(BF16), wg_ref[:, d * width:(d + 1) * width], preferred_element_type=F32)
    gates = _sigmoid(gates + gb[:, d * width:(d + 1) * width])
    log_a = -LRU_C * gates[:, :LRU_WIDTH] * sp_lam[d:d + 1]
    a = jnp.exp(log_a)
    a_scr[d] = a
    b_scr[d] = jnp.sqrt(1.0 - a * a) * gates[:, LRU_WIDTH:] * xc


def _lru_scan(a_scr, b_scr, carry_ref, o_refs):
    n_groups = a_scr.shape[1] // SUBLANE
    sub = lax.broadcasted_iota(jnp.int32, (SUBLANE, LRU_WIDTH), 0)

    def body(gi, hs):
        new = []
        for d in range(2):
            h = hs[d]
            grp = gi if d == 0 else n_groups - 1 - gi
            rows = pl.ds(pl.multiple_of(grp * SUBLANE, SUBLANE), SUBLANE)
            a8 = a_scr[d, rows, :]
            b8 = b_scr[d, rows, :]
            for s in (1, 2, 4):
                shift = s if d == 0 else SUBLANE - s
                has_prev = sub >= s if d == 0 else sub < SUBLANE - s
                a_prev = pltpu.roll(a8, shift, axis=0)
                b_prev = pltpu.roll(b8, shift, axis=0)
                b8 = jnp.where(has_prev, a8 * b_prev + b8, b8)
                a8 = jnp.where(has_prev, a8 * a_prev, a8)
            out = a8 * h + b8
            o_refs[d][rows, :] = out
            last = SUBLANE - 1 if d == 0 else 0
            new.append(out[last:last + 1])
        return tuple(new)

    hf, hb = lax.fori_loop(0, n_groups, body, (carry_ref[0:1], carry_ref[1:2]))
    carry_ref[0:1] = hf
    carry_ref[1:2] = hb


def _lru_kernel(cur_f, prev_f, next_f, cur_b, prev_b, next_b, cw_ref, cb_ref, wg_ref, gb_ref, lam_ref,
                hf_ref, hb_ref, a_scr, b_scr, carry_ref):
    i = pl.program_id(1)
    nb = pl.num_programs(1)

    @pl.when(i == 0)
    def _():
        carry_ref[...] = jnp.zeros_like(carry_ref)

    cw = cw_ref[...]
    cb = cb_ref[...]
    gb = gb_ref[...]
    sp_lam = jax.nn.softplus(-lam_ref[...])
    _lru_coefficients(0, cur_f[...], prev_f[...], next_f[...], i == 0, i == nb - 1, cw, cb, wg_ref, gb, sp_lam,
                      a_scr, b_scr)
    _lru_coefficients(1, cur_b[...], prev_b[...], next_b[...], i == nb - 1, i == 0, cw, cb, wg_ref, gb, sp_lam,
                      a_scr, b_scr)
    _lru_scan(a_scr, b_scr, carry_ref, (hf_ref, hb_ref))


def _lru_gate_matrix(gate_w):
    eye = jnp.eye(LRU_BLOCKS, dtype=gate_w.dtype)
    full = jnp.einsum('dgncm,nk->ncdgkm', gate_w, eye)
    return full.reshape(LRU_WIDTH, 4 * LRU_WIDTH)


def _rglru(hm3, conv_w, conv_b, wg, gate_b, lam, tb=512):
    bsz, t, _ = hm3.shape
    nb = t // tb
    col = C_LX // LRU_WIDTH
    in_specs = list(_halo_specs(tb, LRU_WIDTH, col, nb, t, False)) + list(_halo_specs(tb, LRU_WIDTH, col, nb, t, True))
    in_specs += [_const_spec((LRU_CONV, LRU_WIDTH)), _const_spec((1, LRU_WIDTH)),
                 _const_spec((LRU_WIDTH, 4 * LRU_WIDTH)), _const_spec((1, 4 * LRU_WIDTH)), _const_spec((2, LRU_WIDTH))]
    out_specs = [pl.BlockSpec((None, tb, LRU_WIDTH), lambda b, i: (b, i, 0)),
                 pl.BlockSpec((None, tb, LRU_WIDTH), lambda b, i: (b, nb - 1 - i, 0))]
    return pl.pallas_call(
        _lru_kernel,
        grid=(bsz, nb),
        in_specs=in_specs,
        out_specs=out_specs,
        out_shape=[jax.ShapeDtypeStruct((bsz, t, LRU_WIDTH), F32)] * 2,
        scratch_shapes=[pltpu.VMEM((2, tb, LRU_WIDTH), F32), pltpu.VMEM((2, tb, LRU_WIDTH), F32),
                        pltpu.VMEM((2, LRU_WIDTH), F32)],
        compiler_params=_cparams(("parallel", "arbitrary")),
        name="rglru",
    )(hm3, hm3, hm3, hm3, hm3, hm3, conv_w, conv_b.reshape(1, -1), wg, gate_b.reshape(1, -1), lam)


def _merge_kernel(x_ref, of_ref, ob_ref, z_ref, ys_ref, su_ref, na_ref, hf_ref, hb_ref, lg_ref,
                  ng_ref, sd_ref, gw_ref, gbias_ref, wgt_ref, wbr_ref, wout_ref, lng_ref, lnb_ref, o_ref):
    x = x_ref[...]
    xb = x.astype(BF16)
    o = of_ref[...] + ob_ref[...]
    z = z_ref[...]
    parts = []
    for h in range(DN_HEADS):
        oh = o[:, h * DN_DV:(h + 1) * DN_DV]
        ms = jnp.mean(oh * oh, axis=-1, keepdims=True)
        parts.append(oh * lax.rsqrt(ms + EPS) * ng_ref[...])
    y_a = jnp.concatenate(parts, axis=1) * (z * _sigmoid(z))
    y = jax.nn.gelu(ys_ref[...] + sd_ref[...] * su_ref[...])
    y_b = y * _sigmoid(jnp.dot(y.astype(BF16), gw_ref[...], preferred_element_type=F32) + gbias_ref[...])
    y_d = (hf_ref[...] + hb_ref[...]) * jax.nn.gelu(lg_ref[...])
    ys = (y_a.astype(BF16), y_b.astype(BF16), na_ref[...], y_d.astype(BF16))
    acc = None
    for n in range(N_BRANCH):
        gate = _sigmoid(jnp.dot(xb, wgt_ref[:, n * D_MODEL:(n + 1) * D_MODEL], preferred_element_type=F32))
        term = gate * jnp.dot(ys[n], wbr_ref[n], preferred_element_type=F32)
        acc = term if acc is None else acc + term
    mix = jnp.dot(acc.astype(BF16), wout_ref[...], preferred_element_type=F32)
    o_ref[...] = _layer_norm(ALPHA * x + mix, lng_ref[...], lnb_ref[...])


def _merge(x, o_f, o_b, hm, y_s5, na_o, h_f, h_b, norm_g, s5_d, glu_w, glu_b, w_gate, w_branch, w_out, ln_g, ln_b,
           tm=256):
    n = x.shape[0]
    tok = lambda w, cb=0: pl.BlockSpec((tm, w), lambda i: (i, cb))
    in_specs = [tok(D_MODEL), tok(512), tok(512), tok(512, C_Z // 512), tok(512), tok(512, C_SU // 512), tok(512),
                tok(512), tok(512), tok(512, C_LG // 512),
                _const_spec((1, DN_DV)), _const_spec((1, S5_WIDTH)), _const_spec((S5_WIDTH, S5_WIDTH)),
                _const_spec((1, S5_WIDTH)), _const_spec((D_MODEL, N_BRANCH * D_MODEL)),
                _const_spec((N_BRANCH, BRANCH_W, D_MODEL)), _const_spec((D_MODEL, D_MODEL)),
                _const_spec((1, D_MODEL)), _const_spec((1, D_MODEL))]
    return pl.pallas_call(
        _merge_kernel,
        grid=(n // tm,),
        in_specs=in_specs,
        out_specs=pl.BlockSpec((tm, D_MODEL), lambda i: (i, 0)),
        out_shape=jax.ShapeDtypeStruct((n, D_MODEL), F32),
        compiler_params=_cparams(("parallel",)),
        name="merge",
    )(x, o_f, o_b, hm, y_s5, hm, na_o, h_f, h_b, hm, norm_g.reshape(1, -1), s5_d.reshape(1, -1), glu_w,
      glu_b.reshape(1, -1), w_gate, w_branch, w_out, ln_g.reshape(1, -1), ln_b.reshape(1, -1))


def _mlp_kernel(x_ref, w1_ref, b1_ref, w2_ref, b2_ref, g_ref, b_ref, o_ref):
    x = x_ref[...]
    xb = x.astype(BF16)
    acc = None
    for c0 in range(0, D_FF, D_MODEL):
        f = jnp.dot(xb, w1_ref[:, c0:c0 + D_MODEL], preferred_element_type=F32) + b1_ref[:, c0:c0 + D_MODEL]
        f = jnp.square(jnp.maximum(f, 0.0))
        term = jnp.dot(f.astype(BF16), w2_ref[c0:c0 + D_MODEL, :], preferred_element_type=F32)
        acc = term if acc is None else acc + term
    o_ref[...] = _layer_norm(ALPHA * x + acc + b2_ref[...], g_ref[...], b_ref[...])


def _mlp(x, w1, b1, w2, b2, g, b, tm=512):
    n = x.shape[0]
    return pl.pallas_call(
        _mlp_kernel,
        grid=(n // tm,),
        in_specs=[pl.BlockSpec((tm, D_MODEL), lambda i: (i, 0)), _const_spec((D_MODEL, D_FF)), _const_spec((1, D_FF)),
                  _const_spec((D_FF, D_MODEL)), _const_spec((1, D_MODEL)), _const_spec((1, D_MODEL)),
                  _const_spec((1, D_MODEL))],
        out_specs=pl.BlockSpec((tm, D_MODEL), lambda i: (i, 0)),
        out_shape=jax.ShapeDtypeStruct((n, D_MODEL), F32),
        compiler_params=_cparams(("parallel",)),
        name="mlp",
    )(x, w1, b1.reshape(1, -1), w2, b2.reshape(1, -1), g.reshape(1, -1), b.reshape(1, -1))


def _prepare_layer(l, p):
    w_in = p['w_in'][l]
    col = lambda i: w_in[:, _IN_OFFS[i]:_IN_OFFS[i] + _IN_SPLITS[i]]
    pad = jnp.zeros((D_MODEL, W_MAIN - C_AB - 16), F32)
    w_main = jnp.concatenate([col(0), col(1), col(2), col(3), col(6), col(10), col(11), col(4), col(5), pad], axis=1)
    w_na = jnp.concatenate([col(7) * (NA_DH ** -0.5), col(8), col(9)], axis=1)
    s5_wt, s5_we, s5_w2, s5_al = _s5_operators(p['s5_lambda_re'][l], p['s5_lambda_im'][l], p['s5_log_dt'][l],
                                        p['s5_b_re'][l], p['s5_b_im'][l], p['s5_c_re'][l], p['s5_c_im'][l])
    return dict(
        w_main=w_main.astype(BF16), w_na=w_na.astype(BF16), w_gate=col(12).astype(BF16),
        dn_conv_w=p['dn_conv_w'][l], dn_a_log=p['dn_a_log'][l], dn_dt_bias=p['dn_dt_bias'][l],
        dn_norm_g=p['dn_norm_g'][l],
        s5_wt=s5_wt, s5_we=s5_we, s5_w2=s5_w2, s5_al=s5_al, s5_d=p['s5_d'][l], glu_w=p['s5_glu_w'][l].astype(BF16),
        glu_b=p['s5_glu_b'][l],
        na_table=_na_bias_table(p['na_rpb'][l]),
        lru_conv_w=p['lru_conv_w'][l], lru_conv_b=p['lru_conv_b'][l],
        lru_wg=_lru_gate_matrix(p['lru_gate_w'][l]).astype(BF16), lru_gate_b=p['lru_gate_b'][l],
        lru_lambda=p['lru_lambda'][l],
        w_branch=p['w_branch'][l].astype(BF16), w_out=p['w_out'][l].astype(BF16),
        ln1_g=p['ln1_g'][l], ln1_b=p['ln1_b'][l],
        mlp_w1=p['mlp_w1'][l].astype(BF16), mlp_b1=p['mlp_b1'][l], mlp_w2=p['mlp_w2'][l].astype(BF16),
        mlp_b2=p['mlp_b2'][l], ln2_g=p['ln2_g'][l], ln2_b=p['ln2_b'][l])


def _layer(x, bsz, t, lw, ln_in=None):
    if ln_in is None:
        hm, na = _in_proj(x, lw['w_main'], lw['w_na'])
    else:
        x, hm, na = _in_proj(x, lw['w_main'], lw['w_na'], ln=ln_in)
    hm3 = hm.reshape(bsz, t, W_MAIN)
    o_f, o_b = _deltanet(hm3, lw['dn_conv_w'], lw['dn_a_log'], lw['dn_dt_bias'])
    y_s5 = _s5(hm, t, lw['s5_wt'], lw['s5_we'], lw['s5_w2'], lw['s5_al'])
    na_o = _natten(na.reshape(bsz, t, 3 * NA_WIDTH), lw['na_table'])
    h_f, h_b = _rglru(hm3, lw['lru_conv_w'], lw['lru_conv_b'], lw['lru_wg'], lw['lru_gate_b'], lw['lru_lambda'])
    n = bsz * t
    x1 = _merge(x, o_f.reshape(n, 512), o_b.reshape(n, 512), hm, y_s5, na_o.reshape(n, NA_WIDTH),
                h_f.reshape(n, LRU_WIDTH), h_b.reshape(n, LRU_WIDTH), lw['dn_norm_g'], lw['s5_d'], lw['glu_w'],
                lw['glu_b'], lw['w_gate'], lw['w_branch'], lw['w_out'], lw['ln1_g'], lw['ln1_b'])
    return _mlp(x1, lw['mlp_w1'], lw['mlp_b1'], lw['mlp_w2'], lw['mlp_b2'], lw['ln2_g'], lw['ln2_b'])


def _trunk(x, ln_g, ln_b, layers):
    bsz, t, _ = x.shape
    h = x.reshape(bsz * t, D_MODEL)
    for l, lw in enumerate(layers):
        h = _layer(h, bsz, t, lw, ln_in=(ln_g, ln_b) if l == 0 else None)
    return h.reshape(bsz, t, D_MODEL)


def kernel(x_prompt, x_sample, ln_in_g, ln_in_b, w_in, dn_conv_w, dn_a_log, dn_dt_bias, dn_norm_g, s5_lambda_re,
           s5_lambda_im, s5_log_dt, s5_b_re, s5_b_im, s5_c_re, s5_c_im, s5_d, s5_glu_w, s5_glu_b, na_rpb, lru_conv_w,
           lru_conv_b, lru_gate_w, lru_gate_b, lru_lambda, w_branch, w_out, ln1_g, ln1_b, mlp_w1, mlp_b1, mlp_w2,
           mlp_b2, ln2_g, ln2_b):
    p = dict(w_in=w_in, dn_conv_w=dn_conv_w, dn_a_log=dn_a_log, dn_dt_bias=dn_dt_bias, dn_norm_g=dn_norm_g,
             s5_lambda_re=s5_lambda_re, s5_lambda_im=s5_lambda_im, s5_log_dt=s5_log_dt, s5_b_re=s5_b_re,
             s5_b_im=s5_b_im, s5_c_re=s5_c_re, s5_c_im=s5_c_im, s5_d=s5_d, s5_glu_w=s5_glu_w, s5_glu_b=s5_glu_b,
             na_rpb=na_rpb, lru_conv_w=lru_conv_w, lru_conv_b=lru_conv_b, lru_gate_w=lru_gate_w,
             lru_gate_b=lru_gate_b, lru_lambda=lru_lambda, w_branch=w_branch, w_out=w_out, ln1_g=ln1_g, ln1_b=ln1_b,
             mlp_w1=mlp_w1, mlp_b1=mlp_b1, mlp_w2=mlp_w2, mlp_b2=mlp_b2, ln2_g=ln2_g, ln2_b=ln2_b)
    layers = [_prepare_layer(l, p) for l in range(DEPTH)]
    return (_trunk(x_prompt, ln_in_g, ln_in_b, layers), _trunk(x_sample, ln_in_g, ln_in_b, layers))
```
